```python
import jax, jax.numpy as jnp
from jax import lax
import numpy as np

D_MODEL = 2048
BATCH = 16
SEQ = 2048
DEPTH = 4

GRID_W = 64
CTX_LEN = 256
N_MIXERS = 2
N_POOL_LAYERS = (DEPTH + N_MIXERS - 1) // N_MIXERS
N_ATTN_LAYERS = DEPTH // N_MIXERS
POOL_WINDOWS = (2, 4, 8, 16)
N_POOL_GROUPS = len(POOL_WINDOWS)
POOL_GROUP_DIM = D_MODEL // N_POOL_GROUPS
HEAD_DIM = 64
N_HEADS = D_MODEL // HEAD_DIM
N_KV_HEADS = N_HEADS // 8
GQA_GROUP = N_HEADS // N_KV_HEADS
ATTN_DIM = N_HEADS * HEAD_DIM
KV_DIM = N_KV_HEADS * HEAD_DIM
WINDOW = 128
Q_BLOCK = 128
KEY_SPAN = Q_BLOCK + 2 * WINDOW
AXIS_ROPE_DIM = HEAD_DIM // 2
ROPE_BASE = 10000.0
D_FF = -(-8 * D_MODEL // (3 * 256)) * 256
RMS_EPS = 1e-6
NEG_INF = -1e30

kernel_name = 'hybrid_pool_swa_dit_trunk'


def rms_norm(x, g):
    xf = x.astype(jnp.float32)
    y = xf * lax.rsqrt(jnp.mean(xf * xf, axis=-1, keepdims=True) + RMS_EPS)
    return (y * g.astype(jnp.float32)).astype(x.dtype)


def modulate(x, shift, scale):
    return x * (1 + scale) + shift


def axial_rope_tables(L):
    rows_n = L // GRID_W
    row = jnp.repeat(jnp.arange(rows_n), GRID_W).astype(jnp.float32)
    col = jnp.tile(jnp.arange(GRID_W), rows_n).astype(jnp.float32)
    inv = 1.0 / (ROPE_BASE ** (jnp.arange(0, AXIS_ROPE_DIM, 2, dtype=jnp.float32) / AXIS_ROPE_DIM))
    ang_r = row[:, None] * inv[None, :]
    ang_c = col[:, None] * inv[None, :]
    ang = jnp.concatenate([ang_r, ang_r, ang_c, ang_c], axis=-1)
    return jnp.cos(ang), jnp.sin(ang)


def _rotate_half(x):
    x1, x2 = jnp.split(x, 2, axis=-1)
    return jnp.concatenate([-x2, x1], axis=-1)


def apply_axial_rope(x, cos, sin):
    L = x.shape[1]
    bshape = (L,) + (1,) * (x.ndim - 3) + (HEAD_DIM,)
    xf = x.astype(jnp.float32)
    rot = jnp.concatenate([_rotate_half(xf[..., :AXIS_ROPE_DIM]), _rotate_half(xf[..., AXIS_ROPE_DIM:])], axis=-1)
    return (xf * cos.reshape(bshape) + rot * sin.reshape(bshape)).astype(x.dtype)


def multiscale_pool_mixer(h, w_pool, pool_scale):
    B, L, _ = h.shape
    hf = h.astype(jnp.float32)
    cs = jnp.concatenate([jnp.zeros((B, 1, D_MODEL), jnp.float32), jnp.cumsum(hf, axis=1)], axis=1)
    t = jnp.arange(L)
    parts = []
    for g, w in enumerate(POOL_WINDOWS):
        lo = jnp.clip(t - w // 2, 0, L)
        hi = jnp.clip(t + w // 2, 0, L)
        sl = slice(g * POOL_GROUP_DIM, (g + 1) * POOL_GROUP_DIM)
        csg = cs[..., sl]
        mean = (csg[:, hi] - csg[:, lo]) / (hi - lo).astype(jnp.float32)[None, :, None]
        parts.append(mean - hf[..., sl])
    p = jnp.stack(parts, axis=2)
    y = jnp.einsum('blgc,gcd->blgd', p, w_pool.astype(jnp.float32)).reshape(B, L, D_MODEL)
    return (y * pool_scale.astype(jnp.float32)).astype(h.dtype)


def _split_qkv(t, w_qkv):
    B, n, _ = t.shape
    qkv = t @ w_qkv
    q = qkv[..., :ATTN_DIM].reshape(B, n, N_KV_HEADS, GQA_GROUP, HEAD_DIM)
    k = qkv[..., ATTN_DIM:ATTN_DIM + KV_DIM].reshape(B, n, N_KV_HEADS, HEAD_DIM)
    v = qkv[..., ATTN_DIM + KV_DIM:].reshape(B, n, N_KV_HEADS, HEAD_DIM)
    return q, k, v


def windowed_gqa_mixer(h, hc, w_qkv, w_o, sinks, cos, sin, with_ctx_out):
    B, L, _ = h.shape
    Lc = hc.shape[1]
    scale = HEAD_DIM ** -0.5
    q, k, v = _split_qkv(h, w_qkv)
    qc, kc, vc = _split_qkv(hc, w_qkv)
    q = apply_axial_rope(q, cos, sin) * scale
    k = apply_axial_rope(k, cos, sin)
    qc = qc * scale
    pad = ((0, 0), (WINDOW, WINDOW), (0, 0), (0, 0))
    kp = jnp.pad(k, pad)
    vp = jnp.pad(v, pad)
    sink = sinks.astype(jnp.float32).reshape(1, N_KV_HEADS, GQA_GROUP, 1, 1)

    def attend_block(b):
        start = b * Q_BLOCK
        qb = lax.dynamic_slice_in_dim(q, start, Q_BLOCK, axis=1)
        kb = lax.dynamic_slice_in_dim(kp, start, KEY_SPAN, axis=1)
        vb = lax.dynamic_slice_in_dim(vp, start, KEY_SPAN, axis=1)
        s_loc = jnp.einsum('bqhgd,bkhd->bhgqk', qb, kb, preferred_element_type=jnp.float32)
        qpos = start + jnp.arange(Q_BLOCK)
        kpos = start - WINDOW + jnp.arange(KEY_SPAN)
        valid = (jnp.abs(qpos[:, None] - kpos[None, :]) <= WINDOW) & (kpos >= 0)[None, :] & (kpos < L)[None, :]
        s_loc = jnp.where(valid, s_loc, NEG_INF)
        s_ctx = jnp.einsum('bqhgd,bkhd->bhgqk', qb, kc, preferred_element_type=jnp.float32)
        s_snk = jnp.broadcast_to(sink, s_loc.shape[:-1] + (1,))
        p = jax.nn.softmax(jnp.concatenate([s_loc, s_ctx, s_snk], axis=-1), axis=-1)
        o = (jnp.einsum('bhgqk,bkhd->bqhgd', p[..., :KEY_SPAN].astype(v.dtype), vb)
             + jnp.einsum('bhgqk,bkhd->bqhgd', p[..., KEY_SPAN:KEY_SPAN + Lc].astype(v.dtype), vc))
        return o

    o = lax.map(attend_block, jnp.arange(L // Q_BLOCK))
    o = jnp.moveaxis(o, 0, 1).reshape(B, L, ATTN_DIM)
    y = o @ w_o
    if not with_ctx_out:
        return y, None
    s = jnp.einsum('bqhgd,bkhd->bhgqk', qc, kc, preferred_element_type=jnp.float32)
    s_snk = jnp.broadcast_to(sink, s.shape[:-1] + (1,))
    p = jax.nn.softmax(jnp.concatenate([s, s_snk], axis=-1), axis=-1)
    oc = jnp.einsum('bhgqk,bkhd->bqhgd', p[..., :Lc].astype(vc.dtype), vc).reshape(B, Lc, ATTN_DIM)
    return y, oc @ w_o


def swiglu(h, w_gate_up, w_down):
    gu = h @ w_gate_up
    return (jax.nn.silu(gu[..., :D_FF]) * gu[..., D_FF:]) @ w_down


def setup_inputs(seed: int = 0) -> dict:
    key = jax.random.key(seed)
    ks = jax.random.split(key, 17)
    nrm = jax.random.normal
    f32 = jnp.float32
    return {
        'x': nrm(ks[0], (BATCH, SEQ, D_MODEL), f32),
        'c': nrm(ks[1], (BATCH, D_MODEL), f32),
        'ctx': nrm(ks[2], (BATCH, CTX_LEN, D_MODEL), f32),
        'c_ctx': nrm(ks[3], (D_MODEL,), f32),
        'w_ada': nrm(ks[4], (DEPTH, D_MODEL, 6 * D_MODEL), f32) * (0.5 * D_MODEL ** -0.5),
        'b_ada': 0.02 * nrm(ks[5], (DEPTH, 6 * D_MODEL), f32),
        'norm_pre_mix': 1.0 + 0.05 * nrm(ks[6], (DEPTH, D_MODEL), f32),
        'norm_post_mix': 1.0 + 0.05 * nrm(ks[7], (DEPTH, D_MODEL), f32),
        'norm_pre_ffn': 1.0 + 0.05 * nrm(ks[8], (DEPTH, D_MODEL), f32),
        'norm_post_ffn': 1.0 + 0.05 * nrm(ks[9], (DEPTH, D_MODEL), f32),
        'w_pool': nrm(ks[10], (N_POOL_LAYERS, N_POOL_GROUPS, POOL_GROUP_DIM, POOL_GROUP_DIM), f32) * POOL_GROUP_DIM ** -0.5,
        'pool_scale': 1.0 + 0.05 * nrm(ks[11], (N_POOL_LAYERS, D_MODEL), f32),
        'w_qkv': nrm(ks[12], (N_ATTN_LAYERS, D_MODEL, ATTN_DIM + 2 * KV_DIM), f32) * D_MODEL ** -0.5,
        'w_o': nrm(ks[13], (N_ATTN_LAYERS, ATTN_DIM, D_MODEL), f32) * ATTN_DIM ** -0.5,
        'attn_sinks': 0.5 * nrm(ks[14], (N_ATTN_LAYERS, N_HEADS), f32),
        'w_gate_up': nrm(ks[15], (DEPTH, D_MODEL, 2 * D_FF), f32) * D_MODEL ** -0.5,
        'w_down': nrm(ks[16], (DEPTH, D_FF, D_MODEL), f32) * D_FF ** -0.5,
    }


def reference(x, c, ctx, c_ctx, w_ada, b_ada, norm_pre_mix, norm_post_mix, norm_pre_ffn, norm_post_ffn,
              w_pool, pool_scale, w_qkv, w_o, attn_sinks, w_gate_up, w_down):
    L = x.shape[1]
    cos, sin = axial_rope_tables(L)
    silu_c = jax.nn.silu(c)
    silu_cc = jax.nn.silu(c_ctx)
    for i in range(DEPTH):
        last = i == DEPTH - 1
        ada = silu_c @ w_ada[i] + b_ada[i]
        sh1, sc1, g1, sh2, sc2, g2 = jnp.split(ada[:, None, :], 6, axis=-1)
        ada_c = silu_cc @ w_ada[i] + b_ada[i]
        csh1, csc1, cg1, csh2, csc2, cg2 = jnp.split(ada_c, 6, axis=-1)

        h = modulate(rms_norm(x, norm_pre_mix[i]), sh1, sc1)
        hc = modulate(rms_norm(ctx, norm_pre_mix[i]), csh1, csc1)
        j = i // N_MIXERS
        if i % N_MIXERS == 0:
            y = multiscale_pool_mixer(h, w_pool[j], pool_scale[j])
            yc = None if last else multiscale_pool_mixer(hc, w_pool[j], pool_scale[j])
        else:
            y, yc = windowed_gqa_mixer(h, hc, w_qkv[j], w_o[j], attn_sinks[j], cos, sin, not last)
        x = x + g1 * rms_norm(y, norm_post_mix[i])
        if not last:
            ctx = ctx + cg1 * rms_norm(yc, norm_post_mix[i])

        h = modulate(rms_norm(x, norm_pre_ffn[i]), sh2, sc2)
        x = x + g2 * rms_norm(swiglu(h, w_gate_up[i], w_down[i]), norm_post_ffn[i])
        if not last:
            hc = modulate(rms_norm(ctx, norm_pre_ffn[i]), csh2, csc2)
            ctx = ctx + cg2 * rms_norm(swiglu(hc, w_gate_up[i], w_down[i]), norm_post_ffn[i])
    return x
```

```python
import functools

import jax
import jax.numpy as jnp
from jax import lax
from jax.experimental import pallas as pl
from jax.experimental.pallas import tpu as pltpu

GRID_W = 64
POOL_WINDOWS = (2, 4, 8, 16)
HEAD_DIM = 64
GQA_GROUP = 8
WINDOW = 128
Q_BLOCK = 128
ROPE_BASE = 10000.0
RMS_EPS = 1e-6
NEG_INF = -1e30

LANES = 128
SUBLANES = 8
BF16_ROWS = 16
VMEM_LIMIT_BYTES = 56 * 1024 * 1024

POOL_HALO = SUBLANES
HEAD_PAIR = 2 * HEAD_DIM

F32 = jnp.float32
BF16 = jnp.bfloat16


def _params(*semantics):
    return pltpu.CompilerParams(dimension_semantics=semantics, vmem_limit_bytes=VMEM_LIMIT_BYTES)


def _resident(shape):
    nd = len(shape)
    return pl.BlockSpec(shape, lambda *_: (0,) * nd, pipeline_mode=pl.Buffered(1))


def _rms(x, w):
    ms = jnp.mean(x * x, axis=-1, keepdims=True)
    return x * lax.rsqrt(ms + RMS_EPS) * w


def _norm_mod(x, w, shift, scale):
    return _rms(x, w) * (1.0 + scale) + shift


def _ada_kernel(c_ref, w_ref, b_ref, o_ref):
    s = jax.nn.silu(c_ref[...]).astype(BF16)
    o_ref[0] = jnp.dot(s, w_ref[0].astype(BF16), preferred_element_type=F32) + b_ref[0]


def _ada_call(c_rows, w_ada, b_ada):
    depth, d, n = w_ada.shape
    rows = c_rows.shape[0]
    tn = 1024
    return pl.pallas_call(
        _ada_kernel,
        grid=(depth, n // tn),
        in_specs=[
            pl.BlockSpec((rows, d), lambda i, j: (0, 0)),
            pl.BlockSpec((1, d, tn), lambda i, j: (i, 0, j)),
            pl.BlockSpec((1, 1, tn), lambda i, j: (i, 0, j)),
        ],
        out_specs=pl.BlockSpec((1, rows, tn), lambda i, j: (i, 0, j)),
        out_shape=jax.ShapeDtypeStruct((depth, rows, n), F32),
        compiler_params=_params("parallel", "parallel"),
        name="ada_proj",
    )(c_rows, w_ada, b_ada.reshape(depth, 1, n))


def _ffn_kernel(x_ref, mod_ref, npre_ref, npost_ref, wg_ref, wu_ref, wd_ref, o_ref, h_ref):
    k = pl.program_id(2)

    @pl.when(k == 0)
    def _():
        h = _norm_mod(x_ref[0], npre_ref[...], mod_ref[0, 3:4, :], mod_ref[0, 4:5, :])
        h_ref[...] = h.astype(BF16)

    h = h_ref[...]
    g = jnp.dot(h, wg_ref[...], preferred_element_type=F32)
    u = jnp.dot(h, wu_ref[...], preferred_element_type=F32)
    a = (jax.nn.silu(g) * u).astype(BF16)
    part = jnp.dot(a, wd_ref[...], preferred_element_type=F32)

    @pl.when(k == 0)
    def _():
        o_ref[0] = part

    @pl.when(k > 0)
    def _():
        o_ref[0] += part

    @pl.when(k == pl.num_programs(2) - 1)
    def _():
        y = _rms(o_ref[0], npost_ref[...])
        o_ref[0] = x_ref[0] + mod_ref[0, 5:6, :] * y


def _ffn_call(x, mod, npre, npost, w_gate_up, w_down, tm, tf):
    b, l, d = x.shape
    f = w_down.shape[0]
    nk = f // tf
    return pl.pallas_call(
        _ffn_kernel,
        grid=(b, l // tm, nk),
        in_specs=[
            pl.BlockSpec((1, tm, d), lambda i, t, k: (i, t, 0)),
            pl.BlockSpec((1, 6, d), lambda i, t, k: (i, 0, 0)),
            pl.BlockSpec((1, d), lambda i, t, k: (0, 0)),
            pl.BlockSpec((1, d), lambda i, t, k: (0, 0)),
            pl.BlockSpec((d, tf), lambda i, t, k: (0, k)),
            pl.BlockSpec((d, tf), lambda i, t, k: (0, nk + k)),
            pl.BlockSpec((tf, d), lambda i, t, k: (k, 0)),
        ],
        out_specs=pl.BlockSpec((1, tm, d), lambda i, t, k: (i, t, 0)),
        out_shape=jax.ShapeDtypeStruct((b, l, d), F32),
        scratch_shapes=[pltpu.VMEM((tm, d), BF16)],
        compiler_params=_params("parallel", "parallel", "arbitrary"),
        name="swiglu",
    )(x, mod, npre, npost, w_gate_up, w_gate_up, w_down)


def _pool_kernel(x_ref, xp_ref, xn_ref, mod_ref, npre_ref, npost_ref, wp_ref, ps_ref, o_ref, h_ref, *, seq_len):
    t = pl.program_id(1)
    tm = x_ref.shape[1]
    gd = wp_ref.shape[1]
    d = x_ref.shape[2]
    npre = npre_ref[...]
    shift, scale = mod_ref[0, 0:1, :], mod_ref[0, 1:2, :]

    hp = _norm_mod(xp_ref[0], npre, shift, scale)
    hn = _norm_mod(xn_ref[0], npre, shift, scale)
    h_ref[0:POOL_HALO, :] = jnp.where(t > 0, hp, 0.0)
    h_ref[POOL_HALO:POOL_HALO + tm, :] = _norm_mod(x_ref[0], npre, shift, scale)
    h_ref[POOL_HALO + tm:, :] = jnp.where(t < pl.num_programs(1) - 1, hn, 0.0)

    pos = t * tm + lax.broadcasted_iota(jnp.int32, (tm, gd), 0)
    ss = jnp.zeros((tm, 1), F32)
    for g, w in enumerate(POOL_WINDOWS):
        cols = pl.ds(g * gd, gd)
        acc = h_ref[pl.ds(POOL_HALO - w // 2, tm), cols]
        for j in range(-w // 2 + 1, w // 2):
            acc = acc + h_ref[pl.ds(POOL_HALO + j, tm), cols]
        cnt = jnp.minimum(pos + w // 2, seq_len) - jnp.maximum(pos - w // 2, 0)
        p = acc / cnt.astype(F32) - h_ref[pl.ds(POOL_HALO, tm), cols]
        y = jnp.dot(p.astype(BF16), wp_ref[g], preferred_element_type=F32) * ps_ref[:, cols]
        o_ref[0, :, cols] = y
        ss = ss + jnp.sum(y * y, axis=-1, keepdims=True)

    rstd = lax.rsqrt(ss * (1.0 / d) + RMS_EPS)
    o_ref[0] = x_ref[0] + mod_ref[0, 2:3, :] * (o_ref[0] * rstd * npost_ref[...])


def _pool_call(x, mod, npre, npost, w_pool, pool_scale, tm):
    b, l, d = x.shape
    hb = tm // POOL_HALO
    last_hb = l // POOL_HALO - 1
    return pl.pallas_call(
        functools.partial(_pool_kernel, seq_len=l),
        grid=(b, l // tm),
        in_specs=[
            pl.BlockSpec((1, tm, d), lambda i, t: (i, t, 0)),
            pl.BlockSpec((1, POOL_HALO, d), lambda i, t: (i, jnp.maximum(t * hb - 1, 0), 0)),
            pl.BlockSpec((1, POOL_HALO, d), lambda i, t: (i, jnp.minimum((t + 1) * hb, last_hb), 0)),
            pl.BlockSpec((1, 6, d), lambda i, t: (i, 0, 0)),
            pl.BlockSpec((1, d), lambda i, t: (0, 0)),
            pl.BlockSpec((1, d), lambda i, t: (0, 0)),
            _resident(w_pool.shape),
            pl.BlockSpec((1, d), lambda i, t: (0, 0)),
        ],
        out_specs=pl.BlockSpec((1, tm, d), lambda i, t: (i, t, 0)),
        out_shape=jax.ShapeDtypeStruct((b, l, d), F32),
        scratch_shapes=[pltpu.VMEM((tm + 2 * POOL_HALO, d), F32)],
        compiler_params=_params("parallel", "parallel"),
        name="pool_mixer",
    )(x, x, x, mod, npre, npost, w_pool, pool_scale)


def _rope(x, cos, sin_signed, low_half):
    out = []
    for j in range(x.shape[1] // LANES):
        c = x[:, j * LANES:(j + 1) * LANES]
        rot = jnp.where(low_half, pltpu.roll(c, LANES - HEAD_DIM // 4, 1), pltpu.roll(c, HEAD_DIM // 4, 1))
        out.append(c * cos + rot * sin_signed)
    return jnp.concatenate(out, axis=1)


def _qkv_kernel(*refs, rope, q_dim, kv_dim):
    if rope:
        x_ref, mod_ref, npre_ref, w_ref, cos_ref, sin_ref, q_ref, k_ref, v_ref = refs
    else:
        x_ref, mod_ref, npre_ref, w_ref, q_ref, k_ref, v_ref = refs
    h = _norm_mod(x_ref[0], npre_ref[...], mod_ref[0, 0:1, :], mod_ref[0, 1:2, :]).astype(BF16)
    qkv = jnp.dot(h, w_ref[...], preferred_element_type=F32)
    q = qkv[:, :q_dim]
    k = qkv[:, q_dim:q_dim + kv_dim]
    v = qkv[:, q_dim + kv_dim:]
    if rope:
        cos, sin_signed = cos_ref[...], sin_ref[...]
        lane = lax.broadcasted_iota(jnp.int32, cos.shape, 1)
        low_half = (lane % (HEAD_DIM // 2)) < (HEAD_DIM // 4)
        q = _rope(q, cos, sin_signed, low_half)
        k = _rope(k, cos, sin_signed, low_half)
    q_ref[0] = (q * (HEAD_DIM ** -0.5)).astype(BF16)
    k_ref[0] = k.astype(BF16)
    v_ref[0] = v.astype(BF16)


def _qkv_call(x, mod, npre, w_qkv_dup, q_dim, rope_tables, tm):
    b, l, d = x.shape
    n = w_qkv_dup.shape[1]
    kv_dim = (n - q_dim) // 2
    rope = rope_tables is not None
    in_specs = [
        pl.BlockSpec((1, tm, d), lambda i, t: (i, t, 0)),
        pl.BlockSpec((1, 6, d), lambda i, t: (i, 0, 0)),
        pl.BlockSpec((1, d), lambda i, t: (0, 0)),
        _resident(w_qkv_dup.shape),
    ]
    args = [x, mod, npre, w_qkv_dup]
    if rope:
        in_specs += [pl.BlockSpec((tm, LANES), lambda i, t: (t, 0))] * 2
        args += list(rope_tables)
    return pl.pallas_call(
        functools.partial(_qkv_kernel, rope=rope, q_dim=q_dim, kv_dim=kv_dim),
        grid=(b, l // tm),
        in_specs=in_specs,
        out_specs=[
            pl.BlockSpec((1, tm, q_dim), lambda i, t: (i, t, 0)),
            pl.BlockSpec((1, tm, kv_dim), lambda i, t: (i, t, 0)),
            pl.BlockSpec((1, tm, kv_dim), lambda i, t: (i, t, 0)),
        ],
        out_shape=[
            jax.ShapeDtypeStruct((b, l, q_dim), BF16),
            jax.ShapeDtypeStruct((b, l, kv_dim), BF16),
            jax.ShapeDtypeStruct((b, l, kv_dim), BF16),
        ],
        compiler_params=_params("parallel", "parallel"),
        name="qkv_rope" if rope else "qkv_ctx",
    )(*args)


def _attend_kv_head(q_ref, o_ref, sink_ref, h, key_sets):
    nq = q_ref.shape[1]
    lane = lax.broadcasted_iota(jnp.int32, (nq, HEAD_PAIR), 1)
    low = lane < HEAD_DIM
    zero = jnp.zeros((), BF16)
    base = h * GQA_GROUP * HEAD_DIM
    order = list(range(0, GQA_GROUP, 2)) + list(range(1, GQA_GROUP, 2))
    q_rows, sink_rows = [], []
    for g in order:
        pair = q_ref[0, :, pl.ds(base + (g // 2) * HEAD_PAIR, HEAD_PAIR)]
        q_rows.append(jnp.where(low if g % 2 == 0 else ~low, pair, zero))
        sink_rows.append(jnp.full((nq, 1), sink_ref[h * GQA_GROUP + g], F32))
    qg = jnp.concatenate(q_rows, axis=0)
    snk = jnp.concatenate(sink_rows, axis=0)

    nt = (((1,), (1,)), ((), ()))
    scores = []
    m = snk
    for k, _, valid in key_sets:
        s = lax.dot_general(qg, k, nt, preferred_element_type=F32)
        if valid is not None:
            s = jnp.where(valid, s, NEG_INF)
        scores.append(s)
        m = jnp.maximum(m, jnp.max(s, axis=-1, keepdims=True))
    den = jnp.exp(snk - m)
    half = (GQA_GROUP // 2) * nq
    klane = lax.broadcasted_iota(jnp.int32, (1, HEAD_PAIR), 1) < HEAD_DIM
    o_pairs = jnp.zeros((half, HEAD_PAIR), F32)
    for s, (_, v, _) in zip(scores, key_sets):
        p = jnp.exp(s - m)
        den = den + jnp.sum(p, axis=-1, keepdims=True)
        p = p.astype(BF16)
        o_pairs = o_pairs + jnp.dot(p[:half], jnp.where(klane, v, zero), preferred_element_type=F32)
        o_pairs = o_pairs + jnp.dot(p[half:], jnp.where(klane, zero, v), preferred_element_type=F32)
    inv_pairs = jnp.where(lax.broadcasted_iota(jnp.int32, (half, HEAD_PAIR), 1) < HEAD_DIM,
                          1.0 / den[:half], 1.0 / den[half:])
    o_pairs = o_pairs * inv_pairs
    for j in range(GQA_GROUP // 2):
        o_ref[0, :, pl.ds(base + j * HEAD_PAIR, HEAD_PAIR)] = o_pairs[j * nq:(j + 1) * nq].astype(o_ref.dtype)


def _attn_kernel(sink_ref, q_ref, kp_ref, kc_ref, kn_ref, vp_ref, vc_ref, vn_ref, kx_ref, vx_ref, o_ref):
    i = pl.program_id(1)
    nq = Q_BLOCK
    rows = GQA_GROUP * nq
    r = lax.broadcasted_iota(jnp.int32, (rows, 3 * nq), 0) % nq
    col = lax.broadcasted_iota(jnp.int32, (rows, 3 * nq), 1)
    valid = (col >= r) & (col <= r + 2 * WINDOW)
    valid = valid & ((col >= nq) | (i > 0)) & ((col < 2 * nq) | (i < pl.num_programs(1) - 1))
    k_loc = jnp.concatenate([kp_ref[0], kc_ref[0], kn_ref[0]], axis=0)
    v_loc = jnp.concatenate([vp_ref[0], vc_ref[0], vn_ref[0]], axis=0)
    for h in range(k_loc.shape[1] // HEAD_PAIR):
        cols = slice(h * HEAD_PAIR, (h + 1) * HEAD_PAIR)
        key_sets = [(k_loc[:, cols], v_loc[:, cols], valid),
                    (kx_ref[0, :, cols], vx_ref[0, :, cols], None)]
        _attend_kv_head(q_ref, o_ref, sink_ref, h, key_sets)


def _attn_call(q, k, v, kx, vx, sinks):
    b, l, qd = q.shape
    kvd = k.shape[2]
    lc = kx.shape[1]
    nb = l // Q_BLOCK
    kv_spec = lambda f: pl.BlockSpec((1, Q_BLOCK, kvd), f)
    prev = lambda i, t: (i, jnp.maximum(t - 1, 0), 0)
    cur = lambda i, t: (i, t, 0)
    nxt = lambda i, t: (i, jnp.minimum(t + 1, nb - 1), 0)
    return pl.pallas_call(
        _attn_kernel,
        grid=(b, nb),
        in_specs=[
            pl.BlockSpec(memory_space=pltpu.SMEM),
            pl.BlockSpec((1, Q_BLOCK, qd), cur),
            kv_spec(prev), kv_spec(cur), kv_spec(nxt),
            kv_spec(prev), kv_spec(cur), kv_spec(nxt),
            pl.BlockSpec((1, lc, kvd), lambda i, t: (i, 0, 0)),
            pl.BlockSpec((1, lc, kvd), lambda i, t: (i, 0, 0)),
        ],
        out_specs=pl.BlockSpec((1, Q_BLOCK, qd), cur),
        out_shape=jax.ShapeDtypeStruct((b, l, qd), BF16),
        compiler_params=_params("parallel", "parallel"),
        name="window_attn",
    )(sinks, q, k, k, k, v, v, v, kx, vx)


def _ctx_attn_kernel(sink_ref, q_ref, k_ref, v_ref, o_ref):
    for h in range(k_ref.shape[2] // HEAD_PAIR):
        cols = slice(h * HEAD_PAIR, (h + 1) * HEAD_PAIR)
        _attend_kv_head(q_ref, o_ref, sink_ref, h, [(k_ref[0, :, cols], v_ref[0, :, cols], None)])


def _ctx_attn_call(q, k, v, sinks):
    b, lc, qd = q.shape
    kvd = k.shape[2]
    return pl.pallas_call(
        _ctx_attn_kernel,
        grid=(b,),
        in_specs=[
            pl.BlockSpec(memory_space=pltpu.SMEM),
            pl.BlockSpec((1, lc, qd), lambda i: (i, 0, 0)),
            pl.BlockSpec((1, lc, kvd), lambda i: (i, 0, 0)),
            pl.BlockSpec((1, lc, kvd), lambda i: (i, 0, 0)),
        ],
        out_specs=pl.BlockSpec((1, lc, qd), lambda i: (i, 0, 0)),
        out_shape=jax.ShapeDtypeStruct((b, lc, qd), BF16),
        compiler_params=_params("parallel"),
        name="ctx_attn",
    )(sinks, q, k, v)


def _oproj_kernel(a_ref, x_ref, mod_ref, npost_ref, w_ref, o_ref):
    y = jnp.dot(a_ref[0], w_ref[...], preferred_element_type=F32)
    o_ref[0] = x_ref[0] + mod_ref[0, 2:3, :] * _rms(y, npost_ref[...])


def _oproj_call(a, x, mod, npost, w_o, tm):
    b, l, d = x.shape
    ad = a.shape[2]
    return pl.pallas_call(
        _oproj_kernel,
        grid=(b, l // tm),
        in_specs=[
            pl.BlockSpec((1, tm, ad), lambda i, t: (i, t, 0)),
            pl.BlockSpec((1, tm, d), lambda i, t: (i, t, 0)),
            pl.BlockSpec((1, 6, d), lambda i, t: (i, 0, 0)),
            pl.BlockSpec((1, d), lambda i, t: (0, 0)),
            _resident(w_o.shape),
        ],
        out_specs=pl.BlockSpec((1, tm, d), lambda i, t: (i, t, 0)),
        out_shape=jax.ShapeDtypeStruct((b, l, d), F32),
        compiler_params=_params("parallel", "parallel"),
        name="attn_out_proj",
    )(a, x, mod, npost, w_o)


def _rope_tables(l):
    axis_dim = HEAD_DIM // 2
    rows_n = l // GRID_W
    row = jnp.repeat(jnp.arange(rows_n), GRID_W).astype(F32)
    col = jnp.tile(jnp.arange(GRID_W), rows_n).astype(F32)
    inv = 1.0 / (ROPE_BASE ** (jnp.arange(0, axis_dim, 2, dtype=F32) / axis_dim))
    ang_r = row[:, None] * inv[None, :]
    ang_c = col[:, None] * inv[None, :]
    ang = jnp.concatenate([ang_r, ang_r, ang_c, ang_c], axis=-1)
    sign = jnp.tile(jnp.concatenate([-jnp.ones(axis_dim // 2, F32), jnp.ones(axis_dim // 2, F32)]), 2)
    cos, sin = jnp.cos(ang), jnp.sin(ang) * sign[None, :]
    return jnp.tile(cos, (1, LANES // HEAD_DIM)), jnp.tile(sin, (1, LANES // HEAD_DIM))


def _dup_heads(w, n_heads):
    d = w.shape[0]
    w = w.reshape(d, n_heads, 1, HEAD_DIM)
    return jnp.broadcast_to(w, (d, n_heads, 2, HEAD_DIM)).reshape(d, n_heads * HEAD_PAIR)


def _token_tile(l, target):
    return min(l, target)


def kernel(x, c, ctx, c_ctx, w_ada, b_ada, norm_pre_mix, norm_post_mix, norm_pre_ffn, norm_post_ffn,
           w_pool, pool_scale, w_qkv, w_o, attn_sinks, w_gate_up, w_down):
    b, l, d = x.shape
    depth = w_ada.shape[0]
    n_mixers = 2
    q_dim = w_o.shape[1]
    kv_heads = (w_qkv.shape[2] - q_dim) // (2 * HEAD_DIM)
    kv_dim = kv_heads * HEAD_DIM

    rows = -(-(b + 1) // BF16_ROWS) * BF16_ROWS
    c_rows = jnp.concatenate([c, c_ctx[None, :], jnp.zeros((rows - b - 1, d), F32)], axis=0)
    ada = _ada_call(c_rows, w_ada, b_ada).reshape(depth, rows, 6, d)
    mod_x = ada[:, :b]
    mod_c = jnp.broadcast_to(ada[:, b:b + 1], (depth, b, 6, d))

    w_gu_bf = w_gate_up.astype(BF16)
    w_dn_bf = w_down.astype(BF16)
    w_pool_bf = w_pool.astype(BF16)
    w_o_bf = w_o.astype(BF16)
    w_qkv_bf = w_qkv.astype(BF16)
    w_qkv_dup = jnp.concatenate([
        w_qkv_bf[:, :, :q_dim],
        jax.vmap(lambda w: _dup_heads(w, kv_heads))(w_qkv_bf[:, :, q_dim:q_dim + kv_dim]),
        jax.vmap(lambda w: _dup_heads(w, kv_heads))(w_qkv_bf[:, :, q_dim + kv_dim:]),
    ], axis=-1)
    tables = _rope_tables(l)

    lc = ctx.shape[1]
    tm_x = _token_tile(l, 512)
    tm_c = _token_tile(b * lc, 512)
    tf = 512

    flat = lambda a: a.reshape(1, b * lc, a.shape[-1])
    unflat = lambda a: a.reshape(b, lc, a.shape[-1])

    for i in range(depth):
        last = i == depth - 1
        j = i // n_mixers
        npre, npost = norm_pre_mix[i][None, :], norm_post_mix[i][None, :]
        if i % n_mixers == 0:
            ps = pool_scale[j][None, :]
            x = _pool_call(x, mod_x[i], npre, npost, w_pool_bf[j], ps, tm_x)
            if not last:
                ctx = _pool_call(ctx, mod_c[i], npre, npost, w_pool_bf[j], ps, _token_tile(lc, 512))
        else:
            q, k, v = _qkv_call(x, mod_x[i], npre, w_qkv_dup[j], q_dim, tables, tm_x)
            qc, kc, vc = _qkv_call(flat(ctx), mod_c[i][:1], npre, w_qkv_dup[j], q_dim, None, tm_c)
            qc, kc, vc = unflat(qc), unflat(kc), unflat(vc)
            a = _attn_call(q, k, v, kc, vc, attn_sinks[j])
            x = _oproj_call(a, x, mod_x[i], npost, w_o_bf[j], tm_x)
            if not last:
                ac = _ctx_attn_call(qc, kc, vc, attn_sinks[j])
                ctx = unflat(_oproj_call(flat(ac), flat(ctx), mod_c[i][:1], npost, w_o_bf[j], tm_c))

        npre, npost = norm_pre_ffn[i][None, :], norm_post_ffn[i][None, :]
        x = _ffn_call(x, mod_x[i], npre, npost, w_gu_bf[i], w_dn_bf[i], tm_x, tf)
        if not last:
            ctx = unflat(_ffn_call(flat(ctx), mod_c[i][:1], npre, npost, w_gu_bf[i], w_dn_bf[i], tm_c, tf))
    return x
```

```python
import functools

import jax
import jax.numpy as jnp
from jax import lax
from jax.experimental import pallas as pl
from jax.experimental.pallas import tpu as pltpu

GRID_W = 64
POOL_WINDOWS = (2, 4, 8, 16)
HEAD_DIM = 64
GQA_GROUP = 8
WINDOW = 128
Q_BLOCK = 128
ROPE_BASE = 10000.0
RMS_EPS = 1e-6
NEG_INF = -1e30

LANES = 128
SUBLANES = 8
BF16_ROWS = 16
VMEM_LIMIT_BYTES = 56 * 1024 * 1024

POOL_HALO = SUBLANES
HEAD_PAIR = 2 * HEAD_DIM

F32 = jnp.float32
BF16 = jnp.bfloat16


def _params(*semantics):
    return pltpu.CompilerParams(dimension_semantics=semantics, vmem_limit_bytes=VMEM_LIMIT_BYTES)


def _resident(shape):
    nd = len(shape)
    return pl.BlockSpec(shape, lambda *_: (0,) * nd, pipeline_mode=pl.Buffered(1))


def _rms(x, w):
    ms = jnp.mean(x * x, axis=-1, keepdims=True)
    return x * lax.rsqrt(ms + RMS_EPS) * w


def _norm_mod(x, w, shift, scale):
    return _rms(x, w * (1.0 + scale)) + shift


def _ada_kernel(c_ref, w_ref, b_ref, o_ref):
    s = jax.nn.silu(c_ref[...]).astype(BF16)
    o_ref[0] = jnp.dot(s, w_ref[0].astype(BF16), preferred_element_type=F32) + b_ref[0]


def _ada_call(c_rows, w_ada, b_ada):
    depth, d, n = w_ada.shape
    rows = c_rows.shape[0]
    tn = 1024
    return pl.pallas_call(
        _ada_kernel,
        grid=(depth, n // tn),
        in_specs=[
            pl.BlockSpec((rows, d), lambda i, j: (0, 0)),
            pl.BlockSpec((1, d, tn), lambda i, j: (i, 0, j)),
            pl.BlockSpec((1, 1, tn), lambda i, j: (i, 0, j)),
        ],
        out_specs=pl.BlockSpec((1, rows, tn), lambda i, j: (i, 0, j)),
        out_shape=jax.ShapeDtypeStruct((depth, rows, n), F32),
        compiler_params=_params("parallel", "parallel"),
        name="ada_proj",
    )(c_rows, w_ada, b_ada.reshape(depth, 1, n))


def _ffn_kernel(x_ref, mod_ref, npre_ref, npost_ref, wg_ref, wu_ref, wd_ref, o_ref, h_ref):
    k = pl.program_id(2)

    @pl.when(k == 0)
    def _():
        h = _norm_mod(x_ref[0], npre_ref[...], mod_ref[0, 3:4, :], mod_ref[0, 4:5, :])
        h_ref[...] = h.astype(BF16)
        o_ref[0] = jnp.zeros(o_ref.shape[1:], F32)

    h = h_ref[...]
    g = jnp.dot(h, wg_ref[...], preferred_element_type=F32)
    u = jnp.dot(h, wu_ref[...], preferred_element_type=F32)
    a = (jax.nn.silu(g) * u).astype(BF16)
    o_ref[0] += jnp.dot(a, wd_ref[...], preferred_element_type=F32)

    @pl.when(k == pl.num_programs(2) - 1)
    def _():
        o_ref[0] = x_ref[0] + _rms(o_ref[0], mod_ref[0, 5:6, :] * npost_ref[...])


def _ffn_call(x, mod, npre, npost, w_gate_up, w_down, tm, tf):
    b, l, d = x.shape
    f = w_down.shape[0]
    nk = f // tf
    return pl.pallas_call(
        _ffn_kernel,
        grid=(b, l // tm, nk),
        in_specs=[
            pl.BlockSpec((1, tm, d), lambda i, t, k: (i, t, 0)),
            pl.BlockSpec((1, 6, d), lambda i, t, k: (i, 0, 0)),
            pl.BlockSpec((1, d), lambda i, t, k: (0, 0)),
            pl.BlockSpec((1, d), lambda i, t, k: (0, 0)),
            pl.BlockSpec((d, tf), lambda i, t, k: (0, k)),
            pl.BlockSpec((d, tf), lambda i, t, k: (0, nk + k)),
            pl.BlockSpec((tf, d), lambda i, t, k: (k, 0)),
        ],
        out_specs=pl.BlockSpec((1, tm, d), lambda i, t, k: (i, t, 0)),
        out_shape=jax.ShapeDtypeStruct((b, l, d), F32),
        scratch_shapes=[pltpu.VMEM((tm, d), BF16)],
        compiler_params=_params("parallel", "parallel", "arbitrary"),
        name="swiglu",
    )(x, mod, npre, npost, w_gate_up, w_gate_up, w_down)


def _pool_kernel(x_ref, xp_ref, xn_ref, mod_ref, npre_ref, npost_ref, wp_ref, ps_ref, o_ref, h_ref, *, seq_len):
    t = pl.program_id(1)
    tm = x_ref.shape[1]
    gd = wp_ref.shape[1]
    d = x_ref.shape[2]
    npre = npre_ref[...]
    shift, scale = mod_ref[0, 0:1, :], mod_ref[0, 1:2, :]

    hp = _norm_mod(xp_ref[0], npre, shift, scale)
    hn = _norm_mod(xn_ref[0], npre, shift, scale)
    h_ref[0:POOL_HALO, :] = jnp.where(t > 0, hp, 0.0)
    h_ref[POOL_HALO:POOL_HALO + tm, :] = _norm_mod(x_ref[0], npre, shift, scale)
    h_ref[POOL_HALO + tm:, :] = jnp.where(t < pl.num_programs(1) - 1, hn, 0.0)

    pos = t * tm + lax.broadcasted_iota(jnp.int32, (tm, gd), 0)
    ss = jnp.zeros((tm, 1), F32)
    for g, w in enumerate(POOL_WINDOWS):
        cols = pl.ds(g * gd, gd)
        acc = h_ref[pl.ds(POOL_HALO - w // 2, tm), cols]
        for j in range(-w // 2 + 1, w // 2):
            acc = acc + h_ref[pl.ds(POOL_HALO + j, tm), cols]
        cnt = jnp.minimum(pos + w // 2, seq_len) - jnp.maximum(pos - w // 2, 0)
        p = acc / cnt.astype(F32) - h_ref[pl.ds(POOL_HALO, tm), cols]
        y = jnp.dot(p.astype(BF16), wp_ref[g], preferred_element_type=F32) * ps_ref[:, cols]
        o_ref[0, :, cols] = y
        ss = ss + jnp.sum(y * y, axis=-1, keepdims=True)

    rstd = lax.rsqrt(ss * (1.0 / d) + RMS_EPS)
    o_ref[0] = x_ref[0] + o_ref[0] * rstd * (mod_ref[0, 2:3, :] * npost_ref[...])


def _pool_call(x, mod, npre, npost, w_pool, pool_scale, tm):
    b, l, d = x.shape
    hb = tm // POOL_HALO
    last_hb = l // POOL_HALO - 1
    return pl.pallas_call(
        functools.partial(_pool_kernel, seq_len=l),
        grid=(b, l // tm),
        in_specs=[
            pl.BlockSpec((1, tm, d), lambda i, t: (i, t, 0)),
            pl.BlockSpec((1, POOL_HALO, d), lambda i, t: (i, jnp.maximum(t * hb - 1, 0), 0)),
            pl.BlockSpec((1, POOL_HALO, d), lambda i, t: (i, jnp.minimum((t + 1) * hb, last_hb), 0)),
            pl.BlockSpec((1, 6, d), lambda i, t: (i, 0, 0)),
            pl.BlockSpec((1, d), lambda i, t: (0, 0)),
            pl.BlockSpec((1, d), lambda i, t: (0, 0)),
            _resident(w_pool.shape),
            pl.BlockSpec((1, d), lambda i, t: (0, 0)),
        ],
        out_specs=pl.BlockSpec((1, tm, d), lambda i, t: (i, t, 0)),
        out_shape=jax.ShapeDtypeStruct((b, l, d), F32),
        scratch_shapes=[pltpu.VMEM((tm + 2 * POOL_HALO, d), F32)],
        compiler_params=_params("parallel", "parallel"),
        name="pool_mixer",
    )(x, x, x, mod, npre, npost, w_pool, pool_scale)


def _rope(x, cos, sin_signed, low_half):
    out = []
    for j in range(x.shape[1] // LANES):
        c = x[:, j * LANES:(j + 1) * LANES]
        rot = jnp.where(low_half, pltpu.roll(c, LANES - HEAD_DIM // 4, 1), pltpu.roll(c, HEAD_DIM // 4, 1))
        out.append(c * cos + rot * sin_signed)
    return jnp.concatenate(out, axis=1)


def _qkv_kernel(*refs, rope, q_dim, kv_dim):
    if rope:
        x_ref, mod_ref, npre_ref, w_ref, cos_ref, sin_ref, q_ref, k_ref, vt_ref = refs
    else:
        x_ref, mod_ref, npre_ref, w_ref, q_ref, k_ref, vt_ref = refs
    h = _norm_mod(x_ref[0], npre_ref[...], mod_ref[0, 0:1, :], mod_ref[0, 1:2, :]).astype(BF16)
    qkv = jnp.dot(h, w_ref[...], preferred_element_type=F32)
    q = qkv[:, :q_dim]
    k = qkv[:, q_dim:q_dim + kv_dim]
    v = qkv[:, q_dim + kv_dim:]
    if rope:
        cos, sin_signed = cos_ref[...], sin_ref[...]
        lane = lax.broadcasted_iota(jnp.int32, cos.shape, 1)
        low_half = (lane % (HEAD_DIM // 2)) < (HEAD_DIM // 4)
        q = _rope(q, cos, sin_signed, low_half)
        k = _rope(k, cos, sin_signed, low_half)
    q_ref[0] = (q * (HEAD_DIM ** -0.5)).astype(BF16)
    k_ref[0] = k.astype(BF16)
    vt_ref[0] = v.T.astype(BF16)


def _qkv_call(x, mod, npre, w_qkv_dup, q_dim, rope_tables, tm):
    b, l, d = x.shape
    n = w_qkv_dup.shape[1]
    kv_dim = (n - q_dim) // 2
    rope = rope_tables is not None
    in_specs = [
        pl.BlockSpec((1, tm, d), lambda i, t: (i, t, 0)),
        pl.BlockSpec((1, 6, d), lambda i, t: (i, 0, 0)),
        pl.BlockSpec((1, d), lambda i, t: (0, 0)),
        _resident(w_qkv_dup.shape),
    ]
    args = [x, mod, npre, w_qkv_dup]
    if rope:
        in_specs += [pl.BlockSpec((tm, LANES), lambda i, t: (t, 0))] * 2
        args += list(rope_tables)
    return pl.pallas_call(
        functools.partial(_qkv_kernel, rope=rope, q_dim=q_dim, kv_dim=kv_dim),
        grid=(b, l // tm),
        in_specs=in_specs,
        out_specs=[
            pl.BlockSpec((1, tm, q_dim), lambda i, t: (i, t, 0)),
            pl.BlockSpec((1, tm, kv_dim), lambda i, t: (i, t, 0)),
            pl.BlockSpec((1, kv_dim, tm), lambda i, t: (i, 0, t)),
        ],
        out_shape=[
            jax.ShapeDtypeStruct((b, l, q_dim), BF16),
            jax.ShapeDtypeStruct((b, l, kv_dim), BF16),
            jax.ShapeDtypeStruct((b, kv_dim, l), BF16),
        ],
        compiler_params=_params("parallel", "parallel"),
        name="qkv_rope" if rope else "qkv_ctx",
    )(*args)


def _attend_kv_head(q_ref, o_ref, sink_ref, h, k, vt, valid):
    nq = q_ref.shape[1]
    low = lax.broadcasted_iota(jnp.int32, (nq, HEAD_PAIR), 1) < HEAD_DIM
    zero = jnp.zeros((), BF16)
    base = h * GQA_GROUP * HEAD_DIM
    q_rows, sink_cols = [], []
    for g in range(GQA_GROUP):
        pair = q_ref[0, :, pl.ds(base + (g // 2) * HEAD_PAIR, HEAD_PAIR)]
        q_rows.append(jnp.where(low if g % 2 == 0 else ~low, pair, zero))
        sink_cols.append(jnp.full((1, nq), sink_ref[h * GQA_GROUP + g], F32))
    qg = jnp.concatenate(q_rows, axis=0)
    snk = jnp.concatenate(sink_cols, axis=1)

    s = lax.dot_general(k, qg, (((1,), (1,)), ((), ())), preferred_element_type=F32)
    if valid is not None:
        s = jnp.where(valid, s, NEG_INF)
    m = jnp.maximum(snk, jnp.max(s, axis=0, keepdims=True))
    p = jnp.exp(s - m)
    den = jnp.exp(snk - m) + jnp.sum(p, axis=0, keepdims=True)
    acc = jnp.dot(vt, p.astype(BF16), preferred_element_type=F32) * (1.0 / den)
    top = lax.broadcasted_iota(jnp.int32, (HEAD_PAIR, nq), 0) < HEAD_DIM
    for j in range(GQA_GROUP // 2):
        pair_t = jnp.where(top, acc[:, (2 * j) * nq:(2 * j + 1) * nq], acc[:, (2 * j + 1) * nq:(2 * j + 2) * nq])
        o_ref[0, :, pl.ds(base + j * HEAD_PAIR, HEAD_PAIR)] = pair_t.T.astype(o_ref.dtype)


def _attn_kernel(sink_ref, q_ref, kp_ref, kc_ref, kn_ref, vp_ref, vc_ref, vn_ref, kx_ref, vx_ref, o_ref):
    i = pl.program_id(1)
    nq = Q_BLOCK
    n_ctx = kx_ref.shape[1]
    shape = (3 * nq + n_ctx, GQA_GROUP * nq)
    key = lax.broadcasted_iota(jnp.int32, shape, 0)
    qry = lax.broadcasted_iota(jnp.int32, shape, 1) % nq
    valid = (key >= qry) & (key <= qry + 2 * WINDOW)
    valid = valid & ((key >= nq) | (i > 0)) & ((key < 2 * nq) | (i < pl.num_programs(1) - 1))
    valid = valid | (key >= 3 * nq)
    k_all = jnp.concatenate([kp_ref[0], kc_ref[0], kn_ref[0], kx_ref[0]], axis=0)
    vt_all = jnp.concatenate([vp_ref[0], vc_ref[0], vn_ref[0], vx_ref[0]], axis=1)
    for h in range(k_all.shape[1] // HEAD_PAIR):
        rows = slice(h * HEAD_PAIR, (h + 1) * HEAD_PAIR)
        _attend_kv_head(q_ref, o_ref, sink_ref, h, k_all[:, rows], vt_all[rows, :], valid)


def _attn_call(q, k, vt, kx, vxt, sinks, n_ctx):
    b, l, qd = q.shape
    kvd = k.shape[2]
    nb = l // Q_BLOCK
    k_spec = lambda f: pl.BlockSpec((1, Q_BLOCK, kvd), lambda i, t: (i, f(t), 0))
    v_spec = lambda f: pl.BlockSpec((1, kvd, Q_BLOCK), lambda i, t: (i, 0, f(t)))
    prev = lambda t: jnp.maximum(t - 1, 0)
    cur = lambda t: t
    nxt = lambda t: jnp.minimum(t + 1, nb - 1)
    return pl.pallas_call(
        _attn_kernel,
        grid=(b, nb),
        in_specs=[
            pl.BlockSpec(memory_space=pltpu.SMEM),
            pl.BlockSpec((1, Q_BLOCK, qd), lambda i, t: (i, t, 0)),
            k_spec(prev), k_spec(cur), k_spec(nxt),
            v_spec(prev), v_spec(cur), v_spec(nxt),
            pl.BlockSpec((1, n_ctx, kvd), lambda i, t: (0, i, 0)),
            pl.BlockSpec((1, kvd, n_ctx), lambda i, t: (0, 0, i)),
        ],
        out_specs=pl.BlockSpec((1, Q_BLOCK, qd), lambda i, t: (i, t, 0)),
        out_shape=jax.ShapeDtypeStruct((b, l, qd), BF16),
        compiler_params=_params("parallel", "parallel"),
        name="window_attn",
    )(sinks, q, k, k, k, vt, vt, vt, kx, vxt)


def _ctx_attn_kernel(sink_ref, q_ref, k_ref, vt_ref, o_ref):
    for h in range(k_ref.shape[2] // HEAD_PAIR):
        rows = slice(h * HEAD_PAIR, (h + 1) * HEAD_PAIR)
        _attend_kv_head(q_ref, o_ref, sink_ref, h, k_ref[0, :, rows], vt_ref[0, rows, :], None)


def _ctx_attn_call(q, k, vt, sinks, n_ctx):
    qd, kvd = q.shape[2], k.shape[2]
    return pl.pallas_call(
        _ctx_attn_kernel,
        grid=(q.shape[1] // n_ctx,),
        in_specs=[
            pl.BlockSpec(memory_space=pltpu.SMEM),
            pl.BlockSpec((1, n_ctx, qd), lambda i: (0, i, 0)),
            pl.BlockSpec((1, n_ctx, kvd), lambda i: (0, i, 0)),
            pl.BlockSpec((1, kvd, n_ctx), lambda i: (0, 0, i)),
        ],
        out_specs=pl.BlockSpec((1, n_ctx, qd), lambda i: (0, i, 0)),
        out_shape=jax.ShapeDtypeStruct(q.shape, BF16),
        compiler_params=_params("parallel"),
        name="ctx_attn",
    )(sinks, q, k, vt)


def _oproj_kernel(a_ref, x_ref, mod_ref, npost_ref, w_ref, o_ref):
    y = jnp.dot(a_ref[0], w_ref[...], preferred_element_type=F32)
    o_ref[0] = x_ref[0] + _rms(y, mod_ref[0, 2:3, :] * npost_ref[...])


def _oproj_call(a, x, mod, npost, w_o, tm):
    b, l, d = x.shape
    ad = a.shape[2]
    return pl.pallas_call(
        _oproj_kernel,
        grid=(b, l // tm),
        in_specs=[
            pl.BlockSpec((1, tm, ad), lambda i, t: (i, t, 0)),
            pl.BlockSpec((1, tm, d), lambda i, t: (i, t, 0)),
            pl.BlockSpec((1, 6, d), lambda i, t: (i, 0, 0)),
            pl.BlockSpec((1, d), lambda i, t: (0, 0)),
            _resident(w_o.shape),
        ],
        out_specs=pl.BlockSpec((1, tm, d), lambda i, t: (i, t, 0)),
        out_shape=jax.ShapeDtypeStruct((b, l, d), F32),
        compiler_params=_params("parallel", "parallel"),
        name="attn_out_proj",
    )(a, x, mod, npost, w_o)


def _rope_tables(l):
    axis_dim = HEAD_DIM // 2
    rows_n = l // GRID_W
    row = jnp.repeat(jnp.arange(rows_n), GRID_W).astype(F32)
    col = jnp.tile(jnp.arange(GRID_W), rows_n).astype(F32)
    inv = 1.0 / (ROPE_BASE ** (jnp.arange(0, axis_dim, 2, dtype=F32) / axis_dim))
    ang_r = row[:, None] * inv[None, :]
    ang_c = col[:, None] * inv[None, :]
    ang = jnp.concatenate([ang_r, ang_r, ang_c, ang_c], axis=-1)
    sign = jnp.tile(jnp.concatenate([-jnp.ones(axis_dim // 2, F32), jnp.ones(axis_dim // 2, F32)]), 2)
    cos, sin = jnp.cos(ang), jnp.sin(ang) * sign[None, :]
    return jnp.tile(cos, (1, LANES // HEAD_DIM)), jnp.tile(sin, (1, LANES // HEAD_DIM))


def _dup_heads(w, n_heads):
    d = w.shape[0]
    w = w.reshape(d, n_heads, 1, HEAD_DIM)
    return jnp.broadcast_to(w, (d, n_heads, 2, HEAD_DIM)).reshape(d, n_heads * HEAD_PAIR)


def _token_tile(l, target):
    return min(l, target)


def kernel(x, c, ctx, c_ctx, w_ada, b_ada, norm_pre_mix, norm_post_mix, norm_pre_ffn, norm_post_ffn,
           w_pool, pool_scale, w_qkv, w_o, attn_sinks, w_gate_up, w_down):
    b, l, d = x.shape
    depth = w_ada.shape[0]
    n_mixers = 2
    q_dim = w_o.shape[1]
    kv_heads = (w_qkv.shape[2] - q_dim) // (2 * HEAD_DIM)
    kv_dim = kv_heads * HEAD_DIM

    rows = -(-(b + 1) // BF16_ROWS) * BF16_ROWS
    c_rows = jnp.concatenate([c, c_ctx[None, :], jnp.zeros((rows - b - 1, d), F32)], axis=0)
    ada = _ada_call(c_rows, w_ada, b_ada).reshape(depth, rows, 6, d)
    mod_x = ada[:, :b]
    mod_c = jnp.broadcast_to(ada[:, b:b + 1], (depth, b, 6, d))

    w_gu_bf = w_gate_up.astype(BF16)
    w_dn_bf = w_down.astype(BF16)
    w_pool_bf = w_pool.astype(BF16)
    w_o_bf = w_o.astype(BF16)
    w_qkv_bf = w_qkv.astype(BF16)
    w_qkv_dup = jnp.concatenate([
        w_qkv_bf[:, :, :q_dim],
        jax.vmap(lambda w: _dup_heads(w, kv_heads))(w_qkv_bf[:, :, q_dim:q_dim + kv_dim]),
        jax.vmap(lambda w: _dup_heads(w, kv_heads))(w_qkv_bf[:, :, q_dim + kv_dim:]),
    ], axis=-1)
    tables = _rope_tables(l)

    lc = ctx.shape[1]
    tm_x = _token_tile(l, 512)
    tm_c = _token_tile(b * lc, 512)
    tf = 512

    flat = lambda a: a.reshape(1, b * lc, a.shape[-1])
    unflat = lambda a: a.reshape(b, lc, a.shape[-1])

    for i in range(depth):
        last = i == depth - 1
        j = i // n_mixers
        npre, npost = norm_pre_mix[i][None, :], norm_post_mix[i][None, :]
        if i % n_mixers == 0:
            ps = pool_scale[j][None, :]
            x = _pool_call(x, mod_x[i], npre, npost, w_pool_bf[j], ps, tm_x)
            if not last:
                ctx = _pool_call(ctx, mod_c[i], npre, npost, w_pool_bf[j], ps, _token_tile(lc, 512))
        else:
            q, k, v = _qkv_call(x, mod_x[i], npre, w_qkv_dup[j], q_dim, tables, tm_x)
            qc, kc, vct = _qkv_call(flat(ctx), mod_c[i][:1], npre, w_qkv_dup[j], q_dim, None, tm_c)
            a = _attn_call(q, k, v, kc, vct, attn_sinks[j], lc)
            x = _oproj_call(a, x, mod_x[i], npost, w_o_bf[j], tm_x)
            if not last:
                ac = _ctx_attn_call(qc, kc, vct, attn_sinks[j], lc)
                ctx = unflat(_oproj_call(ac, flat(ctx), mod_c[i][:1], npost, w_o_bf[j], tm_c))

        npre, npost = norm_pre_ffn[i][None, :], norm_post_ffn[i][None, :]
        x = _ffn_call(x, mod_x[i], npre, npost, w_gu_bf[i], w_dn_bf[i], tm_x, tf)
        if not last:
            ctx = unflat(_ffn_call(flat(ctx), mod_c[i][:1], npre, npost, w_gu_bf[i], w_dn_bf[i], tm_c, tf))
    return x
```

```python
import functools

import jax
import jax.numpy as jnp
from jax import lax
from jax.experimental import pallas as pl
from jax.experimental.pallas import tpu as pltpu

GRID_W = 64
POOL_WINDOWS = (2, 4, 8, 16)
HEAD_DIM = 64
GQA_GROUP = 8
WINDOW = 128
Q_BLOCK = 128
ROPE_BASE = 10000.0
RMS_EPS = 1e-6
NEG_INF = -1e30
LOG2E = 1.4426950408889634

LANES = 128
SUBLANES = 8
BF16_ROWS = 16
VMEM_LIMIT_BYTES = 56 * 1024 * 1024

POOL_HALO = SUBLANES
HEAD_PAIR = 2 * HEAD_DIM

F32 = jnp.float32
BF16 = jnp.bfloat16


def _params(*semantics):
    return pltpu.CompilerParams(dimension_semantics=semantics, vmem_limit_bytes=VMEM_LIMIT_BYTES)


def _resident(shape):
    nd = len(shape)
    return pl.BlockSpec(shape, lambda *_: (0,) * nd, pipeline_mode=pl.Buffered(1))


def _rms(x, w):
    ms = jnp.mean(x * x, axis=-1, keepdims=True)
    return x * lax.rsqrt(ms + RMS_EPS) * w


def _norm_mod(x, w, shift, scale):
    return _rms(x, w * (1.0 + scale)) + shift


def _ada_kernel(c_ref, w_ref, b_ref, o_ref):
    s = jax.nn.silu(c_ref[...]).astype(BF16)
    o_ref[0] = jnp.dot(s, w_ref[0].astype(BF16), preferred_element_type=F32) + b_ref[0]


def _ada_call(c_rows, w_ada, b_ada):
    depth, d, n = w_ada.shape
    rows = c_rows.shape[0]
    tn = 1024
    return pl.pallas_call(
        _ada_kernel,
        grid=(depth, n // tn),
        in_specs=[
            pl.BlockSpec((rows, d), lambda i, j: (0, 0)),
            pl.BlockSpec((1, d, tn), lambda i, j: (i, 0, j)),
            pl.BlockSpec((1, 1, tn), lambda i, j: (i, 0, j)),
        ],
        out_specs=pl.BlockSpec((1, rows, tn), lambda i, j: (i, 0, j)),
        out_shape=jax.ShapeDtypeStruct((depth, rows, n), F32),
        compiler_params=_params("parallel", "parallel"),
        name="ada_proj",
    )(c_rows, w_ada, b_ada.reshape(depth, 1, n))


def _ffn_kernel(x_ref, mod_ref, npre_ref, npost_ref, wg_ref, wu_ref, wd_ref, o_ref, h_ref):
    k = pl.program_id(2)

    @pl.when(k == 0)
    def _():
        h = _norm_mod(x_ref[0], npre_ref[...], mod_ref[0, 3:4, :], mod_ref[0, 4:5, :])
        h_ref[...] = h.astype(BF16)
        o_ref[0] = jnp.zeros(o_ref.shape[1:], F32)

    h = h_ref[...]
    g = jnp.dot(h, wg_ref[...], preferred_element_type=F32)
    u = jnp.dot(h, wu_ref[...], preferred_element_type=F32)
    a = (jax.nn.silu(g) * u).astype(BF16)
    o_ref[0] += jnp.dot(a, wd_ref[...], preferred_element_type=F32)

    @pl.when(k == pl.num_programs(2) - 1)
    def _():
        o_ref[0] = x_ref[0] + _rms(o_ref[0], mod_ref[0, 5:6, :] * npost_ref[...])


def _ffn_call(x, mod, npre, npost, w_gate_up, w_down, tm, tf):
    b, l, d = x.shape
    f = w_down.shape[0]
    nk = f // tf
    return pl.pallas_call(
        _ffn_kernel,
        grid=(b, l // tm, nk),
        in_specs=[
            pl.BlockSpec((1, tm, d), lambda i, t, k: (i, t, 0)),
            pl.BlockSpec((1, 6, d), lambda i, t, k: (i, 0, 0)),
            pl.BlockSpec((1, d), lambda i, t, k: (0, 0)),
            pl.BlockSpec((1, d), lambda i, t, k: (0, 0)),
            pl.BlockSpec((d, tf), lambda i, t, k: (0, k)),
            pl.BlockSpec((d, tf), lambda i, t, k: (0, nk + k)),
            pl.BlockSpec((tf, d), lambda i, t, k: (k, 0)),
        ],
        out_specs=pl.BlockSpec((1, tm, d), lambda i, t, k: (i, t, 0)),
        out_shape=jax.ShapeDtypeStruct((b, l, d), F32),
        scratch_shapes=[pltpu.VMEM((tm, d), BF16)],
        compiler_params=_params("parallel", "parallel", "arbitrary"),
        name="swiglu",
    )(x, mod, npre, npost, w_gate_up, w_gate_up, w_down)


def _pool_kernel(x_ref, xp_ref, xn_ref, mod_ref, npre_ref, npost_ref, wp_ref, ps_ref, o_ref, h_ref, *, seq_len):
    t = pl.program_id(1)
    tm = x_ref.shape[1]
    gd = wp_ref.shape[1]
    d = x_ref.shape[2]
    npre = npre_ref[...]
    shift, scale = mod_ref[0, 0:1, :], mod_ref[0, 1:2, :]

    hp = _norm_mod(xp_ref[0], npre, shift, scale)
    hn = _norm_mod(xn_ref[0], npre, shift, scale)
    h_ref[0:POOL_HALO, :] = jnp.where(t > 0, hp, 0.0)
    h_ref[POOL_HALO:POOL_HALO + tm, :] = _norm_mod(x_ref[0], npre, shift, scale)
    h_ref[POOL_HALO + tm:, :] = jnp.where(t < pl.num_programs(1) - 1, hn, 0.0)

    pos = t * tm + lax.broadcasted_iota(jnp.int32, (tm, gd), 0)
    ss = jnp.zeros((tm, 1), F32)
    for g, w in enumerate(POOL_WINDOWS):
        cols = pl.ds(g * gd, gd)
        acc = h_ref[pl.ds(POOL_HALO - w // 2, tm), cols]
        for j in range(-w // 2 + 1, w // 2):
            acc = acc + h_ref[pl.ds(POOL_HALO + j, tm), cols]
        cnt = jnp.minimum(pos + w // 2, seq_len) - jnp.maximum(pos - w // 2, 0)
        p = acc / cnt.astype(F32) - h_ref[pl.ds(POOL_HALO, tm), cols]
        y = jnp.dot(p.astype(BF16), wp_ref[g], preferred_element_type=F32) * ps_ref[:, cols]
        o_ref[0, :, cols] = y
        ss = ss + jnp.sum(y * y, axis=-1, keepdims=True)

    rstd = lax.rsqrt(ss * (1.0 / d) + RMS_EPS)
    o_ref[0] = x_ref[0] + o_ref[0] * rstd * (mod_ref[0, 2:3, :] * npost_ref[...])


def _pool_call(x, mod, npre, npost, w_pool, pool_scale, tm):
    b, l, d = x.shape
    hb = tm // POOL_HALO
    last_hb = l // POOL_HALO - 1
    return pl.pallas_call(
        functools.partial(_pool_kernel, seq_len=l),
        grid=(b, l // tm),
        in_specs=[
            pl.BlockSpec((1, tm, d), lambda i, t: (i, t, 0)),
            pl.BlockSpec((1, POOL_HALO, d), lambda i, t: (i, jnp.maximum(t * hb - 1, 0), 0)),
            pl.BlockSpec((1, POOL_HALO, d), lambda i, t: (i, jnp.minimum((t + 1) * hb, last_hb), 0)),
            pl.BlockSpec((1, 6, d), lambda i, t: (i, 0, 0)),
            pl.BlockSpec((1, d), lambda i, t: (0, 0)),
            pl.BlockSpec((1, d), lambda i, t: (0, 0)),
            _resident(w_pool.shape),
            pl.BlockSpec((1, d), lambda i, t: (0, 0)),
        ],
        out_specs=pl.BlockSpec((1, tm, d), lambda i, t: (i, t, 0)),
        out_shape=jax.ShapeDtypeStruct((b, l, d), F32),
        scratch_shapes=[pltpu.VMEM((tm + 2 * POOL_HALO, d), F32)],
        compiler_params=_params("parallel", "parallel"),
        name="pool_mixer",
    )(x, x, x, mod, npre, npost, w_pool, pool_scale)


def _rope(x, cos, sin_signed, low_half):
    out = []
    for j in range(x.shape[1] // LANES):
        c = x[:, j * LANES:(j + 1) * LANES]
        rot = jnp.where(low_half, pltpu.roll(c, LANES - HEAD_DIM // 4, 1), pltpu.roll(c, HEAD_DIM // 4, 1))
        out.append(c * cos + rot * sin_signed)
    return jnp.concatenate(out, axis=1)


def _qkv_kernel(*refs, rope, q_dim, k_dim):
    if rope:
        x_ref, mod_ref, npre_ref, w_ref, cos_ref, sin_ref, q_ref, k_ref, vt_ref = refs
    else:
        x_ref, mod_ref, npre_ref, w_ref, q_ref, k_ref, vt_ref = refs
    h = _norm_mod(x_ref[0], npre_ref[...], mod_ref[0, 0:1, :], mod_ref[0, 1:2, :]).astype(BF16)
    qkv = jnp.dot(h, w_ref[...], preferred_element_type=F32)
    q = qkv[:, :q_dim]
    k = qkv[:, q_dim:q_dim + k_dim]
    v = qkv[:, q_dim + k_dim:]
    if rope:
        cos, sin_signed = cos_ref[...], sin_ref[...]
        lane = lax.broadcasted_iota(jnp.int32, cos.shape, 1)
        low_half = (lane % (HEAD_DIM // 2)) < (HEAD_DIM // 4)
        q = _rope(q, cos, sin_signed, low_half)
        k = _rope(k, cos, sin_signed, low_half)
    q_ref[0] = (q * (HEAD_DIM ** -0.5 * LOG2E)).astype(BF16)
    k_ref[0] = k.astype(BF16)
    vt_ref[0] = v.T.astype(BF16)


def _qkv_call(x, mod, npre, w_qkv_dup, q_dim, rope_tables, tm):
    b, l, d = x.shape
    n = w_qkv_dup.shape[1]
    v_dim = (n - q_dim) // 3
    k_dim = 2 * v_dim
    rope = rope_tables is not None
    in_specs = [
        pl.BlockSpec((1, tm, d), lambda i, t: (i, t, 0)),
        pl.BlockSpec((1, 6, d), lambda i, t: (i, 0, 0)),
        pl.BlockSpec((1, d), lambda i, t: (0, 0)),
        _resident(w_qkv_dup.shape),
    ]
    args = [x, mod, npre, w_qkv_dup]
    if rope:
        in_specs += [pl.BlockSpec((tm, LANES), lambda i, t: (t, 0))] * 2
        args += list(rope_tables)
    return pl.pallas_call(
        functools.partial(_qkv_kernel, rope=rope, q_dim=q_dim, k_dim=k_dim),
        grid=(b, l // tm),
        in_specs=in_specs,
        out_specs=[
            pl.BlockSpec((1, tm, q_dim), lambda i, t: (i, t, 0)),
            pl.BlockSpec((1, tm, k_dim), lambda i, t: (i, t, 0)),
            pl.BlockSpec((1, v_dim, tm), lambda i, t: (i, 0, t)),
        ],
        out_shape=[
            jax.ShapeDtypeStruct((b, l, q_dim), BF16),
            jax.ShapeDtypeStruct((b, l, k_dim), BF16),
            jax.ShapeDtypeStruct((b, v_dim, l), BF16),
        ],
        compiler_params=_params("parallel", "parallel"),
        name="qkv_rope" if rope else "qkv_ctx",
    )(*args)


def _attend_scores(q_ref, sink_ref, h, k):
    nq = q_ref.shape[1]
    low = lax.broadcasted_iota(jnp.int32, (nq, HEAD_PAIR), 1) < HEAD_DIM
    zero = jnp.zeros((), BF16)
    base = h * GQA_GROUP * HEAD_DIM
    q_rows, sink_cols = [], []
    for g in range(GQA_GROUP):
        pair = q_ref[0, :, pl.ds(base + (g // 2) * HEAD_PAIR, HEAD_PAIR)]
        q_rows.append(jnp.where(low if g % 2 == 0 else ~low, pair, zero))
        sink_cols.append(jnp.full((1, nq), sink_ref[h * GQA_GROUP + g] * LOG2E, F32))
    qg = jnp.concatenate(q_rows, axis=0)
    snk = jnp.concatenate(sink_cols, axis=1)
    s = lax.dot_general(k, qg, (((1,), (1,)), ((), ())), preferred_element_type=F32)
    return s, snk


def _attend_softmax(s, snk, masks):
    blocks, row = [], 0
    for first, bias in masks:
        if first > row:
            blocks.append(s[row:first])
        row = first + bias.shape[0]
        blocks.append(s[first:row] + bias)
    if masks:
        if row < s.shape[0]:
            blocks.append(s[row:])
        s = jnp.concatenate(blocks, axis=0)
    m = jnp.maximum(snk, jnp.max(s, axis=0, keepdims=True))
    return jnp.exp2(s - m).astype(BF16), jnp.exp2(snk - m)


def _attend_values(o_ref, h, p, sink_p, vt):
    nq = o_ref.shape[1]
    base = h * GQA_GROUP * HEAD_DIM
    vt_ones = jnp.concatenate([vt, jnp.ones((BF16_ROWS, vt.shape[1]), BF16)], axis=0)
    acc = jnp.dot(vt_ones, p, preferred_element_type=F32)
    den = sink_p + acc[HEAD_DIM:HEAD_DIM + 1]
    acc = acc[:HEAD_DIM] * (1.0 / den)
    for j in range(GQA_GROUP // 2):
        pair_t = jnp.concatenate([acc[:, (2 * j) * nq:(2 * j + 1) * nq], acc[:, (2 * j + 1) * nq:(2 * j + 2) * nq]],
                                 axis=0)
        o_ref[0, :, pl.ds(base + j * HEAD_PAIR, HEAD_PAIR)] = pair_t.T.astype(o_ref.dtype)


def _attn_kernel(sink_ref, q_ref, kp_ref, kc_ref, kn_ref, vp_ref, vc_ref, vn_ref, kx_ref, vx_ref, o_ref):
    i = pl.program_id(1)
    nq = Q_BLOCK
    shape = (nq, GQA_GROUP * nq)
    key = lax.broadcasted_iota(jnp.int32, shape, 0)
    qry = lax.broadcasted_iota(jnp.int32, shape, 1) % nq
    valid_prev = (key >= qry) & (i > 0)
    valid_next = (key <= qry) & (i < pl.num_programs(1) - 1)
    masks = [(0, jnp.where(valid_prev, 0.0, NEG_INF)), (2 * nq, jnp.where(valid_next, 0.0, NEG_INF))]
    k_all = jnp.concatenate([kp_ref[0], kc_ref[0], kn_ref[0], kx_ref[0]], axis=0)
    vt_all = jnp.concatenate([vp_ref[0], vc_ref[0], vn_ref[0], vx_ref[0]], axis=1)
    n_kv = vt_all.shape[0] // HEAD_DIM
    scores, probs = {}, {}
    for h in range(n_kv + 2):
        if h < n_kv:
            scores[h] = _attend_scores(q_ref, sink_ref, h, k_all[:, h * HEAD_PAIR:(h + 1) * HEAD_PAIR])
        if 0 <= h - 2 < n_kv:
            _attend_values(o_ref, h - 2, *probs.pop(h - 2), vt_all[(h - 2) * HEAD_DIM:(h - 1) * HEAD_DIM, :])
        if 0 <= h - 1 < n_kv:
            probs[h - 1] = _attend_softmax(*scores.pop(h - 1), masks)


def _attn_call(q, k, vt, kx, vxt, sinks, n_ctx):
    b, l, qd = q.shape
    kd, vd = k.shape[2], vt.shape[1]
    nb = l // Q_BLOCK
    k_spec = lambda f: pl.BlockSpec((1, Q_BLOCK, kd), lambda i, t: (i, f(t), 0))
    v_spec = lambda f: pl.BlockSpec((1, vd, Q_BLOCK), lambda i, t: (i, 0, f(t)))
    prev = lambda t: jnp.maximum(t - 1, 0)
    cur = lambda t: t
    nxt = lambda t: jnp.minimum(t + 1, nb - 1)
    return pl.pallas_call(
        _attn_kernel,
        grid=(b, nb),
        in_specs=[
            pl.BlockSpec(memory_space=pltpu.SMEM),
            pl.BlockSpec((1, Q_BLOCK, qd), lambda i, t: (i, t, 0)),
            k_spec(prev), k_spec(cur), k_spec(nxt),
            v_spec(prev), v_spec(cur), v_spec(nxt),
            pl.BlockSpec((1, n_ctx, kd), lambda i, t: (0, i, 0)),
            pl.BlockSpec((1, vd, n_ctx), lambda i, t: (0, 0, i)),
        ],
        out_specs=pl.BlockSpec((1, Q_BLOCK, qd), lambda i, t: (i, t, 0)),
        out_shape=jax.ShapeDtypeStruct((b, l, qd), BF16),
        compiler_params=_params("parallel", "parallel"),
        name="window_attn",
    )(sinks, q, k, k, k, vt, vt, vt, kx, vxt)


def _ctx_attn_kernel(sink_ref, q_ref, k_ref, vt_ref, o_ref):
    for h in range(vt_ref.shape[1] // HEAD_DIM):
        sc = _attend_scores(q_ref, sink_ref, h, k_ref[0, :, h * HEAD_PAIR:(h + 1) * HEAD_PAIR])
        _attend_values(o_ref, h, *_attend_softmax(*sc, []), vt_ref[0, h * HEAD_DIM:(h + 1) * HEAD_DIM, :])


def _ctx_attn_call(q, k, vt, sinks, n_ctx):
    qd, kd, vd = q.shape[2], k.shape[2], vt.shape[1]
    return pl.pallas_call(
        _ctx_attn_kernel,
        grid=(q.shape[1] // n_ctx,),
        in_specs=[
            pl.BlockSpec(memory_space=pltpu.SMEM),
            pl.BlockSpec((1, n_ctx, qd), lambda i: (0, i, 0)),
            pl.BlockSpec((1, n_ctx, kd), lambda i: (0, i, 0)),
            pl.BlockSpec((1, vd, n_ctx), lambda i: (0, 0, i)),
        ],
        out_specs=pl.BlockSpec((1, n_ctx, qd), lambda i: (0, i, 0)),
        out_shape=jax.ShapeDtypeStruct(q.shape, BF16),
        compiler_params=_params("parallel"),
        name="ctx_attn",
    )(sinks, q, k, vt)


def _oproj_kernel(a_ref, x_ref, mod_ref, npost_ref, w_ref, o_ref):
    y = jnp.dot(a_ref[0], w_ref[...], preferred_element_type=F32)
    o_ref[0] = x_ref[0] + _rms(y, mod_ref[0, 2:3, :] * npost_ref[...])


def _oproj_call(a, x, mod, npost, w_o, tm):
    b, l, d = x.shape
    ad = a.shape[2]
    return pl.pallas_call(
        _oproj_kernel,
        grid=(b, l // tm),
        in_specs=[
            pl.BlockSpec((1, tm, ad), lambda i, t: (i, t, 0)),
            pl.BlockSpec((1, tm, d), lambda i, t: (i, t, 0)),
            pl.BlockSpec((1, 6, d), lambda i, t: (i, 0, 0)),
            pl.BlockSpec((1, d), lambda i, t: (0, 0)),
            _resident(w_o.shape),
        ],
        out_specs=pl.BlockSpec((1, tm, d), lambda i, t: (i, t, 0)),
        out_shape=jax.ShapeDtypeStruct((b, l, d), F32),
        compiler_params=_params("parallel", "parallel"),
        name="attn_out_proj",
    )(a, x, mod, npost, w_o)


def _rope_tables(l):
    axis_dim = HEAD_DIM // 2
    rows_n = l // GRID_W
    row = jnp.repeat(jnp.arange(rows_n), GRID_W).astype(F32)
    col = jnp.tile(jnp.arange(GRID_W), rows_n).astype(F32)
    inv = 1.0 / (ROPE_BASE ** (jnp.arange(0, axis_dim, 2, dtype=F32) / axis_dim))
    ang_r = row[:, None] * inv[None, :]
    ang_c = col[:, None] * inv[None, :]
    ang = jnp.concatenate([ang_r, ang_r, ang_c, ang_c], axis=-1)
    sign = jnp.tile(jnp.concatenate([-jnp.ones(axis_dim // 2, F32), jnp.ones(axis_dim // 2, F32)]), 2)
    cos, sin = jnp.cos(ang), jnp.sin(ang) * sign[None, :]
    return jnp.tile(cos, (1, LANES // HEAD_DIM)), jnp.tile(sin, (1, LANES // HEAD_DIM))


def _dup_heads(w, n_heads):
    d = w.shape[0]
    w = w.reshape(d, n_heads, 1, HEAD_DIM)
    return jnp.broadcast_to(w, (d, n_heads, 2, HEAD_DIM)).reshape(d, n_heads * HEAD_PAIR)


def _token_tile(l, target):
    return min(l, target)


def kernel(x, c, ctx, c_ctx, w_ada, b_ada, norm_pre_mix, norm_post_mix, norm_pre_ffn, norm_post_ffn,
           w_pool, pool_scale, w_qkv, w_o, attn_sinks, w_gate_up, w_down):
    b, l, d = x.shape
    depth = w_ada.shape[0]
    n_mixers = 2
    q_dim = w_o.shape[1]
    kv_heads = (w_qkv.shape[2] - q_dim) // (2 * HEAD_DIM)
    kv_dim = kv_heads * HEAD_DIM

    rows = -(-(b + 1) // BF16_ROWS) * BF16_ROWS
    c_rows = jnp.concatenate([c, c_ctx[None, :], jnp.zeros((rows - b - 1, d), F32)], axis=0)
    ada = _ada_call(c_rows, w_ada, b_ada).reshape(depth, rows, 6, d)
    mod_x = ada[:, :b]
    mod_c = jnp.broadcast_to(ada[:, b:b + 1], (depth, b, 6, d))

    w_gu_bf = w_gate_up.astype(BF16)
    w_dn_bf = w_down.astype(BF16)
    w_pool_bf = w_pool.astype(BF16)
    w_o_bf = w_o.astype(BF16)
    w_qkv_bf = w_qkv.astype(BF16)
    w_qkv_dup = jnp.concatenate([
        w_qkv_bf[:, :, :q_dim],
        jax.vmap(lambda w: _dup_heads(w, kv_heads))(w_qkv_bf[:, :, q_dim:q_dim + kv_dim]),
        w_qkv_bf[:, :, q_dim + kv_dim:],
    ], axis=-1)
    tables = _rope_tables(l)

    lc = ctx.shape[1]
    tm_x = _token_tile(l, 512)
    tm_c = _token_tile(b * lc, 512)
    tf = 512

    flat = lambda a: a.reshape(1, b * lc, a.shape[-1])
    unflat = lambda a: a.reshape(b, lc, a.shape[-1])

    for i in range(depth):
        last = i == depth - 1
        j = i // n_mixers
        npre, npost = norm_pre_mix[i][None, :], norm_post_mix[i][None, :]
        if i % n_mixers == 0:
            ps = pool_scale[j][None, :]
            x = _pool_call(x, mod_x[i], npre, npost, w_pool_bf[j], ps, tm_x)
            if not last:
                ctx = _pool_call(ctx, mod_c[i], npre, npost, w_pool_bf[j], ps, _token_tile(lc, 512))
        else:
            q, k, v = _qkv_call(x, mod_x[i], npre, w_qkv_dup[j], q_dim, tables, tm_x)
            qc, kc, vct = _qkv_call(flat(ctx), mod_c[i][:1], npre, w_qkv_dup[j], q_dim, None, tm_c)
            a = _attn_call(q, k, v, kc, vct, attn_sinks[j], lc)
            x = _oproj_call(a, x, mod_x[i], npost, w_o_bf[j], tm_x)
            if not last:
                ac = _ctx_attn_call(qc, kc, vct, attn_sinks[j], lc)
                ctx = unflat(_oproj_call(ac, flat(ctx), mod_c[i][:1], npost, w_o_bf[j], tm_c))

        npre, npost = norm_pre_ffn[i][None, :], norm_post_ffn[i][None, :]
        x = _ffn_call(x, mod_x[i], npre, npost, w_gu_bf[i], w_dn_bf[i], tm_x, tf)
        if not last:
            ctx = unflat(_ffn_call(flat(ctx), mod_c[i][:1], npre, npost, w_gu_bf[i], w_dn_bf[i], tm_c, tf))
    return x
```

```python
import functools

import jax
import jax.numpy as jnp
from jax import lax
from jax.experimental import pallas as pl
from jax.experimental.pallas import tpu as pltpu

GRID_W = 64
POOL_WINDOWS = (2, 4, 8, 16)
HEAD_DIM = 64
GQA_GROUP = 8
WINDOW = 128
Q_BLOCK = 128
ROPE_BASE = 10000.0
RMS_EPS = 1e-6
NEG_INF = -1e30
LOG2E = 1.4426950408889634

LANES = 128
SUBLANES = 8
BF16_ROWS = 16
VMEM_LIMIT_BYTES = 56 * 1024 * 1024
SWIGLU_VMEM_LIMIT_BYTES = 63 * 1024 * 1024

POOL_HALO = SUBLANES
HEAD_PAIR = 2 * HEAD_DIM

F32 = jnp.float32
BF16 = jnp.bfloat16


def _params(*semantics, vmem_limit_bytes=VMEM_LIMIT_BYTES):
    return pltpu.CompilerParams(dimension_semantics=semantics, vmem_limit_bytes=vmem_limit_bytes)


def _resident(shape):
    nd = len(shape)
    return pl.BlockSpec(shape, lambda *_: (0,) * nd, pipeline_mode=pl.Buffered(1))


def _rms(x, w):
    ms = jnp.mean(x * x, axis=-1, keepdims=True)
    return x * lax.rsqrt(ms + RMS_EPS) * w


def _norm_mod(x, w, shift, scale):
    return _rms(x, w * (1.0 + scale)) + shift


def _ada_kernel(c_ref, w_ref, b_ref, o_ref):
    s = jax.nn.silu(c_ref[...]).astype(BF16)
    o_ref[0] = jnp.dot(s, w_ref[0].astype(BF16), preferred_element_type=F32) + b_ref[0]


def _ada_call(c_rows, w_ada, b_ada):
    depth, d, n = w_ada.shape
    rows = c_rows.shape[0]
    tn = 1024
    return pl.pallas_call(
        _ada_kernel,
        grid=(depth, n // tn),
        in_specs=[
            pl.BlockSpec((rows, d), lambda i, j: (0, 0)),
            pl.BlockSpec((1, d, tn), lambda i, j: (i, 0, j)),
            pl.BlockSpec((1, 1, tn), lambda i, j: (i, 0, j)),
        ],
        out_specs=pl.BlockSpec((1, rows, tn), lambda i, j: (i, 0, j)),
        out_shape=jax.ShapeDtypeStruct((depth, rows, n), F32),
        compiler_params=_params("parallel", "parallel"),
        name="ada_proj",
    )(c_rows, w_ada, b_ada.reshape(depth, 1, n))


def _ffn_kernel(x_ref, mod_ref, npre_ref, npost_ref, wg_ref, wu_ref, wd_ref, o_ref, h_ref):
    k = pl.program_id(2)

    @pl.when(k == 0)
    def _():
        h = _norm_mod(x_ref[0], npre_ref[...], mod_ref[0, 3:4, :], mod_ref[0, 4:5, :])
        h_ref[...] = h.astype(BF16)
        o_ref[0] = jnp.zeros(o_ref.shape[1:], F32)

    h = h_ref[...]
    g = jnp.dot(h, wg_ref[...], preferred_element_type=F32)
    u = jnp.dot(h, wu_ref[...], preferred_element_type=F32)
    a = (jax.nn.silu(g) * u).astype(BF16)
    o_ref[0] += jnp.dot(a, wd_ref[...], preferred_element_type=F32)

    @pl.when(k == pl.num_programs(2) - 1)
    def _():
        o_ref[0] = x_ref[0] + _rms(o_ref[0], mod_ref[0, 5:6, :] * npost_ref[...])


def _ffn_call(x, mod, npre, npost, w_gate_up, w_down, tm, tf):
    b, l, d = x.shape
    f = w_down.shape[0]
    nk = f // tf
    return pl.pallas_call(
        _ffn_kernel,
        grid=(b, l // tm, nk),
        in_specs=[
            pl.BlockSpec((1, tm, d), lambda i, t, k: (i, t, 0)),
            pl.BlockSpec((1, 6, d), lambda i, t, k: (i, 0, 0)),
            pl.BlockSpec((1, d), lambda i, t, k: (0, 0)),
            pl.BlockSpec((1, d), lambda i, t, k: (0, 0)),
            pl.BlockSpec((d, tf), lambda i, t, k: (0, k)),
            pl.BlockSpec((d, tf), lambda i, t, k: (0, nk + k)),
            pl.BlockSpec((tf, d), lambda i, t, k: (k, 0)),
        ],
        out_specs=pl.BlockSpec((1, tm, d), lambda i, t, k: (i, t, 0)),
        out_shape=jax.ShapeDtypeStruct((b, l, d), F32),
        scratch_shapes=[pltpu.VMEM((tm, d), BF16)],
        compiler_params=_params("parallel", "parallel", "arbitrary", vmem_limit_bytes=SWIGLU_VMEM_LIMIT_BYTES),
        name="swiglu",
    )(x, mod, npre, npost, w_gate_up, w_gate_up, w_down)


def _pool_kernel(x_ref, xp_ref, xn_ref, mod_ref, npre_ref, npost_ref, wp_ref, ps_ref, o_ref, h_ref, *, seq_len):
    t = pl.program_id(1)
    tm = x_ref.shape[1]
    gd = wp_ref.shape[1]
    d = x_ref.shape[2]
    npre = npre_ref[...]
    shift, scale = mod_ref[0, 0:1, :], mod_ref[0, 1:2, :]

    hp = _norm_mod(xp_ref[0], npre, shift, scale)
    hn = _norm_mod(xn_ref[0], npre, shift, scale)
    h_ref[0:POOL_HALO, :] = jnp.where(t > 0, hp, 0.0)
    h_ref[POOL_HALO:POOL_HALO + tm, :] = _norm_mod(x_ref[0], npre, shift, scale)
    h_ref[POOL_HALO + tm:, :] = jnp.where(t < pl.num_programs(1) - 1, hn, 0.0)

    pos = t * tm + lax.broadcasted_iota(jnp.int32, (tm, gd), 0)
    ss = jnp.zeros((tm, 1), F32)
    for g, w in enumerate(POOL_WINDOWS):
        cols = pl.ds(g * gd, gd)
        acc = h_ref[pl.ds(POOL_HALO - w // 2, tm), cols]
        for j in range(-w // 2 + 1, w // 2):
            acc = acc + h_ref[pl.ds(POOL_HALO + j, tm), cols]
        cnt = jnp.minimum(pos + w // 2, seq_len) - jnp.maximum(pos - w // 2, 0)
        p = acc / cnt.astype(F32) - h_ref[pl.ds(POOL_HALO, tm), cols]
        y = jnp.dot(p.astype(BF16), wp_ref[g], preferred_element_type=F32) * ps_ref[:, cols]
        o_ref[0, :, cols] = y
        ss = ss + jnp.sum(y * y, axis=-1, keepdims=True)

    rstd = lax.rsqrt(ss * (1.0 / d) + RMS_EPS)
    o_ref[0] = x_ref[0] + o_ref[0] * rstd * (mod_ref[0, 2:3, :] * npost_ref[...])


def _pool_call(x, mod, npre, npost, w_pool, pool_scale, tm):
    b, l, d = x.shape
    hb = tm // POOL_HALO
    last_hb = l // POOL_HALO - 1
    return pl.pallas_call(
        functools.partial(_pool_kernel, seq_len=l),
        grid=(b, l // tm),
        in_specs=[
            pl.BlockSpec((1, tm, d), lambda i, t: (i, t, 0)),
            pl.BlockSpec((1, POOL_HALO, d), lambda i, t: (i, jnp.maximum(t * hb - 1, 0), 0)),
            pl.BlockSpec((1, POOL_HALO, d), lambda i, t: (i, jnp.minimum((t + 1) * hb, last_hb), 0)),
            pl.BlockSpec((1, 6, d), lambda i, t: (i, 0, 0)),
            pl.BlockSpec((1, d), lambda i, t: (0, 0)),
            pl.BlockSpec((1, d), lambda i, t: (0, 0)),
            _resident(w_pool.shape),
            pl.BlockSpec((1, d), lambda i, t: (0, 0)),
        ],
        out_specs=pl.BlockSpec((1, tm, d), lambda i, t: (i, t, 0)),
        out_shape=jax.ShapeDtypeStruct((b, l, d), F32),
        scratch_shapes=[pltpu.VMEM((tm + 2 * POOL_HALO, d), F32)],
        compiler_params=_params("parallel", "parallel"),
        name="pool_mixer",
    )(x, x, x, mod, npre, npost, w_pool, pool_scale)


def _rope(x, cos, sin_signed, low_half):
    out = []
    for j in range(x.shape[1] // LANES):
        c = x[:, j * LANES:(j + 1) * LANES]
        rot = jnp.where(low_half, pltpu.roll(c, LANES - HEAD_DIM // 4, 1), pltpu.roll(c, HEAD_DIM // 4, 1))
        out.append(c * cos + rot * sin_signed)
    return jnp.concatenate(out, axis=1)


def _qkv_kernel(*refs, rope, q_dim, k_dim):
    if rope:
        x_ref, mod_ref, npre_ref, w_ref, cos_ref, sin_ref, q_ref, k_ref, vt_ref = refs
    else:
        x_ref, mod_ref, npre_ref, w_ref, q_ref, k_ref, vt_ref = refs
    h = _norm_mod(x_ref[0], npre_ref[...], mod_ref[0, 0:1, :], mod_ref[0, 1:2, :]).astype(BF16)
    qkv = jnp.dot(h, w_ref[...], preferred_element_type=F32)
    q = qkv[:, :q_dim]
    k = qkv[:, q_dim:q_dim + k_dim]
    v = qkv[:, q_dim + k_dim:]
    if rope:
        cos, sin_signed = cos_ref[...], sin_ref[...]
        lane = lax.broadcasted_iota(jnp.int32, cos.shape, 1)
        low_half = (lane % (HEAD_DIM // 2)) < (HEAD_DIM // 4)
        q = _rope(q, cos, sin_signed, low_half)
        k = _rope(k, cos, sin_signed, low_half)
    q_ref[0] = (q * (HEAD_DIM ** -0.5 * LOG2E)).astype(BF16)
    k_ref[0] = k.astype(BF16)
    vt_ref[0] = v.T.astype(BF16)


def _qkv_call(x, mod, npre, w_qkv_dup, q_dim, rope_tables, tm):
    b, l, d = x.shape
    n = w_qkv_dup.shape[1]
    v_dim = (n - q_dim) // 3
    k_dim = 2 * v_dim
    rope = rope_tables is not None
    in_specs = [
        pl.BlockSpec((1, tm, d), lambda i, t: (i, t, 0)),
        pl.BlockSpec((1, 6, d), lambda i, t: (i, 0, 0)),
        pl.BlockSpec((1, d), lambda i, t: (0, 0)),
        _resident(w_qkv_dup.shape),
    ]
    args = [x, mod, npre, w_qkv_dup]
    if rope:
        in_specs += [pl.BlockSpec((tm, LANES), lambda i, t: (t, 0))] * 2
        args += list(rope_tables)
    return pl.pallas_call(
        functools.partial(_qkv_kernel, rope=rope, q_dim=q_dim, k_dim=k_dim),
        grid=(b, l // tm),
        in_specs=in_specs,
        out_specs=[
            pl.BlockSpec((1, tm, q_dim), lambda i, t: (i, t, 0)),
            pl.BlockSpec((1, tm, k_dim), lambda i, t: (i, t, 0)),
            pl.BlockSpec((1, v_dim, tm), lambda i, t: (i, 0, t)),
        ],
        out_shape=[
            jax.ShapeDtypeStruct((b, l, q_dim), BF16),
            jax.ShapeDtypeStruct((b, l, k_dim), BF16),
            jax.ShapeDtypeStruct((b, v_dim, l), BF16),
        ],
        compiler_params=_params("parallel", "parallel"),
        name="qkv_rope" if rope else "qkv_ctx",
    )(*args)


def _attend_scores(q_ref, sink_ref, h, k):
    nq = q_ref.shape[1]
    low = lax.broadcasted_iota(jnp.int32, (nq, HEAD_PAIR), 1) < HEAD_DIM
    zero = jnp.zeros((), BF16)
    base = h * GQA_GROUP * HEAD_DIM
    q_rows, sink_cols = [], []
    for g in range(GQA_GROUP):
        pair = q_ref[0, :, pl.ds(base + (g // 2) * HEAD_PAIR, HEAD_PAIR)]
        q_rows.append(jnp.where(low if g % 2 == 0 else ~low, pair, zero))
        sink_cols.append(jnp.full((1, nq), sink_ref[h * GQA_GROUP + g] * LOG2E, F32))
    qg = jnp.concatenate(q_rows, axis=0)
    snk = jnp.concatenate(sink_cols, axis=1)
    s = lax.dot_general(k, qg, (((1,), (1,)), ((), ())), preferred_element_type=F32)
    return s, snk


def _attend_softmax(s, snk, masks):
    blocks, row = [], 0
    for first, bias in masks:
        if first > row:
            blocks.append(s[row:first])
        row = first + bias.shape[0]
        blocks.append(s[first:row] + bias)
    if masks:
        if row < s.shape[0]:
            blocks.append(s[row:])
        s = jnp.concatenate(blocks, axis=0)
    m = jnp.maximum(snk, jnp.max(s, axis=0, keepdims=True))
    return jnp.exp2(s - m).astype(BF16), jnp.exp2(snk - m)


def _attend_values(o_ref, h, p, sink_p, vt):
    nq = o_ref.shape[1]
    base = h * GQA_GROUP * HEAD_DIM
    vt_ones = jnp.concatenate([vt, jnp.ones((BF16_ROWS, vt.shape[1]), BF16)], axis=0)
    acc = jnp.dot(vt_ones, p, preferred_element_type=F32)
    den = sink_p + acc[HEAD_DIM:HEAD_DIM + 1]
    acc = acc[:HEAD_DIM] * (1.0 / den)
    for j in range(GQA_GROUP // 2):
        pair_t = jnp.concatenate([acc[:, (2 * j) * nq:(2 * j + 1) * nq], acc[:, (2 * j + 1) * nq:(2 * j + 2) * nq]],
                                 axis=0)
        o_ref[0, :, pl.ds(base + j * HEAD_PAIR, HEAD_PAIR)] = pair_t.T.astype(o_ref.dtype)


def _attn_kernel(sink_ref, q_ref, kp_ref, kc_ref, kn_ref, vp_ref, vc_ref, vn_ref, kx_ref, vx_ref, o_ref):
    i = pl.program_id(1)
    nq = Q_BLOCK
    shape = (nq, GQA_GROUP * nq)
    key = lax.broadcasted_iota(jnp.int32, shape, 0)
    qry = lax.broadcasted_iota(jnp.int32, shape, 1) % nq
    valid_prev = (key >= qry) & (i > 0)
    valid_next = (key <= qry) & (i < pl.num_programs(1) - 1)
    masks = [(0, jnp.where(valid_prev, 0.0, NEG_INF)), (2 * nq, jnp.where(valid_next, 0.0, NEG_INF))]
    k_all = jnp.concatenate([kp_ref[0], kc_ref[0], kn_ref[0], kx_ref[0]], axis=0)
    vt_all = jnp.concatenate([vp_ref[0], vc_ref[0], vn_ref[0], vx_ref[0]], axis=1)
    n_kv = vt_all.shape[0] // HEAD_DIM
    scores, probs = {}, {}
    for h in range(n_kv + 2):
        if h < n_kv:
            scores[h] = _attend_scores(q_ref, sink_ref, h, k_all[:, h * HEAD_PAIR:(h + 1) * HEAD_PAIR])
        if 0 <= h - 2 < n_kv:
            _attend_values(o_ref, h - 2, *probs.pop(h - 2), vt_all[(h - 2) * HEAD_DIM:(h - 1) * HEAD_DIM, :])
        if 0 <= h - 1 < n_kv:
            probs[h - 1] = _attend_softmax(*scores.pop(h - 1), masks)


def _attn_call(q, k, vt, kx, vxt, sinks, n_ctx):
    b, l, qd = q.shape
    kd, vd = k.shape[2], vt.shape[1]
    nb = l // Q_BLOCK
    k_spec = lambda f: pl.BlockSpec((1, Q_BLOCK, kd), lambda i, t: (i, f(t), 0))
    v_spec = lambda f: pl.BlockSpec((1, vd, Q_BLOCK), lambda i, t: (i, 0, f(t)))
    prev = lambda t: jnp.maximum(t - 1, 0)
    cur = lambda t: t
    nxt = lambda t: jnp.minimum(t + 1, nb - 1)
    return pl.pallas_call(
        _attn_kernel,
        grid=(b, nb),
        in_specs=[
            pl.BlockSpec(memory_space=pltpu.SMEM),
            pl.BlockSpec((1, Q_BLOCK, qd), lambda i, t: (i, t, 0)),
            k_spec(prev), k_spec(cur), k_spec(nxt),
            v_spec(prev), v_spec(cur), v_spec(nxt),
            pl.BlockSpec((1, n_ctx, kd), lambda i, t: (0, i, 0)),
            pl.BlockSpec((1, vd, n_ctx), lambda i, t: (0, 0, i)),
        ],
        out_specs=pl.BlockSpec((1, Q_BLOCK, qd), lambda i, t: (i, t, 0)),
        out_shape=jax.ShapeDtypeStruct((b, l, qd), BF16),
        compiler_params=_params("parallel", "parallel"),
        name="window_attn",
    )(sinks, q, k, k, k, vt, vt, vt, kx, vxt)


def _ctx_attn_kernel(sink_ref, q_ref, k_ref, vt_ref, o_ref):
    for h in range(vt_ref.shape[1] // HEAD_DIM):
        sc = _attend_scores(q_ref, sink_ref, h, k_ref[0, :, h * HEAD_PAIR:(h + 1) * HEAD_PAIR])
        _attend_values(o_ref, h, *_attend_softmax(*sc, []), vt_ref[0, h * HEAD_DIM:(h + 1) * HEAD_DIM, :])


def _ctx_attn_call(q, k, vt, sinks, n_ctx):
    qd, kd, vd = q.shape[2], k.shape[2], vt.shape[1]
    return pl.pallas_call(
        _ctx_attn_kernel,
        grid=(q.shape[1] // n_ctx,),
        in_specs=[
            pl.BlockSpec(memory_space=pltpu.SMEM),
            pl.BlockSpec((1, n_ctx, qd), lambda i: (0, i, 0)),
            pl.BlockSpec((1, n_ctx, kd), lambda i: (0, i, 0)),
            pl.BlockSpec((1, vd, n_ctx), lambda i: (0, 0, i)),
        ],
        out_specs=pl.BlockSpec((1, n_ctx, qd), lambda i: (0, i, 0)),
        out_shape=jax.ShapeDtypeStruct(q.shape, BF16),
        compiler_params=_params("parallel"),
        name="ctx_attn",
    )(sinks, q, k, vt)


def _oproj_kernel(a_ref, x_ref, mod_ref, npost_ref, w_ref, o_ref):
    y = jnp.dot(a_ref[0], w_ref[...], preferred_element_type=F32)
    o_ref[0] = x_ref[0] + _rms(y, mod_ref[0, 2:3, :] * npost_ref[...])


def _oproj_call(a, x, mod, npost, w_o, tm):
    b, l, d = x.shape
    ad = a.shape[2]
    return pl.pallas_call(
        _oproj_kernel,
        grid=(b, l // tm),
        in_specs=[
            pl.BlockSpec((1, tm, ad), lambda i, t: (i, t, 0)),
            pl.BlockSpec((1, tm, d), lambda i, t: (i, t, 0)),
            pl.BlockSpec((1, 6, d), lambda i, t: (i, 0, 0)),
            pl.BlockSpec((1, d), lambda i, t: (0, 0)),
            _resident(w_o.shape),
        ],
        out_specs=pl.BlockSpec((1, tm, d), lambda i, t: (i, t, 0)),
        out_shape=jax.ShapeDtypeStruct((b, l, d), F32),
        compiler_params=_params("parallel", "parallel"),
        name="attn_out_proj",
    )(a, x, mod, npost, w_o)


def _rope_tables(l):
    axis_dim = HEAD_DIM // 2
    rows_n = l // GRID_W
    row = jnp.repeat(jnp.arange(rows_n), GRID_W).astype(F32)
    col = jnp.tile(jnp.arange(GRID_W), rows_n).astype(F32)
    inv = 1.0 / (ROPE_BASE ** (jnp.arange(0, axis_dim, 2, dtype=F32) / axis_dim))
    ang_r = row[:, None] * inv[None, :]
    ang_c = col[:, None] * inv[None, :]
    ang = jnp.concatenate([ang_r, ang_r, ang_c, ang_c], axis=-1)
    sign = jnp.tile(jnp.concatenate([-jnp.ones(axis_dim // 2, F32), jnp.ones(axis_dim // 2, F32)]), 2)
    cos, sin = jnp.cos(ang), jnp.sin(ang) * sign[None, :]
    return jnp.tile(cos, (1, LANES // HEAD_DIM)), jnp.tile(sin, (1, LANES // HEAD_DIM))


def _dup_heads(w, n_heads):
    d = w.shape[0]
    w = w.reshape(d, n_heads, 1, HEAD_DIM)
    return jnp.broadcast_to(w, (d, n_heads, 2, HEAD_DIM)).reshape(d, n_heads * HEAD_PAIR)


def _token_tile(l, target):
    return min(l, target)


def kernel(x, c, ctx, c_ctx, w_ada, b_ada, norm_pre_mix, norm_post_mix, norm_pre_ffn, norm_post_ffn,
           w_pool, pool_scale, w_qkv, w_o, attn_sinks, w_gate_up, w_down):
    b, l, d = x.shape
    depth = w_ada.shape[0]
    n_mixers = 2
    q_dim = w_o.shape[1]
    kv_heads = (w_qkv.shape[2] - q_dim) // (2 * HEAD_DIM)
    kv_dim = kv_heads * HEAD_DIM

    rows = -(-(b + 1) // BF16_ROWS) * BF16_ROWS
    c_rows = jnp.concatenate([c, c_ctx[None, :], jnp.zeros((rows - b - 1, d), F32)], axis=0)
    ada = _ada_call(c_rows, w_ada, b_ada).reshape(depth, rows, 6, d)
    mod_x = ada[:, :b]
    mod_c = jnp.broadcast_to(ada[:, b:b + 1], (depth, b, 6, d))

    w_gu_bf = w_gate_up.astype(BF16)
    w_dn_bf = w_down.astype(BF16)
    w_pool_bf = w_pool.astype(BF16)
    w_o_bf = w_o.astype(BF16)
    w_qkv_bf = w_qkv.astype(BF16)
    w_qkv_dup = jnp.concatenate([
        w_qkv_bf[:, :, :q_dim],
        jax.vmap(lambda w: _dup_heads(w, kv_heads))(w_qkv_bf[:, :, q_dim:q_dim + kv_dim]),
        w_qkv_bf[:, :, q_dim + kv_dim:],
    ], axis=-1)
    tables = _rope_tables(l)

    lc = ctx.shape[1]
    tm_x = _token_tile(l, 512)
    tm_ffn = _token_tile(l, 1024)
    tm_c = _token_tile(b * lc, 512)
    tf = 512

    flat = lambda a: a.reshape(1, b * lc, a.shape[-1])
    unflat = lambda a: a.reshape(b, lc, a.shape[-1])

    for i in range(depth):
        last = i == depth - 1
        j = i // n_mixers
        npre, npost = norm_pre_mix[i][None, :], norm_post_mix[i][None, :]
        if i % n_mixers == 0:
            ps = pool_scale[j][None, :]
            x = _pool_call(x, mod_x[i], npre, npost, w_pool_bf[j], ps, tm_x)
            if not last:
                ctx = _pool_call(ctx, mod_c[i], npre, npost, w_pool_bf[j], ps, _token_tile(lc, 512))
        else:
            q, k, v = _qkv_call(x, mod_x[i], npre, w_qkv_dup[j], q_dim, tables, tm_x)
            qc, kc, vct = _qkv_call(flat(ctx), mod_c[i][:1], npre, w_qkv_dup[j], q_dim, None, tm_c)
            a = _attn_call(q, k, v, kc, vct, attn_sinks[j], lc)
            x = _oproj_call(a, x, mod_x[i], npost, w_o_bf[j], tm_x)
            if not last:
                ac = _ctx_attn_call(qc, kc, vct, attn_sinks[j], lc)
                ctx = unflat(_oproj_call(ac, flat(ctx), mod_c[i][:1], npost, w_o_bf[j], tm_c))

        npre, npost = norm_pre_ffn[i][None, :], norm_post_ffn[i][None, :]
        x = _ffn_call(x, mod_x[i], npre, npost, w_gu_bf[i], w_dn_bf[i], tm_ffn, tf)
        if not last:
            ctx = unflat(_ffn_call(flat(ctx), mod_c[i][:1], npre, npost, w_gu_bf[i], w_dn_bf[i], _token_tile(b * lc, 1024), tf))
    return x
```

```python
import functools

import jax
import jax.numpy as jnp
from jax import lax
from jax.experimental import pallas as pl
from jax.experimental.pallas import tpu as pltpu

GRID_W = 64
POOL_WINDOWS = (2, 4, 8, 16)
HEAD_DIM = 64
GQA_GROUP = 8
WINDOW = 128
Q_BLOCK = 128
ROPE_BASE = 10000.0
RMS_EPS = 1e-6
NEG_INF = -1e30
LOG2E = 1.4426950408889634

LANES = 128
SUBLANES = 8
BF16_ROWS = 16
VMEM_LIMIT_BYTES = 56 * 1024 * 1024
SWIGLU_VMEM_LIMIT_BYTES = 63 * 1024 * 1024

ROW_SLAB = BF16_ROWS
POOL_HALO = SUBLANES
HEAD_PAIR = 2 * HEAD_DIM

F32 = jnp.float32
BF16 = jnp.bfloat16


def _params(*semantics, vmem_limit_bytes=VMEM_LIMIT_BYTES):
    return pltpu.CompilerParams(dimension_semantics=semantics, vmem_limit_bytes=vmem_limit_bytes)


def _resident(shape):
    nd = len(shape)
    return pl.BlockSpec(shape, lambda *_: (0,) * nd, pipeline_mode=pl.Buffered(1))


def _rms(x, w):
    ms = jnp.mean(x * x, axis=-1, keepdims=True)
    return x * lax.rsqrt(ms + RMS_EPS) * w


def _norm_mod(x, w, shift, scale):
    return _rms(x, w * (1.0 + scale)) + shift


def _ada_kernel(c_ref, w_ref, b_ref, o_ref):
    s = jax.nn.silu(c_ref[...]).astype(BF16)
    o_ref[0] = jnp.dot(s, w_ref[0].astype(BF16), preferred_element_type=F32) + b_ref[0]


def _ada_call(c_rows, w_ada, b_ada):
    depth, d, n = w_ada.shape
    rows = c_rows.shape[0]
    tn = 1024
    return pl.pallas_call(
        _ada_kernel,
        grid=(depth, n // tn),
        in_specs=[
            pl.BlockSpec((rows, d), lambda i, j: (0, 0)),
            pl.BlockSpec((1, d, tn), lambda i, j: (i, 0, j)),
            pl.BlockSpec((1, 1, tn), lambda i, j: (i, 0, j)),
        ],
        out_specs=pl.BlockSpec((1, rows, tn), lambda i, j: (i, 0, j)),
        out_shape=jax.ShapeDtypeStruct((depth, rows, n), F32),
        compiler_params=_params("parallel", "parallel"),
        name="ada_proj",
    )(c_rows, w_ada, b_ada.reshape(depth, 1, n))


def _ffn_kernel(x_ref, mod_ref, npre_ref, npost_ref, wg_ref, wu_ref, wd_ref, o_ref, h_ref):
    k = pl.program_id(2)
    last = pl.num_programs(2) - 1
    tm = x_ref.shape[1]

    def prologue():
        w = npre_ref[...] * (1.0 + mod_ref[0, 4:5, :])
        shift = mod_ref[0, 3:4, :]
        for i in range(tm // ROW_SLAB):
            rows = pl.ds(i * ROW_SLAB, ROW_SLAB)
            h_ref[rows, :] = (_rms(x_ref[0, rows, :], w) + shift).astype(BF16)

    def chunk(first):
        h = h_ref[...]
        g = jnp.dot(h, wg_ref[...], preferred_element_type=F32)
        u = jnp.dot(h, wu_ref[...], preferred_element_type=F32)
        a = (jax.nn.silu(g) * u).astype(BF16)
        part = jnp.dot(a, wd_ref[...], preferred_element_type=F32)
        if first:
            o_ref[0] = part
        else:
            o_ref[0] += part

    def epilogue():
        w = mod_ref[0, 5:6, :] * npost_ref[...]
        for i in range(tm // ROW_SLAB):
            rows = pl.ds(i * ROW_SLAB, ROW_SLAB)
            o_ref[0, rows, :] = x_ref[0, rows, :] + _rms(o_ref[0, rows, :], w)

    @pl.when(k == 0)
    def _():
        prologue()
        chunk(True)

    @pl.when((k > 0) & (k < last))
    def _():
        chunk(False)

    @pl.when(k == last)
    def _():
        chunk(False)
        epilogue()


def _ffn_call(x, mod, npre, npost, w_gate_up, w_down, tm, tf):
    b, l, d = x.shape
    f = w_down.shape[0]
    nk = f // tf
    return pl.pallas_call(
        _ffn_kernel,
        grid=(b, l // tm, nk),
        in_specs=[
            pl.BlockSpec((1, tm, d), lambda i, t, k: (i, t, 0)),
            pl.BlockSpec((1, 6, d), lambda i, t, k: (i, 0, 0)),
            pl.BlockSpec((1, d), lambda i, t, k: (0, 0)),
            pl.BlockSpec((1, d), lambda i, t, k: (0, 0)),
            pl.BlockSpec((d, tf), lambda i, t, k: (0, k)),
            pl.BlockSpec((d, tf), lambda i, t, k: (0, nk + k)),
            pl.BlockSpec((tf, d), lambda i, t, k: (k, 0)),
        ],
        out_specs=pl.BlockSpec((1, tm, d), lambda i, t, k: (i, t, 0)),
        out_shape=jax.ShapeDtypeStruct((b, l, d), F32),
        scratch_shapes=[pltpu.VMEM((tm, d), BF16)],
        compiler_params=_params("parallel", "parallel", "arbitrary", vmem_limit_bytes=SWIGLU_VMEM_LIMIT_BYTES),
        name="swiglu",
    )(x, mod, npre, npost, w_gate_up, w_gate_up, w_down)


def _pool_kernel(x_ref, xp_ref, xn_ref, mod_ref, npre_ref, npost_ref, wp_ref, ps_ref, o_ref, h_ref, *, seq_len):
    t = pl.program_id(1)
    tm = x_ref.shape[1]
    gd = wp_ref.shape[1]
    d = x_ref.shape[2]
    npre = npre_ref[...]
    shift, scale = mod_ref[0, 0:1, :], mod_ref[0, 1:2, :]

    hp = _norm_mod(xp_ref[0], npre, shift, scale)
    hn = _norm_mod(xn_ref[0], npre, shift, scale)
    h_ref[0:POOL_HALO, :] = jnp.where(t > 0, hp, 0.0)
    h_ref[POOL_HALO:POOL_HALO + tm, :] = _norm_mod(x_ref[0], npre, shift, scale)
    h_ref[POOL_HALO + tm:, :] = jnp.where(t < pl.num_programs(1) - 1, hn, 0.0)

    pos = t * tm + lax.broadcasted_iota(jnp.int32, (tm, gd), 0)
    ss = jnp.zeros((tm, 1), F32)
    for g, w in enumerate(POOL_WINDOWS):
        cols = pl.ds(g * gd, gd)
        acc = h_ref[pl.ds(POOL_HALO - w // 2, tm), cols]
        for j in range(-w // 2 + 1, w // 2):
            acc = acc + h_ref[pl.ds(POOL_HALO + j, tm), cols]
        cnt = jnp.minimum(pos + w // 2, seq_len) - jnp.maximum(pos - w // 2, 0)
        p = acc / cnt.astype(F32) - h_ref[pl.ds(POOL_HALO, tm), cols]
        y = jnp.dot(p.astype(BF16), wp_ref[g], preferred_element_type=F32) * ps_ref[:, cols]
        o_ref[0, :, cols] = y
        ss = ss + jnp.sum(y * y, axis=-1, keepdims=True)

    rstd = lax.rsqrt(ss * (1.0 / d) + RMS_EPS)
    o_ref[0] = x_ref[0] + o_ref[0] * rstd * (mod_ref[0, 2:3, :] * npost_ref[...])


def _pool_call(x, mod, npre, npost, w_pool, pool_scale, tm):
    b, l, d = x.shape
    hb = tm // POOL_HALO
    last_hb = l // POOL_HALO - 1
    return pl.pallas_call(
        functools.partial(_pool_kernel, seq_len=l),
        grid=(b, l // tm),
        in_specs=[
            pl.BlockSpec((1, tm, d), lambda i, t: (i, t, 0)),
            pl.BlockSpec((1, POOL_HALO, d), lambda i, t: (i, jnp.maximum(t * hb - 1, 0), 0)),
            pl.BlockSpec((1, POOL_HALO, d), lambda i, t: (i, jnp.minimum((t + 1) * hb, last_hb), 0)),
            pl.BlockSpec((1, 6, d), lambda i, t: (i, 0, 0)),
            pl.BlockSpec((1, d), lambda i, t: (0, 0)),
            pl.BlockSpec((1, d), lambda i, t: (0, 0)),
            _resident(w_pool.shape),
            pl.BlockSpec((1, d), lambda i, t: (0, 0)),
        ],
        out_specs=pl.BlockSpec((1, tm, d), lambda i, t: (i, t, 0)),
        out_shape=jax.ShapeDtypeStruct((b, l, d), F32),
        scratch_shapes=[pltpu.VMEM((tm + 2 * POOL_HALO, d), F32)],
        compiler_params=_params("parallel", "parallel"),
        name="pool_mixer",
    )(x, x, x, mod, npre, npost, w_pool, pool_scale)


def _rope(x, cos, sin_signed, low_half):
    out = []
    for j in range(x.shape[1] // LANES):
        c = x[:, j * LANES:(j + 1) * LANES]
        rot = jnp.where(low_half, pltpu.roll(c, LANES - HEAD_DIM // 4, 1), pltpu.roll(c, HEAD_DIM // 4, 1))
        out.append(c * cos + rot * sin_signed)
    return jnp.concatenate(out, axis=1)


def _qkv_kernel(*refs, rope, q_dim, k_dim):
    if rope:
        x_ref, mod_ref, npre_ref, w_ref, cos_ref, sin_ref, q_ref, k_ref, vt_ref = refs
    else:
        x_ref, mod_ref, npre_ref, w_ref, q_ref, k_ref, vt_ref = refs
    h = _norm_mod(x_ref[0], npre_ref[...], mod_ref[0, 0:1, :], mod_ref[0, 1:2, :]).astype(BF16)
    qkv = jnp.dot(h, w_ref[...], preferred_element_type=F32)
    q = qkv[:, :q_dim]
    k = qkv[:, q_dim:q_dim + k_dim]
    v = qkv[:, q_dim + k_dim:]
    if rope:
        cos, sin_signed = cos_ref[...], sin_ref[...]
        lane = lax.broadcasted_iota(jnp.int32, cos.shape, 1)
        low_half = (lane % (HEAD_DIM // 2)) < (HEAD_DIM // 4)
        q = _rope(q, cos, sin_signed, low_half)
        k = _rope(k, cos, sin_signed, low_half)
    q_ref[0] = (q * (HEAD_DIM ** -0.5 * LOG2E)).astype(BF16)
    k_ref[0] = k.astype(BF16)
    vt_ref[0] = v.T.astype(BF16)


def _qkv_call(x, mod, npre, w_qkv_dup, q_dim, rope_tables, tm):
    b, l, d = x.shape
    n = w_qkv_dup.shape[1]
    v_dim = (n - q_dim) // 3
    k_dim = 2 * v_dim
    rope = rope_tables is not None
    in_specs = [
        pl.BlockSpec((1, tm, d), lambda i, t: (i, t, 0)),
        pl.BlockSpec((1, 6, d), lambda i, t: (i, 0, 0)),
        pl.BlockSpec((1, d), lambda i, t: (0, 0)),
        _resident(w_qkv_dup.shape),
    ]
    args = [x, mod, npre, w_qkv_dup]
    if rope:
        in_specs += [pl.BlockSpec((tm, LANES), lambda i, t: (t, 0))] * 2
        args += list(rope_tables)
    return pl.pallas_call(
        functools.partial(_qkv_kernel, rope=rope, q_dim=q_dim, k_dim=k_dim),
        grid=(b, l // tm),
        in_specs=in_specs,
        out_specs=[
            pl.BlockSpec((1, tm, q_dim), lambda i, t: (i, t, 0)),
            pl.BlockSpec((1, tm, k_dim), lambda i, t: (i, t, 0)),
            pl.BlockSpec((1, v_dim, tm), lambda i, t: (i, 0, t)),
        ],
        out_shape=[
            jax.ShapeDtypeStruct((b, l, q_dim), BF16),
            jax.ShapeDtypeStruct((b, l, k_dim), BF16),
            jax.ShapeDtypeStruct((b, v_dim, l), BF16),
        ],
        compiler_params=_params("parallel", "parallel"),
        name="qkv_rope" if rope else "qkv_ctx",
    )(*args)


def _attend_scores(q_ref, sink_ref, h, k):
    nq = q_ref.shape[1]
    low = lax.broadcasted_iota(jnp.int32, (nq, HEAD_PAIR), 1) < HEAD_DIM
    zero = jnp.zeros((), BF16)
    base = h * GQA_GROUP * HEAD_DIM
    q_rows, sink_cols = [], []
    for g in range(GQA_GROUP):
        pair = q_ref[0, :, pl.ds(base + (g // 2) * HEAD_PAIR, HEAD_PAIR)]
        q_rows.append(jnp.where(low if g % 2 == 0 else ~low, pair, zero))
        sink_cols.append(jnp.full((1, nq), sink_ref[h * GQA_GROUP + g] * LOG2E, F32))
    qg = jnp.concatenate(q_rows, axis=0)
    snk = jnp.concatenate(sink_cols, axis=1)
    s = lax.dot_general(k, qg, (((1,), (1,)), ((), ())), preferred_element_type=F32)
    return s, snk


def _attend_softmax(s, snk, masks):
    blocks, row = [], 0
    for first, bias in masks:
        if first > row:
            blocks.append(s[row:first])
        row = first + bias.shape[0]
        blocks.append(s[first:row] + bias)
    if masks:
        if row < s.shape[0]:
            blocks.append(s[row:])
        s = jnp.concatenate(blocks, axis=0)
    m = jnp.maximum(snk, jnp.max(s, axis=0, keepdims=True))
    return jnp.exp2(s - m).astype(BF16), jnp.exp2(snk - m)


def _attend_values(o_ref, h, p, sink_p, vt):
    nq = o_ref.shape[1]
    base = h * GQA_GROUP * HEAD_DIM
    vt_ones = jnp.concatenate([vt, jnp.ones((BF16_ROWS, vt.shape[1]), BF16)], axis=0)
    acc = jnp.dot(vt_ones, p, preferred_element_type=F32)
    den = sink_p + acc[HEAD_DIM:HEAD_DIM + 1]
    acc = acc[:HEAD_DIM] * (1.0 / den)
    for j in range(GQA_GROUP // 2):
        pair_t = jnp.concatenate([acc[:, (2 * j) * nq:(2 * j + 1) * nq], acc[:, (2 * j + 1) * nq:(2 * j + 2) * nq]],
                                 axis=0)
        o_ref[0, :, pl.ds(base + j * HEAD_PAIR, HEAD_PAIR)] = pair_t.T.astype(o_ref.dtype)


def _attn_kernel(sink_ref, q_ref, kp_ref, kc_ref, kn_ref, vp_ref, vc_ref, vn_ref, kx_ref, vx_ref, o_ref):
    i = pl.program_id(1)
    nq = Q_BLOCK
    shape = (nq, GQA_GROUP * nq)
    key = lax.broadcasted_iota(jnp.int32, shape, 0)
    qry = lax.broadcasted_iota(jnp.int32, shape, 1) % nq
    valid_prev = (key >= qry) & (i > 0)
    valid_next = (key <= qry) & (i < pl.num_programs(1) - 1)
    masks = [(0, jnp.where(valid_prev, 0.0, NEG_INF)), (2 * nq, jnp.where(valid_next, 0.0, NEG_INF))]
    k_all = jnp.concatenate([kp_ref[0], kc_ref[0], kn_ref[0], kx_ref[0]], axis=0)
    vt_all = jnp.concatenate([vp_ref[0], vc_ref[0], vn_ref[0], vx_ref[0]], axis=1)
    n_kv = vt_all.shape[0] // HEAD_DIM
    scores, probs = {}, {}
    for h in range(n_kv + 2):
        if h < n_kv:
            scores[h] = _attend_scores(q_ref, sink_ref, h, k_all[:, h * HEAD_PAIR:(h + 1) * HEAD_PAIR])
        if 0 <= h - 2 < n_kv:
            _attend_values(o_ref, h - 2, *probs.pop(h - 2), vt_all[(h - 2) * HEAD_DIM:(h - 1) * HEAD_DIM, :])
        if 0 <= h - 1 < n_kv:
            probs[h - 1] = _attend_softmax(*scores.pop(h - 1), masks)


def _attn_call(q, k, vt, kx, vxt, sinks, n_ctx):
    b, l, qd = q.shape
    kd, vd = k.shape[2], vt.shape[1]
    nb = l // Q_BLOCK
    k_spec = lambda f: pl.BlockSpec((1, Q_BLOCK, kd), lambda i, t: (i, f(t), 0))
    v_spec = lambda f: pl.BlockSpec((1, vd, Q_BLOCK), lambda i, t: (i, 0, f(t)))
    prev = lambda t: jnp.maximum(t - 1, 0)
    cur = lambda t: t
    nxt = lambda t: jnp.minimum(t + 1, nb - 1)
    return pl.pallas_call(
        _attn_kernel,
        grid=(b, nb),
        in_specs=[
            pl.BlockSpec(memory_space=pltpu.SMEM),
            pl.BlockSpec((1, Q_BLOCK, qd), lambda i, t: (i, t, 0)),
            k_spec(prev), k_spec(cur), k_spec(nxt),
            v_spec(prev), v_spec(cur), v_spec(nxt),
            pl.BlockSpec((1, n_ctx, kd), lambda i, t: (0, i, 0)),
            pl.BlockSpec((1, vd, n_ctx), lambda i, t: (0, 0, i)),
        ],
        out_specs=pl.BlockSpec((1, Q_BLOCK, qd), lambda i, t: (i, t, 0)),
        out_shape=jax.ShapeDtypeStruct((b, l, qd), BF16),
        compiler_params=_params("parallel", "parallel"),
        name="window_attn",
    )(sinks, q, k, k, k, vt, vt, vt, kx, vxt)


def _ctx_attn_kernel(sink_ref, q_ref, k_ref, vt_ref, o_ref):
    for h in range(vt_ref.shape[1] // HEAD_DIM):
        sc = _attend_scores(q_ref, sink_ref, h, k_ref[0, :, h * HEAD_PAIR:(h + 1) * HEAD_PAIR])
        _attend_values(o_ref, h, *_attend_softmax(*sc, []), vt_ref[0, h * HEAD_DIM:(h + 1) * HEAD_DIM, :])


def _ctx_attn_call(q, k, vt, sinks, n_ctx):
    qd, kd, vd = q.shape[2], k.shape[2], vt.shape[1]
    return pl.pallas_call(
        _ctx_attn_kernel,
        grid=(q.shape[1] // n_ctx,),
        in_specs=[
            pl.BlockSpec(memory_space=pltpu.SMEM),
            pl.BlockSpec((1, n_ctx, qd), lambda i: (0, i, 0)),
            pl.BlockSpec((1, n_ctx, kd), lambda i: (0, i, 0)),
            pl.BlockSpec((1, vd, n_ctx), lambda i: (0, 0, i)),
        ],
        out_specs=pl.BlockSpec((1, n_ctx, qd), lambda i: (0, i, 0)),
        out_shape=jax.ShapeDtypeStruct(q.shape, BF16),
        compiler_params=_params("parallel"),
        name="ctx_attn",
    )(sinks, q, k, vt)


def _oproj_kernel(a_ref, x_ref, mod_ref, npost_ref, w_ref, o_ref):
    y = jnp.dot(a_ref[0], w_ref[...], preferred_element_type=F32)
    o_ref[0] = x_ref[0] + _rms(y, mod_ref[0, 2:3, :] * npost_ref[...])


def _oproj_call(a, x, mod, npost, w_o, tm):
    b, l, d = x.shape
    ad = a.shape[2]
    return pl.pallas_call(
        _oproj_kernel,
        grid=(b, l // tm),
        in_specs=[
            pl.BlockSpec((1, tm, ad), lambda i, t: (i, t, 0)),
            pl.BlockSpec((1, tm, d), lambda i, t: (i, t, 0)),
            pl.BlockSpec((1, 6, d), lambda i, t: (i, 0, 0)),
            pl.BlockSpec((1, d), lambda i, t: (0, 0)),
            _resident(w_o.shape),
        ],
        out_specs=pl.BlockSpec((1, tm, d), lambda i, t: (i, t, 0)),
        out_shape=jax.ShapeDtypeStruct((b, l, d), F32),
        compiler_params=_params("parallel", "parallel"),
        name="attn_out_proj",
    )(a, x, mod, npost, w_o)


def _rope_tables(l):
    axis_dim = HEAD_DIM // 2
    rows_n = l // GRID_W
    row = jnp.repeat(jnp.arange(rows_n), GRID_W).astype(F32)
    col = jnp.tile(jnp.arange(GRID_W), rows_n).astype(F32)
    inv = 1.0 / (ROPE_BASE ** (jnp.arange(0, axis_dim, 2, dtype=F32) / axis_dim))
    ang_r = row[:, None] * inv[None, :]
    ang_c = col[:, None] * inv[None, :]
    ang = jnp.concatenate([ang_r, ang_r, ang_c, ang_c], axis=-1)
    sign = jnp.tile(jnp.concatenate([-jnp.ones(axis_dim // 2, F32), jnp.ones(axis_dim // 2, F32)]), 2)
    cos, sin = jnp.cos(ang), jnp.sin(ang) * sign[None, :]
    return jnp.tile(cos, (1, LANES // HEAD_DIM)), jnp.tile(sin, (1, LANES // HEAD_DIM))


def _dup_heads(w, n_heads):
    d = w.shape[0]
    w = w.reshape(d, n_heads, 1, HEAD_DIM)
    return jnp.broadcast_to(w, (d, n_heads, 2, HEAD_DIM)).reshape(d, n_heads * HEAD_PAIR)


def _token_tile(l, target):
    return min(l, target)


def kernel(x, c, ctx, c_ctx, w_ada, b_ada, norm_pre_mix, norm_post_mix, norm_pre_ffn, norm_post_ffn,
           w_pool, pool_scale, w_qkv, w_o, attn_sinks, w_gate_up, w_down):
    b, l, d = x.shape
    depth = w_ada.shape[0]
    n_mixers = 2
    q_dim = w_o.shape[1]
    kv_heads = (w_qkv.shape[2] - q_dim) // (2 * HEAD_DIM)
    kv_dim = kv_heads * HEAD_DIM

    rows = -(-(b + 1) // BF16_ROWS) * BF16_ROWS
    c_rows = jnp.concatenate([c, c_ctx[None, :], jnp.zeros((rows - b - 1, d), F32)], axis=0)
    ada = _ada_call(c_rows, w_ada, b_ada).reshape(depth, rows, 6, d)
    mod_x = ada[:, :b]
    mod_c = jnp.broadcast_to(ada[:, b:b + 1], (depth, b, 6, d))

    w_gu_bf = w_gate_up.astype(BF16)
    w_dn_bf = w_down.astype(BF16)
    w_pool_bf = w_pool.astype(BF16)
    w_o_bf = w_o.astype(BF16)
    w_qkv_bf = w_qkv.astype(BF16)
    w_qkv_dup = jnp.concatenate([
        w_qkv_bf[:, :, :q_dim],
        jax.vmap(lambda w: _dup_heads(w, kv_heads))(w_qkv_bf[:, :, q_dim:q_dim + kv_dim]),
        w_qkv_bf[:, :, q_dim + kv_dim:],
    ], axis=-1)
    tables = _rope_tables(l)

    lc = ctx.shape[1]
    tm_x = _token_tile(l, 512)
    tm_ffn = _token_tile(l, 1024)
    tm_c = _token_tile(b * lc, 512)
    tf = 512

    flat = lambda a: a.reshape(1, b * lc, a.shape[-1])
    unflat = lambda a: a.reshape(b, lc, a.shape[-1])

    for i in range(depth):
        last = i == depth - 1
        j = i // n_mixers
        npre, npost = norm_pre_mix[i][None, :], norm_post_mix[i][None, :]
        if i % n_mixers == 0:
            ps = pool_scale[j][None, :]
            x = _pool_call(x, mod_x[i], npre, npost, w_pool_bf[j], ps, tm_x)
            if not last:
                ctx = _pool_call(ctx, mod_c[i], npre, npost, w_pool_bf[j], ps, _token_tile(lc, 512))
        else:
            q, k, v = _qkv_call(x, mod_x[i], npre, w_qkv_dup[j], q_dim, tables, tm_x)
            qc, kc, vct = _qkv_call(flat(ctx), mod_c[i][:1], npre, w_qkv_dup[j], q_dim, None, tm_c)
            a = _attn_call(q, k, v, kc, vct, attn_sinks[j], lc)
            x = _oproj_call(a, x, mod_x[i], npost, w_o_bf[j], tm_x)
            if not last:
                ac = _ctx_attn_call(qc, kc, vct, attn_sinks[j], lc)
                ctx = unflat(_oproj_call(ac, flat(ctx), mod_c[i][:1], npost, w_o_bf[j], tm_c))

        npre, npost = norm_pre_ffn[i][None, :], norm_post_ffn[i][None, :]
        x = _ffn_call(x, mod_x[i], npre, npost, w_gu_bf[i], w_dn_bf[i], tm_ffn, tf)
        if not last:
            ctx = unflat(_ffn_call(flat(ctx), mod_c[i][:1], npre, npost, w_gu_bf[i], w_dn_bf[i], _token_tile(b * lc, 1024), tf))
    return x
```

```python
import functools

import jax
import jax.numpy as jnp
from jax import lax
from jax.experimental import pallas as pl
from jax.experimental.pallas import tpu as pltpu

GRID_W = 64
POOL_WINDOWS = (2, 4, 8, 16)
HEAD_DIM = 64
GQA_GROUP = 8
WINDOW = 128
Q_BLOCK = 128
ROPE_BASE = 10000.0
RMS_EPS = 1e-6
NEG_INF = -1e30
LOG2E = 1.4426950408889634

LANES = 128
SUBLANES = 8
BF16_ROWS = 16
VMEM_LIMIT_BYTES = 56 * 1024 * 1024
SWIGLU_VMEM_LIMIT_BYTES = 63 * 1024 * 1024

ROW_SLAB = BF16_ROWS
POOL_HALO = SUBLANES
HEAD_PAIR = 2 * HEAD_DIM

F32 = jnp.float32
BF16 = jnp.bfloat16


def _params(*semantics, vmem_limit_bytes=VMEM_LIMIT_BYTES):
    return pltpu.CompilerParams(dimension_semantics=semantics, vmem_limit_bytes=vmem_limit_bytes)


def _resident(shape):
    nd = len(shape)
    return pl.BlockSpec(shape, lambda *_: (0,) * nd, pipeline_mode=pl.Buffered(1))


def _rms(x, w):
    ms = jnp.mean(x * x, axis=-1, keepdims=True)
    return x * lax.rsqrt(ms + RMS_EPS) * w


def _norm_mod(x, w, shift, scale):
    return _rms(x, w * (1.0 + scale)) + shift


def _ada_kernel(c_ref, w_ref, b_ref, o_ref):
    s = jax.nn.silu(c_ref[...]).astype(BF16)
    o_ref[0] = jnp.dot(s, w_ref[0].astype(BF16), preferred_element_type=F32) + b_ref[0]


def _ada_call(c_rows, w_ada, b_ada):
    depth, d, n = w_ada.shape
    rows = c_rows.shape[0]
    tn = 1024
    return pl.pallas_call(
        _ada_kernel,
        grid=(depth, n // tn),
        in_specs=[
            pl.BlockSpec((rows, d), lambda i, j: (0, 0)),
            pl.BlockSpec((1, d, tn), lambda i, j: (i, 0, j)),
            pl.BlockSpec((1, 1, tn), lambda i, j: (i, 0, j)),
        ],
        out_specs=pl.BlockSpec((1, rows, tn), lambda i, j: (i, 0, j)),
        out_shape=jax.ShapeDtypeStruct((depth, rows, n), F32),
        compiler_params=_params("parallel", "parallel"),
        name="ada_proj",
    )(c_rows, w_ada, b_ada.reshape(depth, 1, n))


def _ffn_kernel(x_ref, mod_ref, npre_ref, npost_ref, wg_ref, wu_ref, wd_ref, o_ref, h_ref):
    k = pl.program_id(2)
    last = pl.num_programs(2) - 1
    tm = x_ref.shape[1]

    def prologue():
        w = npre_ref[...] * (1.0 + mod_ref[0, 4:5, :])
        shift = mod_ref[0, 3:4, :]
        for i in range(tm // ROW_SLAB):
            rows = pl.ds(i * ROW_SLAB, ROW_SLAB)
            h_ref[rows, :] = (_rms(x_ref[0, rows, :], w) + shift).astype(BF16)

    def chunk(first):
        h = h_ref[...]
        g = jnp.dot(h, wg_ref[...], preferred_element_type=F32)
        u = jnp.dot(h, wu_ref[...], preferred_element_type=F32)
        a = (jax.nn.silu(g) * u).astype(BF16)
        part = jnp.dot(a, wd_ref[...], preferred_element_type=F32)
        if first:
            o_ref[0] = part
        else:
            o_ref[0] += part

    def epilogue():
        w = mod_ref[0, 5:6, :] * npost_ref[...]
        for i in range(tm // ROW_SLAB):
            rows = pl.ds(i * ROW_SLAB, ROW_SLAB)
            o_ref[0, rows, :] = x_ref[0, rows, :] + _rms(o_ref[0, rows, :], w)

    @pl.when(k == 0)
    def _():
        prologue()
        chunk(True)

    @pl.when((k > 0) & (k < last))
    def _():
        chunk(False)

    @pl.when(k == last)
    def _():
        chunk(False)
        epilogue()


def _ffn_call(x, mod, npre, npost, w_gate_up, w_down, tm, tf):
    b, l, d = x.shape
    f = w_down.shape[0]
    nk = f // tf
    return pl.pallas_call(
        _ffn_kernel,
        grid=(b, l // tm, nk),
        in_specs=[
            pl.BlockSpec((1, tm, d), lambda i, t, k: (i, t, 0)),
            pl.BlockSpec((1, 6, d), lambda i, t, k: (i, 0, 0)),
            pl.BlockSpec((1, d), lambda i, t, k: (0, 0)),
            pl.BlockSpec((1, d), lambda i, t, k: (0, 0)),
            pl.BlockSpec((d, tf), lambda i, t, k: (0, k)),
            pl.BlockSpec((d, tf), lambda i, t, k: (0, nk + k)),
            pl.BlockSpec((tf, d), lambda i, t, k: (k, 0)),
        ],
        out_specs=pl.BlockSpec((1, tm, d), lambda i, t, k: (i, t, 0)),
        out_shape=jax.ShapeDtypeStruct((b, l, d), F32),
        scratch_shapes=[pltpu.VMEM((tm, d), BF16)],
        compiler_params=_params("parallel", "parallel", "arbitrary", vmem_limit_bytes=SWIGLU_VMEM_LIMIT_BYTES),
        name="swiglu",
    )(x, mod, npre, npost, w_gate_up, w_gate_up, w_down)


def _pool_kernel(x_ref, xp_ref, xn_ref, mod_ref, npre_ref, npost_ref, wp_ref, ps_ref, o_ref, h_ref, lv_ref, *,
                 seq_len):
    t = pl.program_id(1)
    tm = x_ref.shape[1]
    gd = wp_ref.shape[1]
    d = x_ref.shape[2]
    npre = npre_ref[...]
    shift, scale = mod_ref[0, 0:1, :], mod_ref[0, 1:2, :]
    tile0 = 2 * POOL_HALO
    span = tm + 2 * POOL_HALO

    hp = _norm_mod(xp_ref[0], npre, shift, scale)
    hn = _norm_mod(xn_ref[0], npre, shift, scale)
    zeros = jnp.zeros((POOL_HALO, gd), F32)
    n_groups = len(POOL_WINDOWS)

    def put(rows, value):
        for g in range(n_groups):
            h_ref[g, rows, :] = value[:, g * gd:(g + 1) * gd]

    put(pl.ds(POOL_HALO, POOL_HALO), jnp.where(t > 0, hp, 0.0))
    w_mod = npre * (1.0 + scale)
    for i in range(tm // ROW_SLAB):
        put(pl.ds(tile0 + i * ROW_SLAB, ROW_SLAB), _rms(x_ref[0, pl.ds(i * ROW_SLAB, ROW_SLAB), :], w_mod) + shift)
    put(pl.ds(tile0 + tm, POOL_HALO), jnp.where(t < pl.num_programs(1) - 1, hn, 0.0))
    for g in range(n_groups):
        h_ref[g, 0:POOL_HALO, :] = zeros
        h_ref[g, POOL_HALO + span:, :] = zeros
    for buf in range(2):
        lv_ref[buf, 0:POOL_HALO, :] = zeros
        lv_ref[buf, POOL_HALO + span:, :] = zeros

    pos = t * tm + lax.broadcasted_iota(jnp.int32, (tm, LANES), 0)
    ss = jnp.zeros((tm, 1), F32)
    for g, w in enumerate(POOL_WINDOWS):
        cols = pl.ds(g * gd, gd)
        cur = h_ref[g, pl.ds(POOL_HALO - 1, span), :] + h_ref[g, pl.ds(POOL_HALO, span), :]
        reach = 1
        while 2 * reach < w:
            buf = lv_ref.at[reach.bit_length() % 2]
            buf[pl.ds(POOL_HALO, span), :] = cur
            cur = buf[pl.ds(POOL_HALO - reach, span), :] + buf[pl.ds(POOL_HALO + reach, span), :]
            reach *= 2
        acc = cur[POOL_HALO:POOL_HALO + tm]
        cnt = jnp.minimum(pos + w // 2, seq_len) - jnp.maximum(pos - w // 2, 0)
        inv_cnt = jnp.concatenate([1.0 / cnt.astype(F32)] * (gd // LANES), axis=1)
        p = acc * inv_cnt - h_ref[g, pl.ds(tile0, tm), :]
        y = jnp.dot(p.astype(BF16), wp_ref[g], preferred_element_type=F32) * ps_ref[:, cols]
        o_ref[0, :, cols] = y
        ss = ss + jnp.sum(y * y, axis=-1, keepdims=True)

    rstd = lax.rsqrt(ss * (1.0 / d) + RMS_EPS)
    o_ref[0] = x_ref[0] + o_ref[0] * rstd * (mod_ref[0, 2:3, :] * npost_ref[...])


def _pool_call(x, mod, npre, npost, w_pool, pool_scale, tm):
    b, l, d = x.shape
    hb = tm // POOL_HALO
    last_hb = l // POOL_HALO - 1
    return pl.pallas_call(
        functools.partial(_pool_kernel, seq_len=l),
        grid=(b, l // tm),
        in_specs=[
            pl.BlockSpec((1, tm, d), lambda i, t: (i, t, 0)),
            pl.BlockSpec((1, POOL_HALO, d), lambda i, t: (i, jnp.maximum(t * hb - 1, 0), 0)),
            pl.BlockSpec((1, POOL_HALO, d), lambda i, t: (i, jnp.minimum((t + 1) * hb, last_hb), 0)),
            pl.BlockSpec((1, 6, d), lambda i, t: (i, 0, 0)),
            pl.BlockSpec((1, d), lambda i, t: (0, 0)),
            pl.BlockSpec((1, d), lambda i, t: (0, 0)),
            _resident(w_pool.shape),
            pl.BlockSpec((1, d), lambda i, t: (0, 0)),
        ],
        out_specs=pl.BlockSpec((1, tm, d), lambda i, t: (i, t, 0)),
        out_shape=jax.ShapeDtypeStruct((b, l, d), F32),
        scratch_shapes=[pltpu.VMEM((w_pool.shape[0], tm + 4 * POOL_HALO, w_pool.shape[1]), F32),
                        pltpu.VMEM((2, tm + 4 * POOL_HALO, w_pool.shape[1]), F32)],
        compiler_params=_params("parallel", "parallel"),
        name="pool_mixer",
    )(x, x, x, mod, npre, npost, w_pool, pool_scale)


def _rope(x, cos, sin_signed, low_half):
    out = []
    for j in range(x.shape[1] // LANES):
        c = x[:, j * LANES:(j + 1) * LANES]
        rot = jnp.where(low_half, pltpu.roll(c, LANES - HEAD_DIM // 4, 1), pltpu.roll(c, HEAD_DIM // 4, 1))
        out.append(c * cos + rot * sin_signed)
    return jnp.concatenate(out, axis=1)


def _qkv_kernel(*refs, rope, q_dim, k_dim):
    if rope:
        x_ref, mod_ref, npre_ref, w_ref, cos_ref, sin_ref, q_ref, k_ref, vt_ref = refs
    else:
        x_ref, mod_ref, npre_ref, w_ref, q_ref, k_ref, vt_ref = refs
    h = _norm_mod(x_ref[0], npre_ref[...], mod_ref[0, 0:1, :], mod_ref[0, 1:2, :]).astype(BF16)
    qkv = jnp.dot(h, w_ref[...], preferred_element_type=F32)
    q = qkv[:, :q_dim]
    k = qkv[:, q_dim:q_dim + k_dim]
    v = qkv[:, q_dim + k_dim:]
    if rope:
        cos, sin_signed = cos_ref[...], sin_ref[...]
        lane = lax.broadcasted_iota(jnp.int32, cos.shape, 1)
        low_half = (lane % (HEAD_DIM // 2)) < (HEAD_DIM // 4)
        q = _rope(q, cos, sin_signed, low_half)
        k = _rope(k, cos, sin_signed, low_half)
    q_ref[0] = (q * (HEAD_DIM ** -0.5 * LOG2E)).astype(BF16)
    k_ref[0] = k.astype(BF16)
    vt_ref[0] = v.T.astype(BF16)


def _qkv_call(x, mod, npre, w_qkv_dup, q_dim, rope_tables, tm):
    b, l, d = x.shape
    n = w_qkv_dup.shape[1]
    v_dim = (n - q_dim) // 3
    k_dim = 2 * v_dim
    rope = rope_tables is not None
    in_specs = [
        pl.BlockSpec((1, tm, d), lambda i, t: (i, t, 0)),
        pl.BlockSpec((1, 6, d), lambda i, t: (i, 0, 0)),
        pl.BlockSpec((1, d), lambda i, t: (0, 0)),
        _resident(w_qkv_dup.shape),
    ]
    args = [x, mod, npre, w_qkv_dup]
    if rope:
        in_specs += [pl.BlockSpec((tm, LANES), lambda i, t: (t, 0))] * 2
        args += list(rope_tables)
    return pl.pallas_call(
        functools.partial(_qkv_kernel, rope=rope, q_dim=q_dim, k_dim=k_dim),
        grid=(b, l // tm),
        in_specs=in_specs,
        out_specs=[
            pl.BlockSpec((1, tm, q_dim), lambda i, t: (i, t, 0)),
            pl.BlockSpec((1, tm, k_dim), lambda i, t: (i, t, 0)),
            pl.BlockSpec((1, v_dim, tm), lambda i, t: (i, 0, t)),
        ],
        out_shape=[
            jax.ShapeDtypeStruct((b, l, q_dim), BF16),
            jax.ShapeDtypeStruct((b, l, k_dim), BF16),
            jax.ShapeDtypeStruct((b, v_dim, l), BF16),
        ],
        compiler_params=_params("parallel", "parallel"),
        name="qkv_rope" if rope else "qkv_ctx",
    )(*args)


def _attend_scores(q_ref, sink_ref, h, k):
    nq = q_ref.shape[1]
    low = lax.broadcasted_iota(jnp.int32, (nq, HEAD_PAIR), 1) < HEAD_DIM
    zero = jnp.zeros((), BF16)
    base = h * GQA_GROUP * HEAD_DIM
    q_rows, sink_cols = [], []
    for g in range(GQA_GROUP):
        pair = q_ref[0, :, pl.ds(base + (g // 2) * HEAD_PAIR, HEAD_PAIR)]
        q_rows.append(jnp.where(low if g % 2 == 0 else ~low, pair, zero))
        sink_cols.append(jnp.full((1, nq), sink_ref[h * GQA_GROUP + g] * LOG2E, F32))
    qg = jnp.concatenate(q_rows, axis=0)
    snk = jnp.concatenate(sink_cols, axis=1)
    s = lax.dot_general(k, qg, (((1,), (1,)), ((), ())), preferred_element_type=F32)
    return s, snk


def _attend_softmax(s, snk, masks):
    blocks, row = [], 0
    for first, bias in masks:
        if first > row:
            blocks.append(s[row:first])
        row = first + bias.shape[0]
        blocks.append(s[first:row] + bias)
    if masks:
        if row < s.shape[0]:
            blocks.append(s[row:])
        s = jnp.concatenate(blocks, axis=0)
    m = jnp.maximum(snk, jnp.max(s, axis=0, keepdims=True))
    return jnp.exp2(s - m).astype(BF16), jnp.exp2(snk - m)


def _attend_values(o_ref, h, p, sink_p, vt):
    nq = o_ref.shape[1]
    base = h * GQA_GROUP * HEAD_DIM
    vt_ones = jnp.concatenate([vt, jnp.ones((BF16_ROWS, vt.shape[1]), BF16)], axis=0)
    acc = jnp.dot(vt_ones, p, preferred_element_type=F32)
    den = sink_p + acc[HEAD_DIM:HEAD_DIM + 1]
    acc = acc[:HEAD_DIM] * (1.0 / den)
    for j in range(GQA_GROUP // 2):
        pair_t = jnp.concatenate([acc[:, (2 * j) * nq:(2 * j + 1) * nq], acc[:, (2 * j + 1) * nq:(2 * j + 2) * nq]],
                                 axis=0)
        o_ref[0, :, pl.ds(base + j * HEAD_PAIR, HEAD_PAIR)] = pair_t.T.astype(o_ref.dtype)


def _attn_kernel(sink_ref, q_ref, kp_ref, kc_ref, kn_ref, vp_ref, vc_ref, vn_ref, kx_ref, vx_ref, o_ref):
    i = pl.program_id(1)
    nq = Q_BLOCK
    shape = (nq, GQA_GROUP * nq)
    key = lax.broadcasted_iota(jnp.int32, shape, 0)
    qry = lax.broadcasted_iota(jnp.int32, shape, 1) % nq
    valid_prev = (key >= qry) & (i > 0)
    valid_next = (key <= qry) & (i < pl.num_programs(1) - 1)
    masks = [(0, jnp.where(valid_prev, 0.0, NEG_INF)), (2 * nq, jnp.where(valid_next, 0.0, NEG_INF))]
    k_all = jnp.concatenate([kp_ref[0], kc_ref[0], kn_ref[0], kx_ref[0]], axis=0)
    vt_all = jnp.concatenate([vp_ref[0], vc_ref[0], vn_ref[0], vx_ref[0]], axis=1)
    n_kv = vt_all.shape[0] // HEAD_DIM
    scores, probs = {}, {}
    for h in range(n_kv + 2):
        if h < n_kv:
            scores[h] = _attend_scores(q_ref, sink_ref, h, k_all[:, h * HEAD_PAIR:(h + 1) * HEAD_PAIR])
        if 0 <= h - 2 < n_kv:
            _attend_values(o_ref, h - 2, *probs.pop(h - 2), vt_all[(h - 2) * HEAD_DIM:(h - 1) * HEAD_DIM, :])
        if 0 <= h - 1 < n_kv:
            probs[h - 1] = _attend_softmax(*scores.pop(h - 1), masks)


def _attn_call(q, k, vt, kx, vxt, sinks, n_ctx):
    b, l, qd = q.shape
    kd, vd = k.shape[2], vt.shape[1]
    nb = l // Q_BLOCK
    k_spec = lambda f: pl.BlockSpec((1, Q_BLOCK, kd), lambda i, t: (i, f(t), 0))
    v_spec = lambda f: pl.BlockSpec((1, vd, Q_BLOCK), lambda i, t: (i, 0, f(t)))
    prev = lambda t: jnp.maximum(t - 1, 0)
    cur = lambda t: t
    nxt = lambda t: jnp.minimum(t + 1, nb - 1)
    return pl.pallas_call(
        _attn_kernel,
        grid=(b, nb),
        in_specs=[
            pl.BlockSpec(memory_space=pltpu.SMEM),
            pl.BlockSpec((1, Q_BLOCK, qd), lambda i, t: (i, t, 0)),
            k_spec(prev), k_spec(cur), k_spec(nxt),
            v_spec(prev), v_spec(cur), v_spec(nxt),
            pl.BlockSpec((1, n_ctx, kd), lambda i, t: (0, i, 0)),
            pl.BlockSpec((1, vd, n_ctx), lambda i, t: (0, 0, i)),
        ],
        out_specs=pl.BlockSpec((1, Q_BLOCK, qd), lambda i, t: (i, t, 0)),
        out_shape=jax.ShapeDtypeStruct((b, l, qd), BF16),
        compiler_params=_params("parallel", "parallel"),
        name="window_attn",
    )(sinks, q, k, k, k, vt, vt, vt, kx, vxt)


def _ctx_attn_kernel(sink_ref, q_ref, k_ref, vt_ref, o_ref):
    for h in range(vt_ref.shape[1] // HEAD_DIM):
        sc = _attend_scores(q_ref, sink_ref, h, k_ref[0, :, h * HEAD_PAIR:(h + 1) * HEAD_PAIR])
        _attend_values(o_ref, h, *_attend_softmax(*sc, []), vt_ref[0, h * HEAD_DIM:(h + 1) * HEAD_DIM, :])


def _ctx_attn_call(q, k, vt, sinks, n_ctx):
    qd, kd, vd = q.shape[2], k.shape[2], vt.shape[1]
    return pl.pallas_call(
        _ctx_attn_kernel,
        grid=(q.shape[1] // n_ctx,),
        in_specs=[
            pl.BlockSpec(memory_space=pltpu.SMEM),
            pl.BlockSpec((1, n_ctx, qd), lambda i: (0, i, 0)),
            pl.BlockSpec((1, n_ctx, kd), lambda i: (0, i, 0)),
            pl.BlockSpec((1, vd, n_ctx), lambda i: (0, 0, i)),
        ],
        out_specs=pl.BlockSpec((1, n_ctx, qd), lambda i: (0, i, 0)),
        out_shape=jax.ShapeDtypeStruct(q.shape, BF16),
        compiler_params=_params("parallel"),
        name="ctx_attn",
    )(sinks, q, k, vt)


def _oproj_kernel(a_ref, x_ref, mod_ref, npost_ref, w_ref, o_ref):
    y = jnp.dot(a_ref[0], w_ref[...], preferred_element_type=F32)
    o_ref[0] = x_ref[0] + _rms(y, mod_ref[0, 2:3, :] * npost_ref[...])


def _oproj_call(a, x, mod, npost, w_o, tm):
    b, l, d = x.shape
    ad = a.shape[2]
    return pl.pallas_call(
        _oproj_kernel,
        grid=(b, l // tm),
        in_specs=[
            pl.BlockSpec((1, tm, ad), lambda i, t: (i, t, 0)),
            pl.BlockSpec((1, tm, d), lambda i, t: (i, t, 0)),
            pl.BlockSpec((1, 6, d), lambda i, t: (i, 0, 0)),
            pl.BlockSpec((1, d), lambda i, t: (0, 0)),
            _resident(w_o.shape),
        ],
        out_specs=pl.BlockSpec((1, tm, d), lambda i, t: (i, t, 0)),
        out_shape=jax.ShapeDtypeStruct((b, l, d), F32),
        compiler_params=_params("parallel", "parallel"),
        name="attn_out_proj",
    )(a, x, mod, npost, w_o)


def _rope_tables(l):
    axis_dim = HEAD_DIM // 2
    rows_n = l // GRID_W
    row = jnp.repeat(jnp.arange(rows_n), GRID_W).astype(F32)
    col = jnp.tile(jnp.arange(GRID_W), rows_n).astype(F32)
    inv = 1.0 / (ROPE_BASE ** (jnp.arange(0, axis_dim, 2, dtype=F32) / axis_dim))
    ang_r = row[:, None] * inv[None, :]
    ang_c = col[:, None] * inv[None, :]
    ang = jnp.concatenate([ang_r, ang_r, ang_c, ang_c], axis=-1)
    sign = jnp.tile(jnp.concatenate([-jnp.ones(axis_dim // 2, F32), jnp.ones(axis_dim // 2, F32)]), 2)
    cos, sin = jnp.cos(ang), jnp.sin(ang) * sign[None, :]
    return jnp.tile(cos, (1, LANES // HEAD_DIM)), jnp.tile(sin, (1, LANES // HEAD_DIM))


def _dup_heads(w, n_heads):
    d = w.shape[0]
    w = w.reshape(d, n_heads, 1, HEAD_DIM)
    return jnp.broadcast_to(w, (d, n_heads, 2, HEAD_DIM)).reshape(d, n_heads * HEAD_PAIR)


def _token_tile(l, target):
    return min(l, target)


def kernel(x, c, ctx, c_ctx, w_ada, b_ada, norm_pre_mix, norm_post_mix, norm_pre_ffn, norm_post_ffn,
           w_pool, pool_scale, w_qkv, w_o, attn_sinks, w_gate_up, w_down):
    b, l, d = x.shape
    depth = w_ada.shape[0]
    n_mixers = 2
    q_dim = w_o.shape[1]
    kv_heads = (w_qkv.shape[2] - q_dim) // (2 * HEAD_DIM)
    kv_dim = kv_heads * HEAD_DIM

    rows = -(-(b + 1) // BF16_ROWS) * BF16_ROWS
    c_rows = jnp.concatenate([c, c_ctx[None, :], jnp.zeros((rows - b - 1, d), F32)], axis=0)
    ada = _ada_call(c_rows, w_ada, b_ada).reshape(depth, rows, 6, d)
    mod_x = ada[:, :b]
    mod_c = jnp.broadcast_to(ada[:, b:b + 1], (depth, b, 6, d))

    bf = lambda w: w.astype(BF16)

    def qkv_weights(w):
        w = bf(w)
        return jnp.concatenate([w[:, :q_dim], _dup_heads(w[:, q_dim:q_dim + kv_dim], kv_heads),
                                w[:, q_dim + kv_dim:]], axis=-1)

    tables = _rope_tables(l)

    lc = ctx.shape[1]
    tm_x = _token_tile(l, 512)
    tm_ffn = _token_tile(l, 1024)
    tm_c = _token_tile(b * lc, 512)
    tf = 512

    flat = lambda a: a.reshape(1, b * lc, a.shape[-1])
    unflat = lambda a: a.reshape(b, lc, a.shape[-1])

    for i in range(depth):
        last = i == depth - 1
        j = i // n_mixers
        npre, npost = norm_pre_mix[i][None, :], norm_post_mix[i][None, :]
        if i % n_mixers == 0:
            ps, w_p = pool_scale[j][None, :], bf(w_pool[j])
            x = _pool_call(x, mod_x[i], npre, npost, w_p, ps, tm_x)
            if not last:
                ctx = _pool_call(ctx, mod_c[i], npre, npost, w_p, ps, _token_tile(lc, 512))
        else:
            w_in, w_out = qkv_weights(w_qkv[j]), bf(w_o[j])
            q, k, v = _qkv_call(x, mod_x[i], npre, w_in, q_dim, tables, tm_x)
            qc, kc, vct = _qkv_call(flat(ctx), mod_c[i][:1], npre, w_in, q_dim, None, tm_c)
            a = _attn_call(q, k, v, kc, vct, attn_sinks[j], lc)
            x = _oproj_call(a, x, mod_x[i], npost, w_out, tm_x)
            if not last:
                ac = _ctx_attn_call(qc, kc, vct, attn_sinks[j], lc)
                ctx = unflat(_oproj_call(ac, flat(ctx), mod_c[i][:1], npost, w_out, tm_c))

        npre, npost = norm_pre_ffn[i][None, :], norm_post_ffn[i][None, :]
        w_gu, w_dn = bf(w_gate_up[i]), bf(w_down[i])
        x = _ffn_call(x, mod_x[i], npre, npost, w_gu, w_dn, tm_ffn, tf)
        if not last:
            ctx = unflat(_ffn_call(flat(ctx), mod_c[i][:1], npre, npost, w_gu, w_dn, _token_tile(b * lc, 1024), tf))
    return x
```

```python
import functools

import jax
import jax.numpy as jnp
from jax import lax
from jax.experimental import pallas as pl
from jax.experimental.pallas import tpu as pltpu

GRID_W = 64
POOL_WINDOWS = (2, 4, 8, 16)
HEAD_DIM = 64
GQA_GROUP = 8
WINDOW = 128
Q_BLOCK = 128
ROPE_BASE = 10000.0
RMS_EPS = 1e-6
NEG_INF = -1e30
LOG2E = 1.4426950408889634

LANES = 128
SUBLANES = 8
BF16_ROWS = 16
VMEM_LIMIT_BYTES = 56 * 1024 * 1024
SWIGLU_VMEM_LIMIT_BYTES = 63 * 1024 * 1024

ROW_SLAB = BF16_ROWS
POOL_HALO = SUBLANES
HEAD_PAIR = 2 * HEAD_DIM

F32 = jnp.float32
BF16 = jnp.bfloat16


def _params(*semantics, vmem_limit_bytes=VMEM_LIMIT_BYTES):
    return pltpu.CompilerParams(dimension_semantics=semantics, vmem_limit_bytes=vmem_limit_bytes)


def _resident(shape):
    nd = len(shape)
    return pl.BlockSpec(shape, lambda *_: (0,) * nd, pipeline_mode=pl.Buffered(1))


def _rms(x, w):
    ms = jnp.mean(x * x, axis=-1, keepdims=True)
    return x * lax.rsqrt(ms + RMS_EPS) * w


def _norm_mod(x, w, shift, scale):
    return _rms(x, w * (1.0 + scale)) + shift


def _ada_kernel(c_ref, w_ref, b_ref, o_ref):
    s = jax.nn.silu(c_ref[...]).astype(BF16)
    o_ref[0] = jnp.dot(s, w_ref[0].astype(BF16), preferred_element_type=F32) + b_ref[0]


def _ada_call(c_rows, w_ada, b_ada):
    depth, d, n = w_ada.shape
    rows = c_rows.shape[0]
    tn = 1024
    return pl.pallas_call(
        _ada_kernel,
        grid=(depth, n // tn),
        in_specs=[
            pl.BlockSpec((rows, d), lambda i, j: (0, 0)),
            pl.BlockSpec((1, d, tn), lambda i, j: (i, 0, j)),
            pl.BlockSpec((1, 1, tn), lambda i, j: (i, 0, j)),
        ],
        out_specs=pl.BlockSpec((1, rows, tn), lambda i, j: (i, 0, j)),
        out_shape=jax.ShapeDtypeStruct((depth, rows, n), F32),
        compiler_params=_params("parallel", "parallel"),
        name="ada_proj",
    )(c_rows, w_ada, b_ada.reshape(depth, 1, n))


def _ffn_kernel(x_ref, mod_ref, npre_ref, npost_ref, wg_ref, wu_ref, wd_ref, o_ref, h_ref):
    k = pl.program_id(2)
    last = pl.num_programs(2) - 1
    tm = x_ref.shape[1]

    def prologue():
        w = npre_ref[...] * (1.0 + mod_ref[0, 4:5, :])
        shift = mod_ref[0, 3:4, :]
        for i in range(tm // ROW_SLAB):
            rows = pl.ds(i * ROW_SLAB, ROW_SLAB)
            h_ref[rows, :] = (_rms(x_ref[0, rows, :], w) + shift).astype(BF16)

    def chunk(first):
        h = h_ref[...]
        g = jnp.dot(h, wg_ref[...], preferred_element_type=F32)
        u = jnp.dot(h, wu_ref[...], preferred_element_type=F32)
        a = (jax.nn.silu(g) * u).astype(BF16)
        part = jnp.dot(a, wd_ref[...], preferred_element_type=F32)
        if first:
            o_ref[0] = part
        else:
            o_ref[0] += part

    def epilogue():
        w = mod_ref[0, 5:6, :] * npost_ref[...]
        for i in range(tm // ROW_SLAB):
            rows = pl.ds(i * ROW_SLAB, ROW_SLAB)
            o_ref[0, rows, :] = x_ref[0, rows, :] + _rms(o_ref[0, rows, :], w)

    @pl.when(k == 0)
    def _():
        prologue()
        chunk(True)

    @pl.when((k > 0) & (k < last))
    def _():
        chunk(False)

    @pl.when(k == last)
    def _():
        chunk(False)
        epilogue()


def _ffn_call(x, mod, npre, npost, w_gate_up, w_down, layer, tm, tf):
    b, l, d = x.shape
    f = w_down.shape[1]
    nk = f // tf
    return pl.pallas_call(
        _ffn_kernel,
        grid=(b, l // tm, nk),
        in_specs=[
            pl.BlockSpec((1, tm, d), lambda i, t, k: (i, t, 0)),
            pl.BlockSpec((1, 6, d), lambda i, t, k: (i, 0, 0)),
            pl.BlockSpec((1, d), lambda i, t, k: (0, 0)),
            pl.BlockSpec((1, d), lambda i, t, k: (0, 0)),
            pl.BlockSpec((None, d, tf), lambda i, t, k: (layer, 0, k)),
            pl.BlockSpec((None, d, tf), lambda i, t, k: (layer, 0, nk + k)),
            pl.BlockSpec((None, tf, d), lambda i, t, k: (layer, k, 0)),
        ],
        out_specs=pl.BlockSpec((1, tm, d), lambda i, t, k: (i, t, 0)),
        out_shape=jax.ShapeDtypeStruct((b, l, d), F32),
        scratch_shapes=[pltpu.VMEM((tm, d), BF16)],
        compiler_params=_params("parallel", "parallel", "arbitrary", vmem_limit_bytes=SWIGLU_VMEM_LIMIT_BYTES),
        name="swiglu",
    )(x, mod, npre, npost, w_gate_up, w_gate_up, w_down)


def _pool_kernel(x_ref, xp_ref, xn_ref, mod_ref, npre_ref, npost_ref, wp_ref, ps_ref, o_ref, h_ref, lv_ref, *,
                 seq_len):
    t = pl.program_id(1)
    tm = x_ref.shape[1]
    gd = wp_ref.shape[1]
    d = x_ref.shape[2]
    npre = npre_ref[...]
    shift, scale = mod_ref[0, 0:1, :], mod_ref[0, 1:2, :]
    tile0 = 2 * POOL_HALO
    span = tm + 2 * POOL_HALO

    hp = _norm_mod(xp_ref[0], npre, shift, scale)
    hn = _norm_mod(xn_ref[0], npre, shift, scale)
    zeros = jnp.zeros((POOL_HALO, gd), F32)
    n_groups = len(POOL_WINDOWS)

    def put(rows, value):
        for g in range(n_groups):
            h_ref[g, rows, :] = value[:, g * gd:(g + 1) * gd]

    put(pl.ds(POOL_HALO, POOL_HALO), jnp.where(t > 0, hp, 0.0))
    w_mod = npre * (1.0 + scale)
    for i in range(tm // ROW_SLAB):
        put(pl.ds(tile0 + i * ROW_SLAB, ROW_SLAB), _rms(x_ref[0, pl.ds(i * ROW_SLAB, ROW_SLAB), :], w_mod) + shift)
    put(pl.ds(tile0 + tm, POOL_HALO), jnp.where(t < pl.num_programs(1) - 1, hn, 0.0))
    for g in range(n_groups):
        h_ref[g, 0:POOL_HALO, :] = zeros
        h_ref[g, POOL_HALO + span:, :] = zeros
    for buf in range(2):
        lv_ref[buf, 0:POOL_HALO, :] = zeros
        lv_ref[buf, POOL_HALO + span:, :] = zeros

    pos = t * tm + lax.broadcasted_iota(jnp.int32, (tm, LANES), 0)
    ss = jnp.zeros((tm, 1), F32)
    for g, w in enumerate(POOL_WINDOWS):
        cols = pl.ds(g * gd, gd)
        cur = h_ref[g, pl.ds(POOL_HALO - 1, span), :] + h_ref[g, pl.ds(POOL_HALO, span), :]
        reach = 1
        while 2 * reach < w:
            buf = lv_ref.at[reach.bit_length() % 2]
            buf[pl.ds(POOL_HALO, span), :] = cur
            cur = buf[pl.ds(POOL_HALO - reach, span), :] + buf[pl.ds(POOL_HALO + reach, span), :]
            reach *= 2
        acc = cur[POOL_HALO:POOL_HALO + tm]
        cnt = jnp.minimum(pos + w // 2, seq_len) - jnp.maximum(pos - w // 2, 0)
        inv_cnt = jnp.concatenate([1.0 / cnt.astype(F32)] * (gd // LANES), axis=1)
        p = acc * inv_cnt - h_ref[g, pl.ds(tile0, tm), :]
        y = jnp.dot(p.astype(BF16), wp_ref[g], preferred_element_type=F32) * ps_ref[:, cols]
        o_ref[0, :, cols] = y
        ss = ss + jnp.sum(y * y, axis=-1, keepdims=True)

    rstd = lax.rsqrt(ss * (1.0 / d) + RMS_EPS)
    o_ref[0] = x_ref[0] + o_ref[0] * rstd * (mod_ref[0, 2:3, :] * npost_ref[...])


def _pool_call(x, mod, npre, npost, w_pool, pool_scale, tm):
    b, l, d = x.shape
    hb = tm // POOL_HALO
    last_hb = l // POOL_HALO - 1
    return pl.pallas_call(
        functools.partial(_pool_kernel, seq_len=l),
        grid=(b, l // tm),
        in_specs=[
            pl.BlockSpec((1, tm, d), lambda i, t: (i, t, 0)),
            pl.BlockSpec((1, POOL_HALO, d), lambda i, t: (i, jnp.maximum(t * hb - 1, 0), 0)),
            pl.BlockSpec((1, POOL_HALO, d), lambda i, t: (i, jnp.minimum((t + 1) * hb, last_hb), 0)),
            pl.BlockSpec((1, 6, d), lambda i, t: (i, 0, 0)),
            pl.BlockSpec((1, d), lambda i, t: (0, 0)),
            pl.BlockSpec((1, d), lambda i, t: (0, 0)),
            _resident(w_pool.shape),
            pl.BlockSpec((1, d), lambda i, t: (0, 0)),
        ],
        out_specs=pl.BlockSpec((1, tm, d), lambda i, t: (i, t, 0)),
        out_shape=jax.ShapeDtypeStruct((b, l, d), F32),
        scratch_shapes=[pltpu.VMEM((w_pool.shape[0], tm + 4 * POOL_HALO, w_pool.shape[1]), F32),
                        pltpu.VMEM((2, tm + 4 * POOL_HALO, w_pool.shape[1]), F32)],
        compiler_params=_params("parallel", "parallel"),
        name="pool_mixer",
    )(x, x, x, mod, npre, npost, w_pool, pool_scale)


def _rope(x, cos, sin_signed, low_half):
    out = []
    for j in range(x.shape[1] // LANES):
        c = x[:, j * LANES:(j + 1) * LANES]
        rot = jnp.where(low_half, pltpu.roll(c, LANES - HEAD_DIM // 4, 1), pltpu.roll(c, HEAD_DIM // 4, 1))
        out.append(c * cos + rot * sin_signed)
    return jnp.concatenate(out, axis=1)


def _qkv_kernel(*refs, rope, q_dim, k_dim):
    if rope:
        x_ref, mod_ref, npre_ref, w_ref, cos_ref, sin_ref, q_ref, k_ref, vt_ref = refs
    else:
        x_ref, mod_ref, npre_ref, w_ref, q_ref, k_ref, vt_ref = refs
    h = _norm_mod(x_ref[0], npre_ref[...], mod_ref[0, 0:1, :], mod_ref[0, 1:2, :]).astype(BF16)
    qkv = jnp.dot(h, w_ref[...], preferred_element_type=F32)
    q = qkv[:, :q_dim]
    k = qkv[:, q_dim:q_dim + k_dim]
    v = qkv[:, q_dim + k_dim:]
    if rope:
        cos, sin_signed = cos_ref[...], sin_ref[...]
        lane = lax.broadcasted_iota(jnp.int32, cos.shape, 1)
        low_half = (lane % (HEAD_DIM // 2)) < (HEAD_DIM // 4)
        q = _rope(q, cos, sin_signed, low_half)
        k = _rope(k, cos, sin_signed, low_half)
    q_ref[0] = (q * (HEAD_DIM ** -0.5 * LOG2E)).astype(BF16)
    k_ref[0] = k.astype(BF16)
    vt_ref[0] = v.T.astype(BF16)


def _qkv_call(x, mod, npre, w_qkv_dup, q_dim, rope_tables, tm):
    b, l, d = x.shape
    n = w_qkv_dup.shape[1]
    v_dim = (n - q_dim) // 3
    k_dim = 2 * v_dim
    rope = rope_tables is not None
    in_specs = [
        pl.BlockSpec((1, tm, d), lambda i, t: (i, t, 0)),
        pl.BlockSpec((1, 6, d), lambda i, t: (i, 0, 0)),
        pl.BlockSpec((1, d), lambda i, t: (0, 0)),
        _resident(w_qkv_dup.shape),
    ]
    args = [x, mod, npre, w_qkv_dup]
    if rope:
        in_specs += [pl.BlockSpec((tm, LANES), lambda i, t: (t, 0))] * 2
        args += list(rope_tables)
    return pl.pallas_call(
        functools.partial(_qkv_kernel, rope=rope, q_dim=q_dim, k_dim=k_dim),
        grid=(b, l // tm),
        in_specs=in_specs,
        out_specs=[
            pl.BlockSpec((1, tm, q_dim), lambda i, t: (i, t, 0)),
            pl.BlockSpec((1, tm, k_dim), lambda i, t: (i, t, 0)),
            pl.BlockSpec((1, v_dim, tm), lambda i, t: (i, 0, t)),
        ],
        out_shape=[
            jax.ShapeDtypeStruct((b, l, q_dim), BF16),
            jax.ShapeDtypeStruct((b, l, k_dim), BF16),
            jax.ShapeDtypeStruct((b, v_dim, l), BF16),
        ],
        compiler_params=_params("parallel", "parallel"),
        name="qkv_rope" if rope else "qkv_ctx",
    )(*args)


def _attend_scores(q_ref, sink_ref, h, k, row0, nq):
    low = lax.broadcasted_iota(jnp.int32, (nq, HEAD_PAIR), 1) < HEAD_DIM
    zero = jnp.zeros((), BF16)
    base = h * GQA_GROUP * HEAD_DIM
    q_rows, sink_cols = [], []
    for g in range(GQA_GROUP):
        pair = q_ref[0, pl.ds(row0, nq), pl.ds(base + (g // 2) * HEAD_PAIR, HEAD_PAIR)]
        q_rows.append(jnp.where(low if g % 2 == 0 else ~low, pair, zero))
        sink_cols.append(jnp.full((1, nq), sink_ref[h * GQA_GROUP + g] * LOG2E, F32))
    qg = jnp.concatenate(q_rows, axis=0)
    snk = jnp.concatenate(sink_cols, axis=1)
    s = lax.dot_general(k, qg, (((1,), (1,)), ((), ())), preferred_element_type=F32)
    return s, snk


def _attend_softmax(s, snk, masks):
    blocks, row = [], 0
    for first, bias in masks:
        if first > row:
            blocks.append(s[row:first])
        row = first + bias.shape[0]
        blocks.append(s[first:row] + bias)
    if masks:
        if row < s.shape[0]:
            blocks.append(s[row:])
        s = jnp.concatenate(blocks, axis=0)
    m = jnp.maximum(snk, jnp.max(s, axis=0, keepdims=True))
    return jnp.exp2(s - m).astype(BF16), jnp.exp2(snk - m)


def _attend_values(o_ref, h, p, sink_p, vt, row0, nq):
    base = h * GQA_GROUP * HEAD_DIM
    vt_ones = jnp.concatenate([vt, jnp.ones((BF16_ROWS, vt.shape[1]), BF16)], axis=0)
    acc = jnp.dot(vt_ones, p, preferred_element_type=F32)
    den = sink_p + acc[HEAD_DIM:HEAD_DIM + 1]
    acc = acc[:HEAD_DIM] * (1.0 / den)
    for j in range(GQA_GROUP // 2):
        pair_t = jnp.concatenate([acc[:, (2 * j) * nq:(2 * j + 1) * nq], acc[:, (2 * j + 1) * nq:(2 * j + 2) * nq]],
                                 axis=0)
        o_ref[0, pl.ds(row0, nq), pl.ds(base + j * HEAD_PAIR, HEAD_PAIR)] = pair_t.T.astype(o_ref.dtype)


def _attn_kernel(sink_ref, q_ref, kp_ref, kc_ref, kn_ref, vp_ref, vc_ref, vn_ref, kx_ref, vx_ref, o_ref):
    t = pl.program_id(1)
    nq = Q_BLOCK
    n_sub = q_ref.shape[1] // nq
    shape = (nq, GQA_GROUP * nq)
    key = lax.broadcasted_iota(jnp.int32, shape, 0)
    qry = lax.broadcasted_iota(jnp.int32, shape, 1) % nq
    inner_prev = jnp.where(key >= qry, 0.0, NEG_INF)
    inner_next = jnp.where(key <= qry, 0.0, NEG_INF)
    first_prev = jnp.where((key >= qry) & (t > 0), 0.0, NEG_INF)
    last_next = jnp.where((key <= qry) & (t < pl.num_programs(1) - 1), 0.0, NEG_INF)
    k_loc = jnp.concatenate([kp_ref[0], kc_ref[0], kn_ref[0]], axis=0)
    vt_loc = jnp.concatenate([vp_ref[0], vc_ref[0], vn_ref[0]], axis=1)
    n_kv = vt_loc.shape[0] // HEAD_DIM

    units = []
    for u in range(n_sub):
        k_all = jnp.concatenate([k_loc[u * nq:(u + 3) * nq], kx_ref[0]], axis=0)
        vt_all = jnp.concatenate([vt_loc[:, u * nq:(u + 3) * nq], vx_ref[0]], axis=1)
        masks = [(0, first_prev if u == 0 else inner_prev),
                 (2 * nq, last_next if u == n_sub - 1 else inner_next)]
        for h in range(n_kv):
            units.append((u * nq, h, k_all[:, h * HEAD_PAIR:(h + 1) * HEAD_PAIR],
                          vt_all[h * HEAD_DIM:(h + 1) * HEAD_DIM, :], masks))
    scores, probs = {}, {}
    for n in range(len(units) + 2):
        if n < len(units):
            row0, h, k, _, _ = units[n]
            scores[n] = _attend_scores(q_ref, sink_ref, h, k, row0, nq)
        if 0 <= n - 2 < len(units):
            row0, h, _, vt, _ = units[n - 2]
            _attend_values(o_ref, h, *probs.pop(n - 2), vt, row0, nq)
        if 0 <= n - 1 < len(units):
            probs[n - 1] = _attend_softmax(*scores.pop(n - 1), units[n - 1][4])


def _attn_call(q, k, vt, kx, vxt, sinks, n_ctx, n_sub):
    b, l, qd = q.shape
    kd, vd = k.shape[2], vt.shape[1]
    nb = l // Q_BLOCK
    rows = n_sub * Q_BLOCK
    prev = lambda t: jnp.maximum(t * n_sub - 1, 0)
    nxt = lambda t: jnp.minimum((t + 1) * n_sub, nb - 1)
    k_halo = lambda f: pl.BlockSpec((1, Q_BLOCK, kd), lambda i, t: (i, f(t), 0))
    v_halo = lambda f: pl.BlockSpec((1, vd, Q_BLOCK), lambda i, t: (i, 0, f(t)))
    return pl.pallas_call(
        _attn_kernel,
        grid=(b, nb // n_sub),
        in_specs=[
            pl.BlockSpec(memory_space=pltpu.SMEM),
            pl.BlockSpec((1, rows, qd), lambda i, t: (i, t, 0)),
            k_halo(prev), pl.BlockSpec((1, rows, kd), lambda i, t: (i, t, 0)), k_halo(nxt),
            v_halo(prev), pl.BlockSpec((1, vd, rows), lambda i, t: (i, 0, t)), v_halo(nxt),
            pl.BlockSpec((1, n_ctx, kd), lambda i, t: (0, i, 0)),
            pl.BlockSpec((1, vd, n_ctx), lambda i, t: (0, 0, i)),
        ],
        out_specs=pl.BlockSpec((1, rows, qd), lambda i, t: (i, t, 0)),
        out_shape=jax.ShapeDtypeStruct((b, l, qd), BF16),
        compiler_params=_params("parallel", "parallel"),
        name="window_attn",
    )(sinks, q, k, k, k, vt, vt, vt, kx, vxt)


def _ctx_attn_kernel(sink_ref, q_ref, k_ref, vt_ref, o_ref):
    nq = q_ref.shape[1]
    for h in range(vt_ref.shape[1] // HEAD_DIM):
        sc = _attend_scores(q_ref, sink_ref, h, k_ref[0, :, h * HEAD_PAIR:(h + 1) * HEAD_PAIR], 0, nq)
        _attend_values(o_ref, h, *_attend_softmax(*sc, []), vt_ref[0, h * HEAD_DIM:(h + 1) * HEAD_DIM, :], 0, nq)


def _ctx_attn_call(q, k, vt, sinks, n_ctx):
    qd, kd, vd = q.shape[2], k.shape[2], vt.shape[1]
    return pl.pallas_call(
        _ctx_attn_kernel,
        grid=(q.shape[1] // n_ctx,),
        in_specs=[
            pl.BlockSpec(memory_space=pltpu.SMEM),
            pl.BlockSpec((1, n_ctx, qd), lambda i: (0, i, 0)),
            pl.BlockSpec((1, n_ctx, kd), lambda i: (0, i, 0)),
            pl.BlockSpec((1, vd, n_ctx), lambda i: (0, 0, i)),
        ],
        out_specs=pl.BlockSpec((1, n_ctx, qd), lambda i: (0, i, 0)),
        out_shape=jax.ShapeDtypeStruct(q.shape, BF16),
        compiler_params=_params("parallel"),
        name="ctx_attn",
    )(sinks, q, k, vt)


def _oproj_kernel(a_ref, x_ref, mod_ref, npost_ref, w_ref, o_ref):
    y = jnp.dot(a_ref[0], w_ref[...], preferred_element_type=F32)
    o_ref[0] = x_ref[0] + _rms(y, mod_ref[0, 2:3, :] * npost_ref[...])


def _oproj_call(a, x, mod, npost, w_o, tm):
    b, l, d = x.shape
    ad = a.shape[2]
    return pl.pallas_call(
        _oproj_kernel,
        grid=(b, l // tm),
        in_specs=[
            pl.BlockSpec((1, tm, ad), lambda i, t: (i, t, 0)),
            pl.BlockSpec((1, tm, d), lambda i, t: (i, t, 0)),
            pl.BlockSpec((1, 6, d), lambda i, t: (i, 0, 0)),
            pl.BlockSpec((1, d), lambda i, t: (0, 0)),
            _resident(w_o.shape),
        ],
        out_specs=pl.BlockSpec((1, tm, d), lambda i, t: (i, t, 0)),
        out_shape=jax.ShapeDtypeStruct((b, l, d), F32),
        compiler_params=_params("parallel", "parallel"),
        name="attn_out_proj",
    )(a, x, mod, npost, w_o)


def _rope_tables(l):
    axis_dim = HEAD_DIM // 2
    rows_n = l // GRID_W
    row = jnp.repeat(jnp.arange(rows_n), GRID_W).astype(F32)
    col = jnp.tile(jnp.arange(GRID_W), rows_n).astype(F32)
    inv = 1.0 / (ROPE_BASE ** (jnp.arange(0, axis_dim, 2, dtype=F32) / axis_dim))
    ang_r = row[:, None] * inv[None, :]
    ang_c = col[:, None] * inv[None, :]
    ang = jnp.concatenate([ang_r, ang_r, ang_c, ang_c], axis=-1)
    sign = jnp.tile(jnp.concatenate([-jnp.ones(axis_dim // 2, F32), jnp.ones(axis_dim // 2, F32)]), 2)
    cos, sin = jnp.cos(ang), jnp.sin(ang) * sign[None, :]
    return jnp.tile(cos, (1, LANES // HEAD_DIM)), jnp.tile(sin, (1, LANES // HEAD_DIM))


def _dup_heads(w, n_heads):
    d = w.shape[0]
    w = w.reshape(d, n_heads, 1, HEAD_DIM)
    return jnp.broadcast_to(w, (d, n_heads, 2, HEAD_DIM)).reshape(d, n_heads * HEAD_PAIR)


def _token_tile(l, target):
    return min(l, target)


def kernel(x, c, ctx, c_ctx, w_ada, b_ada, norm_pre_mix, norm_post_mix, norm_pre_ffn, norm_post_ffn,
           w_pool, pool_scale, w_qkv, w_o, attn_sinks, w_gate_up, w_down):
    b, l, d = x.shape
    depth = w_ada.shape[0]
    n_mixers = 2
    q_dim = w_o.shape[1]
    kv_heads = (w_qkv.shape[2] - q_dim) // (2 * HEAD_DIM)
    kv_dim = kv_heads * HEAD_DIM

    rows = -(-(b + 1) // BF16_ROWS) * BF16_ROWS
    c_rows = jnp.concatenate([c, c_ctx[None, :], jnp.zeros((rows - b - 1, d), F32)], axis=0)
    ada = _ada_call(c_rows, w_ada, b_ada).reshape(depth, rows, 6, d)
    mod_x = ada[:, :b]
    mod_c = jnp.broadcast_to(ada[:, b:b + 1], (depth, b, 6, d))

    bf = lambda w: w.astype(BF16)
    w_gu, w_dn = bf(w_gate_up), bf(w_down)

    def qkv_weights(w):
        w = bf(w)
        return jnp.concatenate([w[:, :q_dim], _dup_heads(w[:, q_dim:q_dim + kv_dim], kv_heads),
                                w[:, q_dim + kv_dim:]], axis=-1)

    tables = _rope_tables(l)

    lc = ctx.shape[1]
    tm_x = _token_tile(l, 512)
    tm_ffn = _token_tile(l, 1024)
    tm_c = _token_tile(b * lc, 512)
    tf = 512

    flat = lambda a: a.reshape(1, b * lc, a.shape[-1])
    unflat = lambda a: a.reshape(b, lc, a.shape[-1])

    for i in range(depth):
        last = i == depth - 1
        j = i // n_mixers
        npre, npost = norm_pre_mix[i][None, :], norm_post_mix[i][None, :]
        if i % n_mixers == 0:
            ps, w_p = pool_scale[j][None, :], bf(w_pool[j])
            x = _pool_call(x, mod_x[i], npre, npost, w_p, ps, tm_x)
            if not last:
                ctx = _pool_call(ctx, mod_c[i], npre, npost, w_p, ps, _token_tile(lc, 512))
        else:
            w_in, w_out = qkv_weights(w_qkv[j]), bf(w_o[j])
            q, k, v = _qkv_call(x, mod_x[i], npre, w_in, q_dim, tables, tm_x)
            qc, kc, vct = _qkv_call(flat(ctx), mod_c[i][:1], npre, w_in, q_dim, None, tm_c)
            a = _attn_call(q, k, v, kc, vct, attn_sinks[j], lc, 2 if (l // Q_BLOCK) % 2 == 0 else 1)
            x = _oproj_call(a, x, mod_x[i], npost, w_out, tm_x)
            if not last:
                ac = _ctx_attn_call(qc, kc, vct, attn_sinks[j], lc)
                ctx = unflat(_oproj_call(ac, flat(ctx), mod_c[i][:1], npost, w_out, tm_c))

        npre, npost = norm_pre_ffn[i][None, :], norm_post_ffn[i][None, :]
        x = _ffn_call(x, mod_x[i], npre, npost, w_gu, w_dn, i, tm_ffn, tf)
        if not last:
            ctx = unflat(_ffn_call(flat(ctx), mod_c[i][:1], npre, npost, w_gu, w_dn, i, _token_tile(b * lc, 1024), tf))
    return x
```

```python
import functools

import jax
import jax.numpy as jnp
from jax import lax
from jax.experimental import pallas as pl
from jax.experimental.pallas import tpu as pltpu

GRID_W = 64
POOL_WINDOWS = (2, 4, 8, 16)
HEAD_DIM = 64
GQA_GROUP = 8
WINDOW = 128
Q_BLOCK = 128
ROPE_BASE = 10000.0
RMS_EPS = 1e-6
NEG_INF = -1e30
LOG2E = 1.4426950408889634

LANES = 128
SUBLANES = 8
BF16_ROWS = 16
VMEM_LIMIT_BYTES = 56 * 1024 * 1024
SWIGLU_VMEM_LIMIT_BYTES = 63 * 1024 * 1024

ROW_SLAB = BF16_ROWS
POOL_HALO = SUBLANES
HEAD_PAIR = 2 * HEAD_DIM

F32 = jnp.float32
BF16 = jnp.bfloat16


def _params(*semantics, vmem_limit_bytes=VMEM_LIMIT_BYTES):
    return pltpu.CompilerParams(dimension_semantics=semantics, vmem_limit_bytes=vmem_limit_bytes)


def _resident(shape):
    nd = len(shape)
    return pl.BlockSpec(shape, lambda *_: (0,) * nd, pipeline_mode=pl.Buffered(1))


def _rms(x, w):
    ms = jnp.mean(x * x, axis=-1, keepdims=True)
    return x * lax.rsqrt(ms + RMS_EPS) * w


def _norm_mod(x, w, shift, scale):
    return _rms(x, w * (1.0 + scale)) + shift


def _ada_kernel(c_ref, w_ref, b_ref, o_ref):
    s = jax.nn.silu(c_ref[...]).astype(BF16)
    o_ref[0] = jnp.dot(s, w_ref[0].astype(BF16), preferred_element_type=F32) + b_ref[0]


def _ada_call(c_rows, w_ada, b_ada):
    depth, d, n = w_ada.shape
    rows = c_rows.shape[0]
    tn = 1024
    return pl.pallas_call(
        _ada_kernel,
        grid=(depth, n // tn),
        in_specs=[
            pl.BlockSpec((rows, d), lambda i, j: (0, 0)),
            pl.BlockSpec((1, d, tn), lambda i, j: (i, 0, j)),
            pl.BlockSpec((1, 1, tn), lambda i, j: (i, 0, j)),
        ],
        out_specs=pl.BlockSpec((1, rows, tn), lambda i, j: (i, 0, j)),
        out_shape=jax.ShapeDtypeStruct((depth, rows, n), F32),
        compiler_params=_params("parallel", "parallel"),
        name="ada_proj",
    )(c_rows, w_ada, b_ada.reshape(depth, 1, n))


def _ffn_kernel(x_ref, mod_ref, npre_ref, npost_ref, wg_ref, wu_ref, wd_ref, o_ref, h_ref):
    k = pl.program_id(2)
    last = pl.num_programs(2) - 1
    tm = x_ref.shape[1]

    def prologue():
        w = npre_ref[...] * (1.0 + mod_ref[0, 4:5, :])
        shift = mod_ref[0, 3:4, :]
        for i in range(tm // ROW_SLAB):
            rows = pl.ds(i * ROW_SLAB, ROW_SLAB)
            h_ref[rows, :] = (_rms(x_ref[0, rows, :], w) + shift).astype(BF16)

    def chunk(first):
        h = h_ref[...]
        g = jnp.dot(h, wg_ref[...], preferred_element_type=F32)
        u = jnp.dot(h, wu_ref[...], preferred_element_type=F32)
        a = (jax.nn.silu(g) * u).astype(BF16)
        part = jnp.dot(a, wd_ref[...], preferred_element_type=F32)
        if first:
            o_ref[0] = part
        else:
            o_ref[0] += part

    def epilogue():
        w = mod_ref[0, 5:6, :] * npost_ref[...]
        for i in range(tm // ROW_SLAB):
            rows = pl.ds(i * ROW_SLAB, ROW_SLAB)
            o_ref[0, rows, :] = x_ref[0, rows, :] + _rms(o_ref[0, rows, :], w)

    @pl.when(k == 0)
    def _():
        prologue()
        chunk(True)

    @pl.when((k > 0) & (k < last))
    def _():
        chunk(False)

    @pl.when(k == last)
    def _():
        chunk(False)
        epilogue()


def _ffn_call(x, mod, npre, npost, w_gate_up, w_down, layer, tm, tf):
    b, l, d = x.shape
    f = w_down.shape[1]
    nk = f // tf
    return pl.pallas_call(
        _ffn_kernel,
        grid=(b, l // tm, nk),
        in_specs=[
            pl.BlockSpec((1, tm, d), lambda i, t, k: (i, t, 0)),
            pl.BlockSpec((1, 6, d), lambda i, t, k: (i, 0, 0)),
            pl.BlockSpec((1, d), lambda i, t, k: (0, 0)),
            pl.BlockSpec((1, d), lambda i, t, k: (0, 0)),
            pl.BlockSpec((None, d, tf), lambda i, t, k: (layer, 0, k)),
            pl.BlockSpec((None, d, tf), lambda i, t, k: (layer, 0, nk + k)),
            pl.BlockSpec((None, tf, d), lambda i, t, k: (layer, k, 0)),
        ],
        out_specs=pl.BlockSpec((1, tm, d), lambda i, t, k: (i, t, 0)),
        out_shape=jax.ShapeDtypeStruct((b, l, d), F32),
        scratch_shapes=[pltpu.VMEM((tm, d), BF16)],
        compiler_params=_params("parallel", "parallel", "arbitrary", vmem_limit_bytes=SWIGLU_VMEM_LIMIT_BYTES),
        name="swiglu",
    )(x, mod, npre, npost, w_gate_up, w_gate_up, w_down)


def _pool_kernel(x_ref, xp_ref, xn_ref, mod_ref, npre_ref, npost_ref, wp_ref, ps_ref, o_ref, h_ref, *, seq_len):
    t = pl.program_id(1)
    tm = x_ref.shape[1]
    gd = wp_ref.shape[1]
    d = x_ref.shape[2]
    npre = npre_ref[...]
    shift, scale = mod_ref[0, 0:1, :], mod_ref[0, 1:2, :]
    span = tm + 2 * POOL_HALO
    n_groups = len(POOL_WINDOWS)

    def put(rows, value):
        for g in range(n_groups):
            h_ref[g, rows, :] = value[:, g * gd:(g + 1) * gd]

    hp = _norm_mod(xp_ref[0], npre, shift, scale)
    hn = _norm_mod(xn_ref[0], npre, shift, scale)
    put(pl.ds(0, POOL_HALO), jnp.where(t > 0, hp, 0.0))
    w_mod = npre * (1.0 + scale)
    for i in range(tm // ROW_SLAB):
        put(pl.ds(POOL_HALO + i * ROW_SLAB, ROW_SLAB),
            _rms(x_ref[0, pl.ds(i * ROW_SLAB, ROW_SLAB), :], w_mod) + shift)
    put(pl.ds(POOL_HALO + tm, POOL_HALO), jnp.where(t < pl.num_programs(1) - 1, hn, 0.0))

    pos = t * tm + lax.broadcasted_iota(jnp.int32, (tm, LANES), 0)
    ss = jnp.zeros((tm, 1), F32)
    for g, w in enumerate(POOL_WINDOWS):
        cols = pl.ds(g * gd, gd)
        cur = h_ref[g]
        cur = cur + pltpu.roll(cur, 1, 0)
        reach = 1
        while 2 * reach < w:
            cur = pltpu.roll(cur, reach, 0) + pltpu.roll(cur, span - reach, 0)
            reach *= 2
        acc = cur[POOL_HALO:POOL_HALO + tm]
        cnt = jnp.minimum(pos + w // 2, seq_len) - jnp.maximum(pos - w // 2, 0)
        inv_cnt = jnp.concatenate([1.0 / cnt.astype(F32)] * (gd // LANES), axis=1)
        p = acc * inv_cnt - h_ref[g, pl.ds(POOL_HALO, tm), :]
        y = jnp.dot(p.astype(BF16), wp_ref[g], preferred_element_type=F32) * ps_ref[:, cols]
        o_ref[0, :, cols] = y
        ss = ss + jnp.sum(y * y, axis=-1, keepdims=True)

    rstd = lax.rsqrt(ss * (1.0 / d) + RMS_EPS)
    o_ref[0] = x_ref[0] + o_ref[0] * rstd * (mod_ref[0, 2:3, :] * npost_ref[...])


def _pool_call(x, mod, npre, npost, w_pool, pool_scale, tm):
    b, l, d = x.shape
    hb = tm // POOL_HALO
    last_hb = l // POOL_HALO - 1
    return pl.pallas_call(
        functools.partial(_pool_kernel, seq_len=l),
        grid=(b, l // tm),
        in_specs=[
            pl.BlockSpec((1, tm, d), lambda i, t: (i, t, 0)),
            pl.BlockSpec((1, POOL_HALO, d), lambda i, t: (i, jnp.maximum(t * hb - 1, 0), 0)),
            pl.BlockSpec((1, POOL_HALO, d), lambda i, t: (i, jnp.minimum((t + 1) * hb, last_hb), 0)),
            pl.BlockSpec((1, 6, d), lambda i, t: (i, 0, 0)),
            pl.BlockSpec((1, d), lambda i, t: (0, 0)),
            pl.BlockSpec((1, d), lambda i, t: (0, 0)),
            _resident(w_pool.shape),
            pl.BlockSpec((1, d), lambda i, t: (0, 0)),
        ],
        out_specs=pl.BlockSpec((1, tm, d), lambda i, t: (i, t, 0)),
        out_shape=jax.ShapeDtypeStruct((b, l, d), F32),
        scratch_shapes=[pltpu.VMEM((w_pool.shape[0], tm + 2 * POOL_HALO, w_pool.shape[1]), F32)],
        compiler_params=_params("parallel", "parallel"),
        name="pool_mixer",
    )(x, x, x, mod, npre, npost, w_pool, pool_scale)


def _rope(x, cos, sin_signed, low_half):
    out = []
    for j in range(x.shape[1] // LANES):
        c = x[:, j * LANES:(j + 1) * LANES]
        rot = jnp.where(low_half, pltpu.roll(c, LANES - HEAD_DIM // 4, 1), pltpu.roll(c, HEAD_DIM // 4, 1))
        out.append(c * cos + rot * sin_signed)
    return jnp.concatenate(out, axis=1)


def _qkv_kernel(*refs, rope, q_dim, k_dim):
    if rope:
        x_ref, mod_ref, npre_ref, w_ref, cos_ref, sin_ref, q_ref, k_ref, vt_ref = refs
    else:
        x_ref, mod_ref, npre_ref, w_ref, q_ref, k_ref, vt_ref = refs
    h = _norm_mod(x_ref[0], npre_ref[...], mod_ref[0, 0:1, :], mod_ref[0, 1:2, :]).astype(BF16)
    qkv = jnp.dot(h, w_ref[...], preferred_element_type=F32)
    q = qkv[:, :q_dim]
    k = qkv[:, q_dim:q_dim + k_dim]
    v = qkv[:, q_dim + k_dim:]
    if rope:
        cos, sin_signed = cos_ref[...], sin_ref[...]
        lane = lax.broadcasted_iota(jnp.int32, cos.shape, 1)
        low_half = (lane % (HEAD_DIM // 2)) < (HEAD_DIM // 4)
        q = _rope(q, cos, sin_signed, low_half)
        k = _rope(k, cos, sin_signed, low_half)
    q_ref[0] = (q * (HEAD_DIM ** -0.5 * LOG2E)).astype(BF16)
    k_ref[0] = k.astype(BF16)
    vt_ref[0] = v.T.astype(BF16)


def _qkv_call(x, mod, npre, w_qkv_dup, q_dim, rope_tables, tm):
    b, l, d = x.shape
    n = w_qkv_dup.shape[1]
    v_dim = (n - q_dim) // 3
    k_dim = 2 * v_dim
    rope = rope_tables is not None
    in_specs = [
        pl.BlockSpec((1, tm, d), lambda i, t: (i, t, 0)),
        pl.BlockSpec((1, 6, d), lambda i, t: (i, 0, 0)),
        pl.BlockSpec((1, d), lambda i, t: (0, 0)),
        _resident(w_qkv_dup.shape),
    ]
    args = [x, mod, npre, w_qkv_dup]
    if rope:
        in_specs += [pl.BlockSpec((tm, LANES), lambda i, t: (t, 0))] * 2
        args += list(rope_tables)
    return pl.pallas_call(
        functools.partial(_qkv_kernel, rope=rope, q_dim=q_dim, k_dim=k_dim),
        grid=(b, l // tm),
        in_specs=in_specs,
        out_specs=[
            pl.BlockSpec((1, tm, q_dim), lambda i, t: (i, t, 0)),
            pl.BlockSpec((1, tm, k_dim), lambda i, t: (i, t, 0)),
            pl.BlockSpec((1, v_dim, tm), lambda i, t: (i, 0, t)),
        ],
        out_shape=[
            jax.ShapeDtypeStruct((b, l, q_dim), BF16),
            jax.ShapeDtypeStruct((b, l, k_dim), BF16),
            jax.ShapeDtypeStruct((b, v_dim, l), BF16),
        ],
        compiler_params=_params("parallel", "parallel"),
        name="qkv_rope" if rope else "qkv_ctx",
    )(*args)


def _attend_scores(q_ref, sink_ref, h, k, row0, nq):
    low = lax.broadcasted_iota(jnp.int32, (nq, HEAD_PAIR), 1) < HEAD_DIM
    zero = jnp.zeros((), BF16)
    base = h * GQA_GROUP * HEAD_DIM
    q_rows, sink_cols = [], []
    for g in range(GQA_GROUP):
        pair = q_ref[0, pl.ds(row0, nq), pl.ds(base + (g // 2) * HEAD_PAIR, HEAD_PAIR)]
        q_rows.append(jnp.where(low if g % 2 == 0 else ~low, pair, zero))
        sink_cols.append(jnp.full((1, nq), sink_ref[h * GQA_GROUP + g] * LOG2E, F32))
    qg = jnp.concatenate(q_rows, axis=0)
    snk = jnp.concatenate(sink_cols, axis=1)
    s = lax.dot_general(k, qg, (((1,), (1,)), ((), ())), preferred_element_type=F32)
    return s, snk


def _attend_softmax(s, snk, masks):
    blocks, row = [], 0
    for first, bias in masks:
        if first > row:
            blocks.append(s[row:first])
        row = first + bias.shape[0]
        blocks.append(s[first:row] + bias)
    if masks:
        if row < s.shape[0]:
            blocks.append(s[row:])
        s = jnp.concatenate(blocks, axis=0)
    m = jnp.maximum(snk, jnp.max(s, axis=0, keepdims=True))
    return jnp.exp2(s - m).astype(BF16), jnp.exp2(snk - m)


def _attend_values(o_ref, h, p, sink_p, vt, row0, nq):
    base = h * GQA_GROUP * HEAD_DIM
    vt_ones = jnp.concatenate([vt, jnp.ones((BF16_ROWS, vt.shape[1]), BF16)], axis=0)
    acc = jnp.dot(vt_ones, p, preferred_element_type=F32)
    den = sink_p + acc[HEAD_DIM:HEAD_DIM + 1]
    acc = acc[:HEAD_DIM] * (1.0 / den)
    for j in range(GQA_GROUP // 2):
        pair_t = jnp.concatenate([acc[:, (2 * j) * nq:(2 * j + 1) * nq], acc[:, (2 * j + 1) * nq:(2 * j + 2) * nq]],
                                 axis=0)
        o_ref[0, pl.ds(row0, nq), pl.ds(base + j * HEAD_PAIR, HEAD_PAIR)] = pair_t.T.astype(o_ref.dtype)


def _attn_kernel(sink_ref, q_ref, kp_ref, kc_ref, kn_ref, vp_ref, vc_ref, vn_ref, kx_ref, vx_ref, o_ref):
    t = pl.program_id(1)
    nq = Q_BLOCK
    n_sub = q_ref.shape[1] // nq
    shape = (nq, GQA_GROUP * nq)
    key = lax.broadcasted_iota(jnp.int32, shape, 0)
    qry = lax.broadcasted_iota(jnp.int32, shape, 1) % nq
    inner_prev = jnp.where(key >= qry, 0.0, NEG_INF)
    inner_next = jnp.where(key <= qry, 0.0, NEG_INF)
    first_prev = jnp.where((key >= qry) & (t > 0), 0.0, NEG_INF)
    last_next = jnp.where((key <= qry) & (t < pl.num_programs(1) - 1), 0.0, NEG_INF)
    k_loc = jnp.concatenate([kp_ref[0], kc_ref[0], kn_ref[0]], axis=0)
    vt_loc = jnp.concatenate([vp_ref[0], vc_ref[0], vn_ref[0]], axis=1)
    n_kv = vt_loc.shape[0] // HEAD_DIM

    units = []
    for u in range(n_sub):
        k_all = jnp.concatenate([k_loc[u * nq:(u + 3) * nq], kx_ref[0]], axis=0)
        vt_all = jnp.concatenate([vt_loc[:, u * nq:(u + 3) * nq], vx_ref[0]], axis=1)
        masks = [(0, first_prev if u == 0 else inner_prev),
                 (2 * nq, last_next if u == n_sub - 1 else inner_next)]
        for h in range(n_kv):
            units.append((u * nq, h, k_all[:, h * HEAD_PAIR:(h + 1) * HEAD_PAIR],
                          vt_all[h * HEAD_DIM:(h + 1) * HEAD_DIM, :], masks))
    scores, probs = {}, {}
    for n in range(len(units) + 2):
        if n < len(units):
            row0, h, k, _, _ = units[n]
            scores[n] = _attend_scores(q_ref, sink_ref, h, k, row0, nq)
        if 0 <= n - 2 < len(units):
            row0, h, _, vt, _ = units[n - 2]
            _attend_values(o_ref, h, *probs.pop(n - 2), vt, row0, nq)
        if 0 <= n - 1 < len(units):
            probs[n - 1] = _attend_softmax(*scores.pop(n - 1), units[n - 1][4])


def _attn_call(q, k, vt, kx, vxt, sinks, n_ctx, n_sub):
    b, l, qd = q.shape
    kd, vd = k.shape[2], vt.shape[1]
    nb = l // Q_BLOCK
    rows = n_sub * Q_BLOCK
    prev = lambda t: jnp.maximum(t * n_sub - 1, 0)
    nxt = lambda t: jnp.minimum((t + 1) * n_sub, nb - 1)
    k_halo = lambda f: pl.BlockSpec((1, Q_BLOCK, kd), lambda i, t: (i, f(t), 0))
    v_halo = lambda f: pl.BlockSpec((1, vd, Q_BLOCK), lambda i, t: (i, 0, f(t)))
    return pl.pallas_call(
        _attn_kernel,
        grid=(b, nb // n_sub),
        in_specs=[
            pl.BlockSpec(memory_space=pltpu.SMEM),
            pl.BlockSpec((1, rows, qd), lambda i, t: (i, t, 0)),
            k_halo(prev), pl.BlockSpec((1, rows, kd), lambda i, t: (i, t, 0)), k_halo(nxt),
            v_halo(prev), pl.BlockSpec((1, vd, rows), lambda i, t: (i, 0, t)), v_halo(nxt),
            pl.BlockSpec((1, n_ctx, kd), lambda i, t: (0, i, 0)),
            pl.BlockSpec((1, vd, n_ctx), lambda i, t: (0, 0, i)),
        ],
        out_specs=pl.BlockSpec((1, rows, qd), lambda i, t: (i, t, 0)),
        out_shape=jax.ShapeDtypeStruct((b, l, qd), BF16),
        compiler_params=_params("parallel", "parallel"),
        name="window_attn",
    )(sinks, q, k, k, k, vt, vt, vt, kx, vxt)


def _ctx_attn_kernel(sink_ref, q_ref, k_ref, vt_ref, o_ref):
    nq = q_ref.shape[1]
    for h in range(vt_ref.shape[1] // HEAD_DIM):
        sc = _attend_scores(q_ref, sink_ref, h, k_ref[0, :, h * HEAD_PAIR:(h + 1) * HEAD_PAIR], 0, nq)
        _attend_values(o_ref, h, *_attend_softmax(*sc, []), vt_ref[0, h * HEAD_DIM:(h + 1) * HEAD_DIM, :], 0, nq)


def _ctx_attn_call(q, k, vt, sinks, n_ctx):
    qd, kd, vd = q.shape[2], k.shape[2], vt.shape[1]
    return pl.pallas_call(
        _ctx_attn_kernel,
        grid=(q.shape[1] // n_ctx,),
        in_specs=[
            pl.BlockSpec(memory_space=pltpu.SMEM),
            pl.BlockSpec((1, n_ctx, qd), lambda i: (0, i, 0)),
            pl.BlockSpec((1, n_ctx, kd), lambda i: (0, i, 0)),
            pl.BlockSpec((1, vd, n_ctx), lambda i: (0, 0, i)),
        ],
        out_specs=pl.BlockSpec((1, n_ctx, qd), lambda i: (0, i, 0)),
        out_shape=jax.ShapeDtypeStruct(q.shape, BF16),
        compiler_params=_params("parallel"),
        name="ctx_attn",
    )(sinks, q, k, vt)


def _oproj_kernel(a_ref, x_ref, mod_ref, npost_ref, w_ref, o_ref):
    y = jnp.dot(a_ref[0], w_ref[...], preferred_element_type=F32)
    o_ref[0] = x_ref[0] + _rms(y, mod_ref[0, 2:3, :] * npost_ref[...])


def _oproj_call(a, x, mod, npost, w_o, tm):
    b, l, d = x.shape
    ad = a.shape[2]
    return pl.pallas_call(
        _oproj_kernel,
        grid=(b, l // tm),
        in_specs=[
            pl.BlockSpec((1, tm, ad), lambda i, t: (i, t, 0)),
            pl.BlockSpec((1, tm, d), lambda i, t: (i, t, 0)),
            pl.BlockSpec((1, 6, d), lambda i, t: (i, 0, 0)),
            pl.BlockSpec((1, d), lambda i, t: (0, 0)),
            _resident(w_o.shape),
        ],
        out_specs=pl.BlockSpec((1, tm, d), lambda i, t: (i, t, 0)),
        out_shape=jax.ShapeDtypeStruct((b, l, d), F32),
        compiler_params=_params("parallel", "parallel"),
        name="attn_out_proj",
    )(a, x, mod, npost, w_o)


def _rope_tables(l):
    axis_dim = HEAD_DIM // 2
    rows_n = l // GRID_W
    row = jnp.repeat(jnp.arange(rows_n), GRID_W).astype(F32)
    col = jnp.tile(jnp.arange(GRID_W), rows_n).astype(F32)
    inv = 1.0 / (ROPE_BASE ** (jnp.arange(0, axis_dim, 2, dtype=F32) / axis_dim))
    ang_r = row[:, None] * inv[None, :]
    ang_c = col[:, None] * inv[None, :]
    ang = jnp.concatenate([ang_r, ang_r, ang_c, ang_c], axis=-1)
    sign = jnp.tile(jnp.concatenate([-jnp.ones(axis_dim // 2, F32), jnp.ones(axis_dim // 2, F32)]), 2)
    cos, sin = jnp.cos(ang), jnp.sin(ang) * sign[None, :]
    return jnp.tile(cos, (1, LANES // HEAD_DIM)), jnp.tile(sin, (1, LANES // HEAD_DIM))


def _dup_heads(w, n_heads):
    d = w.shape[0]
    w = w.reshape(d, n_heads, 1, HEAD_DIM)
    return jnp.broadcast_to(w, (d, n_heads, 2, HEAD_DIM)).reshape(d, n_heads * HEAD_PAIR)


def _token_tile(l, target):
    return min(l, target)


def kernel(x, c, ctx, c_ctx, w_ada, b_ada, norm_pre_mix, norm_post_mix, norm_pre_ffn, norm_post_ffn,
           w_pool, pool_scale, w_qkv, w_o, attn_sinks, w_gate_up, w_down):
    b, l, d = x.shape
    depth = w_ada.shape[0]
    n_mixers = 2
    q_dim = w_o.shape[1]
    kv_heads = (w_qkv.shape[2] - q_dim) // (2 * HEAD_DIM)
    kv_dim = kv_heads * HEAD_DIM

    rows = -(-(b + 1) // BF16_ROWS) * BF16_ROWS
    c_rows = jnp.concatenate([c, c_ctx[None, :], jnp.zeros((rows - b - 1, d), F32)], axis=0)
    ada = _ada_call(c_rows, w_ada, b_ada).reshape(depth, rows, 6, d)
    mod_x = ada[:, :b]
    mod_c = jnp.broadcast_to(ada[:, b:b + 1], (depth, b, 6, d))

    bf = lambda w: w.astype(BF16)
    w_gu, w_dn = bf(w_gate_up), bf(w_down)

    def qkv_weights(w):
        w = bf(w)
        return jnp.concatenate([w[:, :q_dim], _dup_heads(w[:, q_dim:q_dim + kv_dim], kv_heads),
                                w[:, q_dim + kv_dim:]], axis=-1)

    tables = _rope_tables(l)

    lc = ctx.shape[1]
    tm_x = _token_tile(l, 512)
    tm_ffn = _token_tile(l, 1024)
    tm_c = _token_tile(b * lc, 512)
    tf = 512

    flat = lambda a: a.reshape(1, b * lc, a.shape[-1])
    unflat = lambda a: a.reshape(b, lc, a.shape[-1])

    for i in range(depth):
        last = i == depth - 1
        j = i // n_mixers
        npre, npost = norm_pre_mix[i][None, :], norm_post_mix[i][None, :]
        if i % n_mixers == 0:
            ps, w_p = pool_scale[j][None, :], bf(w_pool[j])
            x = _pool_call(x, mod_x[i], npre, npost, w_p, ps, tm_x)
            if not last:
                ctx = _pool_call(ctx, mod_c[i], npre, npost, w_p, ps, _token_tile(lc, 512))
        else:
            w_in, w_out = qkv_weights(w_qkv[j]), bf(w_o[j])
            q, k, v = _qkv_call(x, mod_x[i], npre, w_in, q_dim, tables, tm_x)
            qc, kc, vct = _qkv_call(flat(ctx), mod_c[i][:1], npre, w_in, q_dim, None, tm_c)
            a = _attn_call(q, k, v, kc, vct, attn_sinks[j], lc, 2 if (l // Q_BLOCK) % 2 == 0 else 1)
            x = _oproj_call(a, x, mod_x[i], npost, w_out, tm_x)
            if not last:
                ac = _ctx_attn_call(qc, kc, vct, attn_sinks[j], lc)
                ctx = unflat(_oproj_call(ac, flat(ctx), mod_c[i][:1], npost, w_out, tm_c))

        npre, npost = norm_pre_ffn[i][None, :], norm_post_ffn[i][None, :]
        x = _ffn_call(x, mod_x[i], npre, npost, w_gu, w_dn, i, tm_ffn, tf)
        if not last:
            ctx = unflat(_ffn_call(flat(ctx), mod_c[i][:1], npre, npost, w_gu, w_dn, i, _token_tile(b * lc, 1024), tf))
    return x
```

```python
import functools

import jax
import jax.numpy as jnp
from jax import lax
from jax.experimental import pallas as pl
from jax.experimental.pallas import tpu as pltpu

GRID_W = 64
POOL_WINDOWS = (2, 4, 8, 16)
HEAD_DIM = 64
GQA_GROUP = 8
WINDOW = 128
Q_BLOCK = 128
ROPE_BASE = 10000.0
RMS_EPS = 1e-6
NEG_INF = -1e30
LOG2E = 1.4426950408889634

LANES = 128
SUBLANES = 8
BF16_ROWS = 16
VMEM_LIMIT_BYTES = 56 * 1024 * 1024
SWIGLU_VMEM_LIMIT_BYTES = 63 * 1024 * 1024

ROW_SLAB = BF16_ROWS
POOL_HALO = SUBLANES
HEAD_PAIR = 2 * HEAD_DIM

F32 = jnp.float32
BF16 = jnp.bfloat16


def _params(*semantics, vmem_limit_bytes=VMEM_LIMIT_BYTES):
    return pltpu.CompilerParams(dimension_semantics=semantics, vmem_limit_bytes=vmem_limit_bytes)


def _resident(shape):
    nd = len(shape)
    return pl.BlockSpec(shape, lambda *_: (0,) * nd, pipeline_mode=pl.Buffered(1))


def _rms(x, w):
    ms = jnp.mean(x * x, axis=-1, keepdims=True)
    return x * lax.rsqrt(ms + RMS_EPS) * w


def _norm_mod(x, w, shift, scale):
    return _rms(x, w * (1.0 + scale)) + shift


def _ada_kernel(c_ref, w_ref, b_ref, o_ref):
    s = jax.nn.silu(c_ref[...]).astype(BF16)
    o_ref[0] = jnp.dot(s, w_ref[0].astype(BF16), preferred_element_type=F32) + b_ref[0]


def _ada_call(c_rows, w_ada, b_ada):
    depth, d, n = w_ada.shape
    rows = c_rows.shape[0]
    tn = 1024
    return pl.pallas_call(
        _ada_kernel,
        grid=(depth, n // tn),
        in_specs=[
            pl.BlockSpec((rows, d), lambda i, j: (0, 0)),
            pl.BlockSpec((1, d, tn), lambda i, j: (i, 0, j)),
            pl.BlockSpec((1, 1, tn), lambda i, j: (i, 0, j)),
        ],
        out_specs=pl.BlockSpec((1, rows, tn), lambda i, j: (i, 0, j)),
        out_shape=jax.ShapeDtypeStruct((depth, rows, n), F32),
        compiler_params=_params("parallel", "parallel"),
        name="ada_proj",
    )(c_rows, w_ada, b_ada.reshape(depth, 1, n))


def _ffn_kernel(x_ref, mod_ref, npre_ref, npost_ref, wg_ref, wu_ref, wd_ref, o_ref, h_ref):
    k = pl.program_id(2)
    last = pl.num_programs(2) - 1
    tm = x_ref.shape[1]

    def prologue():
        w = npre_ref[...] * (1.0 + mod_ref[0, 4:5, :])
        shift = mod_ref[0, 3:4, :]
        for i in range(tm // ROW_SLAB):
            rows = pl.ds(i * ROW_SLAB, ROW_SLAB)
            h_ref[rows, :] = (_rms(x_ref[0, rows, :], w) + shift).astype(BF16)

    def chunk(first):
        h = h_ref[...]
        g = jnp.dot(h, wg_ref[...], preferred_element_type=F32)
        u = jnp.dot(h, wu_ref[...], preferred_element_type=F32)
        a = (jax.nn.silu(g) * u).astype(BF16)
        part = jnp.dot(a, wd_ref[...], preferred_element_type=F32)
        if first:
            o_ref[0] = part
        else:
            o_ref[0] += part

    def epilogue():
        w = mod_ref[0, 5:6, :] * npost_ref[...]
        for i in range(tm // ROW_SLAB):
            rows = pl.ds(i * ROW_SLAB, ROW_SLAB)
            o_ref[0, rows, :] = x_ref[0, rows, :] + _rms(o_ref[0, rows, :], w)

    @pl.when(k == 0)
    def _():
        prologue()
        chunk(True)

    @pl.when((k > 0) & (k < last))
    def _():
        chunk(False)

    @pl.when(k == last)
    def _():
        chunk(False)
        epilogue()


def _ffn_call(x, mod, npre, npost, w_gate_up, w_down, layer, tm, tf):
    b, l, d = x.shape
    f = w_down.shape[1]
    nk = f // tf
    return pl.pallas_call(
        _ffn_kernel,
        grid=(b, l // tm, nk),
        in_specs=[
            pl.BlockSpec((1, tm, d), lambda i, t, k: (i, t, 0)),
            pl.BlockSpec((1, 6, d), lambda i, t, k: (i, 0, 0)),
            pl.BlockSpec((1, d), lambda i, t, k: (0, 0)),
            pl.BlockSpec((1, d), lambda i, t, k: (0, 0)),
            pl.BlockSpec((None, d, tf), lambda i, t, k: (layer, 0, k)),
            pl.BlockSpec((None, d, tf), lambda i, t, k: (layer, 0, nk + k)),
            pl.BlockSpec((None, tf, d), lambda i, t, k: (layer, k, 0)),
        ],
        out_specs=pl.BlockSpec((1, tm, d), lambda i, t, k: (i, t, 0)),
        out_shape=jax.ShapeDtypeStruct((b, l, d), F32),
        scratch_shapes=[pltpu.VMEM((tm, d), BF16)],
        compiler_params=_params("parallel", "parallel", "arbitrary", vmem_limit_bytes=SWIGLU_VMEM_LIMIT_BYTES),
        name="swiglu",
    )(x, mod, npre, npost, w_gate_up, w_gate_up, w_down)


def _pool_kernel(x_ref, xp_ref, xn_ref, mod_ref, npre_ref, npost_ref, wp_ref, ps_ref, o_ref, h_ref, *, seq_len):
    t = pl.program_id(1)
    tm = x_ref.shape[1]
    gd = wp_ref.shape[1]
    d = x_ref.shape[2]
    npre = npre_ref[...]
    shift, scale = mod_ref[0, 0:1, :], mod_ref[0, 1:2, :]
    span = tm + 2 * POOL_HALO
    n_groups = len(POOL_WINDOWS)

    def put(rows, value):
        for g in range(n_groups):
            h_ref[g, rows, :] = value[:, g * gd:(g + 1) * gd]

    hp = _norm_mod(xp_ref[0], npre, shift, scale)
    hn = _norm_mod(xn_ref[0], npre, shift, scale)
    put(pl.ds(0, POOL_HALO), jnp.where(t > 0, hp, 0.0))
    w_mod = npre * (1.0 + scale)
    for i in range(tm // ROW_SLAB):
        put(pl.ds(POOL_HALO + i * ROW_SLAB, ROW_SLAB),
            _rms(x_ref[0, pl.ds(i * ROW_SLAB, ROW_SLAB), :], w_mod) + shift)
    put(pl.ds(POOL_HALO + tm, POOL_HALO), jnp.where(t < pl.num_programs(1) - 1, hn, 0.0))

    pos = t * tm + lax.broadcasted_iota(jnp.int32, (tm, LANES), 0)
    ss = jnp.zeros((tm, 1), F32)
    for g, w in enumerate(POOL_WINDOWS):
        cols = pl.ds(g * gd, gd)
        cur = h_ref[g]
        cur = cur + pltpu.roll(cur, 1, 0)
        reach = 1
        while 2 * reach < w:
            cur = pltpu.roll(cur, reach, 0) + pltpu.roll(cur, span - reach, 0)
            reach *= 2
        acc = cur[POOL_HALO:POOL_HALO + tm]
        cnt = jnp.minimum(pos + w // 2, seq_len) - jnp.maximum(pos - w // 2, 0)
        inv_cnt = jnp.concatenate([1.0 / cnt.astype(F32)] * (gd // LANES), axis=1)
        p = acc * inv_cnt - h_ref[g, pl.ds(POOL_HALO, tm), :]
        y = jnp.dot(p.astype(BF16), wp_ref[g], preferred_element_type=F32) * ps_ref[:, cols]
        o_ref[0, :, cols] = y
        ss = ss + jnp.sum(y * y, axis=-1, keepdims=True)

    rstd = lax.rsqrt(ss * (1.0 / d) + RMS_EPS)
    o_ref[0] = x_ref[0] + o_ref[0] * rstd * (mod_ref[0, 2:3, :] * npost_ref[...])


def _pool_call(x, mod, npre, npost, w_pool, pool_scale, tm):
    b, l, d = x.shape
    hb = tm // POOL_HALO
    last_hb = l // POOL_HALO - 1
    return pl.pallas_call(
        functools.partial(_pool_kernel, seq_len=l),
        grid=(b, l // tm),
        in_specs=[
            pl.BlockSpec((1, tm, d), lambda i, t: (i, t, 0)),
            pl.BlockSpec((1, POOL_HALO, d), lambda i, t: (i, jnp.maximum(t * hb - 1, 0), 0)),
            pl.BlockSpec((1, POOL_HALO, d), lambda i, t: (i, jnp.minimum((t + 1) * hb, last_hb), 0)),
            pl.BlockSpec((1, 6, d), lambda i, t: (i, 0, 0)),
            pl.BlockSpec((1, d), lambda i, t: (0, 0)),
            pl.BlockSpec((1, d), lambda i, t: (0, 0)),
            _resident(w_pool.shape),
            pl.BlockSpec((1, d), lambda i, t: (0, 0)),
        ],
        out_specs=pl.BlockSpec((1, tm, d), lambda i, t: (i, t, 0)),
        out_shape=jax.ShapeDtypeStruct((b, l, d), F32),
        scratch_shapes=[pltpu.VMEM((w_pool.shape[0], tm + 2 * POOL_HALO, w_pool.shape[1]), F32)],
        compiler_params=_params("parallel", "parallel"),
        name="pool_mixer",
    )(x, x, x, mod, npre, npost, w_pool, pool_scale)


def _rope(x, cos, sin_signed, low_half):
    out = []
    for j in range(x.shape[1] // LANES):
        c = x[:, j * LANES:(j + 1) * LANES]
        rot = jnp.where(low_half, pltpu.roll(c, LANES - HEAD_DIM // 4, 1), pltpu.roll(c, HEAD_DIM // 4, 1))
        out.append(c * cos + rot * sin_signed)
    return jnp.concatenate(out, axis=1)


def _qkv_kernel(*refs, rope, q_dim, k_dim):
    if rope:
        x_ref, mod_ref, npre_ref, w_ref, cos_ref, sin_ref, q_ref, k_ref, vt_ref = refs
    else:
        x_ref, mod_ref, npre_ref, w_ref, q_ref, k_ref, vt_ref = refs
    h = _norm_mod(x_ref[0], npre_ref[...], mod_ref[0, 0:1, :], mod_ref[0, 1:2, :]).astype(BF16)
    qkv = jnp.dot(h, w_ref[...], preferred_element_type=F32)
    q = qkv[:, :q_dim]
    k = qkv[:, q_dim:q_dim + k_dim]
    v = qkv[:, q_dim + k_dim:]
    if rope:
        cos, sin_signed = cos_ref[...], sin_ref[...]
        lane = lax.broadcasted_iota(jnp.int32, cos.shape, 1)
        low_half = (lane % (HEAD_DIM // 2)) < (HEAD_DIM // 4)
        q = _rope(q, cos, sin_signed, low_half)
        k = _rope(k, cos, sin_signed, low_half)
    q_ref[0] = (q * (HEAD_DIM ** -0.5 * LOG2E)).astype(BF16)
    k_ref[0] = k.astype(BF16)
    vt_ref[0] = v.T.astype(BF16)


def _qkv_call(x, mod, npre, w_qkv_dup, q_dim, rope_tables, tm):
    b, l, d = x.shape
    n = w_qkv_dup.shape[1]
    v_dim = (n - q_dim) // 3
    k_dim = 2 * v_dim
    rope = rope_tables is not None
    in_specs = [
        pl.BlockSpec((1, tm, d), lambda i, t: (i, t, 0)),
        pl.BlockSpec((1, 6, d), lambda i, t: (i, 0, 0)),
        pl.BlockSpec((1, d), lambda i, t: (0, 0)),
        _resident(w_qkv_dup.shape),
    ]
    args = [x, mod, npre, w_qkv_dup]
    if rope:
        in_specs += [pl.BlockSpec((tm, LANES), lambda i, t: (t, 0))] * 2
        args += list(rope_tables)
    return pl.pallas_call(
        functools.partial(_qkv_kernel, rope=rope, q_dim=q_dim, k_dim=k_dim),
        grid=(b, l // tm),
        in_specs=in_specs,
        out_specs=[
            pl.BlockSpec((1, tm, q_dim), lambda i, t: (i, t, 0)),
            pl.BlockSpec((1, tm, k_dim), lambda i, t: (i, t, 0)),
            pl.BlockSpec((1, v_dim, tm), lambda i, t: (i, 0, t)),
        ],
        out_shape=[
            jax.ShapeDtypeStruct((b, l, q_dim), BF16),
            jax.ShapeDtypeStruct((b, l, k_dim), BF16),
            jax.ShapeDtypeStruct((b, v_dim, l), BF16),
        ],
        compiler_params=_params("parallel", "parallel"),
        name="qkv_rope" if rope else "qkv_ctx",
    )(*args)


def _attend_scores(q_ref, sink_ref, h, k, row0, nq):
    low = lax.broadcasted_iota(jnp.int32, (nq, HEAD_PAIR), 1) < HEAD_DIM
    zero = jnp.zeros((), BF16)
    base = h * GQA_GROUP * HEAD_DIM
    q_rows, sink_cols = [], []
    for g in range(GQA_GROUP):
        pair = q_ref[0, pl.ds(row0, nq), pl.ds(base + (g // 2) * HEAD_PAIR, HEAD_PAIR)]
        q_rows.append(jnp.where(low if g % 2 == 0 else ~low, pair, zero))
        sink_cols.append(jnp.full((1, nq), sink_ref[h * GQA_GROUP + g] * LOG2E, F32))
    qg = jnp.concatenate(q_rows, axis=0)
    snk = jnp.concatenate(sink_cols, axis=1)
    s = lax.dot_general(k, qg, (((1,), (1,)), ((), ())), preferred_element_type=F32)
    return s, snk


def _attend_softmax(s, snk, masks):
    blocks, row = [], 0
    for first, bias in masks:
        if first > row:
            blocks.append(s[row:first])
        row = first + bias.shape[0]
        blocks.append(s[first:row] + bias)
    if masks:
        if row < s.shape[0]:
            blocks.append(s[row:])
        s = jnp.concatenate(blocks, axis=0)
    m = jnp.maximum(snk, jnp.max(s, axis=0, keepdims=True))
    return jnp.exp2(s - m).astype(BF16), jnp.exp2(snk - m)


def _attend_values(o_ref, h, p, sink_p, vt, row0, nq):
    base = h * GQA_GROUP * HEAD_DIM
    vt_ones = jnp.concatenate([vt, jnp.ones((BF16_ROWS, vt.shape[1]), BF16)], axis=0)
    acc = jnp.dot(vt_ones, p, preferred_element_type=F32)
    den = sink_p + acc[HEAD_DIM:HEAD_DIM + 1]
    acc = acc[:HEAD_DIM] * (1.0 / den)
    for j in range(GQA_GROUP // 2):
        pair_t = jnp.concatenate([acc[:, (2 * j) * nq:(2 * j + 1) * nq], acc[:, (2 * j + 1) * nq:(2 * j + 2) * nq]],
                                 axis=0)
        o_ref[0, pl.ds(row0, nq), pl.ds(base + j * HEAD_PAIR, HEAD_PAIR)] = pair_t.T.astype(o_ref.dtype)


def _attn_oproj_kernel(sink_ref, q_ref, kp_ref, kc_ref, kn_ref, vp_ref, vc_ref, vn_ref, kx_ref, vx_ref,
                       x_ref, mod_ref, npost_ref, w_ref, o_ref, a_scr, y_scr, *, steps_per_seq):
    n = pl.program_id(0)
    n_tiles = pl.num_programs(0) - 1
    slot = n % 2
    nq = Q_BLOCK
    rows = q_ref.shape[1]
    n_sub = rows // nq
    piece = 2 * LANES

    def attention(fillers):
        t = n % steps_per_seq
        shape = (nq, GQA_GROUP * nq)
        key = lax.broadcasted_iota(jnp.int32, shape, 0)
        qry = lax.broadcasted_iota(jnp.int32, shape, 1) % nq
        inner_prev = jnp.where(key >= qry, 0.0, NEG_INF)
        inner_next = jnp.where(key <= qry, 0.0, NEG_INF)
        first_prev = jnp.where((key >= qry) & (t > 0), 0.0, NEG_INF)
        last_next = jnp.where((key <= qry) & (t < steps_per_seq - 1), 0.0, NEG_INF)
        k_loc = jnp.concatenate([kp_ref[0], kc_ref[0], kn_ref[0]], axis=0)
        vt_loc = jnp.concatenate([vp_ref[0], vc_ref[0], vn_ref[0]], axis=1)
        n_kv = vt_loc.shape[0] // HEAD_DIM
        a_out = a_scr.at[pl.ds(slot, 1)]

        units = []
        for u in range(n_sub):
            k_all = jnp.concatenate([k_loc[u * nq:(u + 3) * nq], kx_ref[0]], axis=0)
            vt_all = jnp.concatenate([vt_loc[:, u * nq:(u + 3) * nq], vx_ref[0]], axis=1)
            masks = [(0, first_prev if u == 0 else inner_prev),
                     (2 * nq, last_next if u == n_sub - 1 else inner_next)]
            for h in range(n_kv):
                units.append((u * nq, h, k_all[:, h * HEAD_PAIR:(h + 1) * HEAD_PAIR],
                              vt_all[h * HEAD_DIM:(h + 1) * HEAD_DIM, :], masks))
        scores, probs = {}, {}
        for i in range(len(units) + 2):
            if i < len(units):
                row0, h, k, _, _ = units[i]
                scores[i] = _attend_scores(q_ref, sink_ref, h, k, row0, nq)
            if 0 <= i - 2 < len(units):
                row0, h, _, vt, _ = units[i - 2]
                _attend_values(a_out, h, *probs.pop(i - 2), vt, row0, nq)
            if 0 <= i - 1 < len(units):
                probs[i - 1] = _attend_softmax(*scores.pop(i - 1), units[i - 1][4])
            if i < len(fillers):
                fillers[i]()
        for f in fillers[len(units) + 2:]:
            f()

    def project(j):
        def run():
            cols = pl.ds(j * piece, piece)
            y_scr[:, cols] = jnp.dot(a_scr[1 - slot], w_ref[:, cols], preferred_element_type=F32)
        return run

    def finish():
        w = mod_ref[0, 2:3, :] * npost_ref[...]
        for i in range(rows // ROW_SLAB):
            sl = pl.ds(i * ROW_SLAB, ROW_SLAB)
            o_ref[0, sl, :] = x_ref[0, sl, :] + _rms(y_scr[sl, :], w)

    pieces = [project(j) for j in range(w_ref.shape[1] // piece)]

    @pl.when(n == 0)
    def _():
        attention([])

    @pl.when((n > 0) & (n < n_tiles))
    def _():
        attention(pieces)
        finish()

    @pl.when(n == n_tiles)
    def _():
        for f in pieces:
            f()
        finish()


def _attn_oproj_call(q, k, vt, kx, vxt, sinks, x, mod, npost, w_o, n_ctx, n_sub):
    b, l, qd = q.shape
    d = x.shape[2]
    kd, vd = k.shape[2], vt.shape[1]
    nb = l // Q_BLOCK
    rows = n_sub * Q_BLOCK
    spb = nb // n_sub
    n_tiles = b * spb
    cur = lambda n: jnp.minimum(n, n_tiles - 1)
    done = lambda n: jnp.maximum(n - 1, 0)
    prev = lambda t: jnp.maximum(t * n_sub - 1, 0)
    nxt = lambda t: jnp.minimum((t + 1) * n_sub, nb - 1)
    k_halo = lambda f: pl.BlockSpec((1, Q_BLOCK, kd), lambda n: (cur(n) // spb, f(cur(n) % spb), 0))
    v_halo = lambda f: pl.BlockSpec((1, vd, Q_BLOCK), lambda n: (cur(n) // spb, 0, f(cur(n) % spb)))
    return pl.pallas_call(
        functools.partial(_attn_oproj_kernel, steps_per_seq=spb),
        grid=(n_tiles + 1,),
        in_specs=[
            pl.BlockSpec(memory_space=pltpu.SMEM),
            pl.BlockSpec((1, rows, qd), lambda n: (cur(n) // spb, cur(n) % spb, 0)),
            k_halo(prev), pl.BlockSpec((1, rows, kd), lambda n: (cur(n) // spb, cur(n) % spb, 0)), k_halo(nxt),
            v_halo(prev), pl.BlockSpec((1, vd, rows), lambda n: (cur(n) // spb, 0, cur(n) % spb)), v_halo(nxt),
            pl.BlockSpec((1, n_ctx, kd), lambda n: (0, cur(n) // spb, 0)),
            pl.BlockSpec((1, vd, n_ctx), lambda n: (0, 0, cur(n) // spb)),
            pl.BlockSpec((1, rows, d), lambda n: (done(n) // spb, done(n) % spb, 0)),
            pl.BlockSpec((1, 6, d), lambda n: (done(n) // spb, 0, 0)),
            pl.BlockSpec((1, d), lambda n: (0, 0)),
            _resident(w_o.shape),
        ],
        out_specs=pl.BlockSpec((1, rows, d), lambda n: (done(n) // spb, done(n) % spb, 0)),
        out_shape=jax.ShapeDtypeStruct(x.shape, F32),
        scratch_shapes=[pltpu.VMEM((2, rows, qd), BF16), pltpu.VMEM((rows, d), F32)],
        compiler_params=_params("arbitrary"),
        name="window_attn_proj",
    )(sinks, q, k, k, k, vt, vt, vt, kx, vxt, x, mod, npost, w_o)


def _ctx_attn_kernel(sink_ref, q_ref, k_ref, vt_ref, o_ref):
    nq = q_ref.shape[1]
    for h in range(vt_ref.shape[1] // HEAD_DIM):
        sc = _attend_scores(q_ref, sink_ref, h, k_ref[0, :, h * HEAD_PAIR:(h + 1) * HEAD_PAIR], 0, nq)
        _attend_values(o_ref, h, *_attend_softmax(*sc, []), vt_ref[0, h * HEAD_DIM:(h + 1) * HEAD_DIM, :], 0, nq)


def _ctx_attn_call(q, k, vt, sinks, n_ctx):
    qd, kd, vd = q.shape[2], k.shape[2], vt.shape[1]
    return pl.pallas_call(
        _ctx_attn_kernel,
        grid=(q.shape[1] // n_ctx,),
        in_specs=[
            pl.BlockSpec(memory_space=pltpu.SMEM),
            pl.BlockSpec((1, n_ctx, qd), lambda i: (0, i, 0)),
            pl.BlockSpec((1, n_ctx, kd), lambda i: (0, i, 0)),
            pl.BlockSpec((1, vd, n_ctx), lambda i: (0, 0, i)),
        ],
        out_specs=pl.BlockSpec((1, n_ctx, qd), lambda i: (0, i, 0)),
        out_shape=jax.ShapeDtypeStruct(q.shape, BF16),
        compiler_params=_params("parallel"),
        name="ctx_attn",
    )(sinks, q, k, vt)


def _oproj_kernel(a_ref, x_ref, mod_ref, npost_ref, w_ref, o_ref):
    y = jnp.dot(a_ref[0], w_ref[...], preferred_element_type=F32)
    o_ref[0] = x_ref[0] + _rms(y, mod_ref[0, 2:3, :] * npost_ref[...])


def _oproj_call(a, x, mod, npost, w_o, tm):
    b, l, d = x.shape
    ad = a.shape[2]
    return pl.pallas_call(
        _oproj_kernel,
        grid=(b, l // tm),
        in_specs=[
            pl.BlockSpec((1, tm, ad), lambda i, t: (i, t, 0)),
            pl.BlockSpec((1, tm, d), lambda i, t: (i, t, 0)),
            pl.BlockSpec((1, 6, d), lambda i, t: (i, 0, 0)),
            pl.BlockSpec((1, d), lambda i, t: (0, 0)),
            _resident(w_o.shape),
        ],
        out_specs=pl.BlockSpec((1, tm, d), lambda i, t: (i, t, 0)),
        out_shape=jax.ShapeDtypeStruct((b, l, d), F32),
        compiler_params=_params("parallel", "parallel"),
        name="attn_out_proj",
    )(a, x, mod, npost, w_o)


def _rope_tables(l):
    axis_dim = HEAD_DIM // 2
    rows_n = l // GRID_W
    row = jnp.repeat(jnp.arange(rows_n), GRID_W).astype(F32)
    col = jnp.tile(jnp.arange(GRID_W), rows_n).astype(F32)
    inv = 1.0 / (ROPE_BASE ** (jnp.arange(0, axis_dim, 2, dtype=F32) / axis_dim))
    ang_r = row[:, None] * inv[None, :]
    ang_c = col[:, None] * inv[None, :]
    ang = jnp.concatenate([ang_r, ang_r, ang_c, ang_c], axis=-1)
    sign = jnp.tile(jnp.concatenate([-jnp.ones(axis_dim // 2, F32), jnp.ones(axis_dim // 2, F32)]), 2)
    cos, sin = jnp.cos(ang), jnp.sin(ang) * sign[None, :]
    return jnp.tile(cos, (1, LANES // HEAD_DIM)), jnp.tile(sin, (1, LANES // HEAD_DIM))


def _dup_heads(w, n_heads):
    d = w.shape[0]
    w = w.reshape(d, n_heads, 1, HEAD_DIM)
    return jnp.broadcast_to(w, (d, n_heads, 2, HEAD_DIM)).reshape(d, n_heads * HEAD_PAIR)


def _token_tile(l, target):
    return min(l, target)


def kernel(x, c, ctx, c_ctx, w_ada, b_ada, norm_pre_mix, norm_post_mix, norm_pre_ffn, norm_post_ffn,
           w_pool, pool_scale, w_qkv, w_o, attn_sinks, w_gate_up, w_down):
    b, l, d = x.shape
    depth = w_ada.shape[0]
    n_mixers = 2
    q_dim = w_o.shape[1]
    kv_heads = (w_qkv.shape[2] - q_dim) // (2 * HEAD_DIM)
    kv_dim = kv_heads * HEAD_DIM

    rows = -(-(b + 1) // BF16_ROWS) * BF16_ROWS
    c_rows = jnp.concatenate([c, c_ctx[None, :], jnp.zeros((rows - b - 1, d), F32)], axis=0)
    ada = _ada_call(c_rows, w_ada, b_ada).reshape(depth, rows, 6, d)
    mod_x = ada[:, :b]
    mod_c = jnp.broadcast_to(ada[:, b:b + 1], (depth, b, 6, d))

    bf = lambda w: w.astype(BF16)
    w_gu, w_dn = bf(w_gate_up), bf(w_down)

    def qkv_weights(w):
        w = bf(w)
        return jnp.concatenate([w[:, :q_dim], _dup_heads(w[:, q_dim:q_dim + kv_dim], kv_heads),
                                w[:, q_dim + kv_dim:]], axis=-1)

    tables = _rope_tables(l)

    lc = ctx.shape[1]
    tm_x = _token_tile(l, 512)
    tm_ffn = _token_tile(l, 1024)
    tm_c = _token_tile(b * lc, 512)
    tf = 512

    flat = lambda a: a.reshape(1, b * lc, a.shape[-1])
    unflat = lambda a: a.reshape(b, lc, a.shape[-1])

    for i in range(depth):
        last = i == depth - 1
        j = i // n_mixers
        npre, npost = norm_pre_mix[i][None, :], norm_post_mix[i][None, :]
        if i % n_mixers == 0:
            ps, w_p = pool_scale[j][None, :], bf(w_pool[j])
            x = _pool_call(x, mod_x[i], npre, npost, w_p, ps, tm_x)
            if not last:
                ctx = _pool_call(ctx, mod_c[i], npre, npost, w_p, ps, _token_tile(lc, 512))
        else:
            w_in, w_out = qkv_weights(w_qkv[j]), bf(w_o[j])
            q, k, v = _qkv_call(x, mod_x[i], npre, w_in, q_dim, tables, tm_x)
            qc, kc, vct = _qkv_call(flat(ctx), mod_c[i][:1], npre, w_in, q_dim, None, tm_c)
            x = _attn_oproj_call(q, k, v, kc, vct, attn_sinks[j], x, mod_x[i], npost, w_out, lc,
                                 2 if (l // Q_BLOCK) % 2 == 0 else 1)
            if not last:
                ac = _ctx_attn_call(qc, kc, vct, attn_sinks[j], lc)
                ctx = unflat(_oproj_call(ac, flat(ctx), mod_c[i][:1], npost, w_out, tm_c))

        npre, npost = norm_pre_ffn[i][None, :], norm_post_ffn[i][None, :]
        x = _ffn_call(x, mod_x[i], npre, npost, w_gu, w_dn, i, tm_ffn, tf)
        if not last:
            ctx = unflat(_ffn_call(flat(ctx), mod_c[i][:1], npre, npost, w_gu, w_dn, i, _token_tile(b * lc, 1024), tf))
    return x
```

```python
import functools

import jax
import jax.numpy as jnp
from jax import lax
from jax.experimental import pallas as pl
from jax.experimental.pallas import tpu as pltpu

GRID_W = 64
POOL_WINDOWS = (2, 4, 8, 16)
HEAD_DIM = 64
GQA_GROUP = 8
WINDOW = 128
Q_BLOCK = 128
ROPE_BASE = 10000.0
RMS_EPS = 1e-6
NEG_INF = -1e30
LOG2E = 1.4426950408889634

LANES = 128
SUBLANES = 8
BF16_ROWS = 16
VMEM_LIMIT_BYTES = 56 * 1024 * 1024
SWIGLU_VMEM_LIMIT_BYTES = 63 * 1024 * 1024

ROW_SLAB = BF16_ROWS
POOL_HALO = SUBLANES
HEAD_PAIR = 2 * HEAD_DIM

F32 = jnp.float32
BF16 = jnp.bfloat16


def _params(*semantics, vmem_limit_bytes=VMEM_LIMIT_BYTES):
    return pltpu.CompilerParams(dimension_semantics=semantics, vmem_limit_bytes=vmem_limit_bytes)


def _resident(shape):
    nd = len(shape)
    return pl.BlockSpec(shape, lambda *_: (0,) * nd, pipeline_mode=pl.Buffered(1))


def _rms(x, w):
    ms = jnp.mean(x * x, axis=-1, keepdims=True)
    return x * lax.rsqrt(ms + RMS_EPS) * w


def _norm_mod(x, w, shift, scale):
    return _rms(x, w * (1.0 + scale)) + shift


def _ada_kernel(c_ref, w_ref, b_ref, o_ref):
    s = jax.nn.silu(c_ref[...]).astype(BF16)
    o_ref[0] = jnp.dot(s, w_ref[0].astype(BF16), preferred_element_type=F32) + b_ref[0]


def _ada_call(c_rows, w_ada, b_ada):
    depth, d, n = w_ada.shape
    rows = c_rows.shape[0]
    tn = 1024
    return pl.pallas_call(
        _ada_kernel,
        grid=(depth, n // tn),
        in_specs=[
            pl.BlockSpec((rows, d), lambda i, j: (0, 0)),
            pl.BlockSpec((1, d, tn), lambda i, j: (i, 0, j)),
            pl.BlockSpec((1, 1, tn), lambda i, j: (i, 0, j)),
        ],
        out_specs=pl.BlockSpec((1, rows, tn), lambda i, j: (i, 0, j)),
        out_shape=jax.ShapeDtypeStruct((depth, rows, n), F32),
        compiler_params=_params("parallel", "parallel"),
        name="ada_proj",
    )(c_rows, w_ada, b_ada.reshape(depth, 1, n))


def _ffn_kernel(x_ref, mod_ref, npre_ref, npost_ref, wg_ref, wu_ref, wd_ref, o_ref, h_ref):
    k = pl.program_id(2)
    last = pl.num_programs(2) - 1
    tm = x_ref.shape[1]

    def prologue():
        w = npre_ref[...] * (1.0 + mod_ref[0, 4:5, :])
        shift = mod_ref[0, 3:4, :]
        for i in range(tm // ROW_SLAB):
            rows = pl.ds(i * ROW_SLAB, ROW_SLAB)
            h_ref[rows, :] = (_rms(x_ref[0, rows, :], w) + shift).astype(BF16)

    def chunk(first):
        h = h_ref[...]
        g = jnp.dot(h, wg_ref[...], preferred_element_type=F32)
        u = jnp.dot(h, wu_ref[...], preferred_element_type=F32)
        a = (jax.nn.silu(g) * u).astype(BF16)
        part = jnp.dot(a, wd_ref[...], preferred_element_type=F32)
        if first:
            o_ref[0] = part
        else:
            o_ref[0] += part

    def epilogue():
        w = mod_ref[0, 5:6, :] * npost_ref[...]
        for i in range(tm // ROW_SLAB):
            rows = pl.ds(i * ROW_SLAB, ROW_SLAB)
            o_ref[0, rows, :] = x_ref[0, rows, :] + _rms(o_ref[0, rows, :], w)

    @pl.when(k == 0)
    def _():
        prologue()
        chunk(True)

    @pl.when((k > 0) & (k < last))
    def _():
        chunk(False)

    @pl.when(k == last)
    def _():
        chunk(False)
        epilogue()


def _ffn_call(x, mod, npre, npost, w_gate_up, w_down, layer, tm, tf):
    b, l, d = x.shape
    f = w_down.shape[1]
    nk = f // tf
    return pl.pallas_call(
        _ffn_kernel,
        grid=(b, l // tm, nk),
        in_specs=[
            pl.BlockSpec((1, tm, d), lambda i, t, k: (i, t, 0)),
            pl.BlockSpec((1, 6, d), lambda i, t, k: (i, 0, 0)),
            pl.BlockSpec((1, d), lambda i, t, k: (0, 0)),
            pl.BlockSpec((1, d), lambda i, t, k: (0, 0)),
            pl.BlockSpec((None, d, tf), lambda i, t, k: (layer, 0, k)),
            pl.BlockSpec((None, d, tf), lambda i, t, k: (layer, 0, nk + k)),
            pl.BlockSpec((None, tf, d), lambda i, t, k: (layer, k, 0)),
        ],
        out_specs=pl.BlockSpec((1, tm, d), lambda i, t, k: (i, t, 0)),
        out_shape=jax.ShapeDtypeStruct((b, l, d), F32),
        scratch_shapes=[pltpu.VMEM((tm, d), BF16)],
        compiler_params=_params("parallel", "parallel", "arbitrary", vmem_limit_bytes=SWIGLU_VMEM_LIMIT_BYTES),
        name="swiglu",
    )(x, mod, npre, npost, w_gate_up, w_gate_up, w_down)


def _pool_kernel(x_ref, xp_ref, xn_ref, mod_ref, npre_ref, npost_ref, wp_ref, ps_ref, o_ref, h_ref, *, seq_len):
    t = pl.program_id(1)
    tm = x_ref.shape[1]
    gd = wp_ref.shape[1]
    d = x_ref.shape[2]
    npre = npre_ref[...]
    shift, scale = mod_ref[0, 0:1, :], mod_ref[0, 1:2, :]
    span = tm + 2 * POOL_HALO
    n_groups = len(POOL_WINDOWS)

    def put(rows, value):
        for g in range(n_groups):
            h_ref[g, rows, :] = value[:, g * gd:(g + 1) * gd]

    hp = _norm_mod(xp_ref[0], npre, shift, scale)
    hn = _norm_mod(xn_ref[0], npre, shift, scale)
    put(pl.ds(0, POOL_HALO), jnp.where(t > 0, hp, 0.0))
    w_mod = npre * (1.0 + scale)
    for i in range(tm // ROW_SLAB):
        put(pl.ds(POOL_HALO + i * ROW_SLAB, ROW_SLAB),
            _rms(x_ref[0, pl.ds(i * ROW_SLAB, ROW_SLAB), :], w_mod) + shift)
    put(pl.ds(POOL_HALO + tm, POOL_HALO), jnp.where(t < pl.num_programs(1) - 1, hn, 0.0))

    pos = t * tm + lax.broadcasted_iota(jnp.int32, (tm, LANES), 0)
    ss = jnp.zeros((tm, 1), F32)
    for g, w in enumerate(POOL_WINDOWS):
        cols = pl.ds(g * gd, gd)
        cur = h_ref[g]
        cur = cur + pltpu.roll(cur, 1, 0)
        reach = 1
        while 2 * reach < w:
            cur = pltpu.roll(cur, reach, 0) + pltpu.roll(cur, span - reach, 0)
            reach *= 2
        acc = cur[POOL_HALO:POOL_HALO + tm]
        cnt = jnp.minimum(pos + w // 2, seq_len) - jnp.maximum(pos - w // 2, 0)
        inv_cnt = jnp.concatenate([1.0 / cnt.astype(F32)] * (gd // LANES), axis=1)
        p = acc * inv_cnt - h_ref[g, pl.ds(POOL_HALO, tm), :]
        y = jnp.dot(p.astype(BF16), wp_ref[g], preferred_element_type=F32) * ps_ref[:, cols]
        o_ref[0, :, cols] = y
        ss = ss + jnp.sum(y * y, axis=-1, keepdims=True)

    rstd = lax.rsqrt(ss * (1.0 / d) + RMS_EPS)
    o_ref[0] = x_ref[0] + o_ref[0] * rstd * (mod_ref[0, 2:3, :] * npost_ref[...])


def _pool_call(x, mod, npre, npost, w_pool, pool_scale, tm):
    b, l, d = x.shape
    hb = tm // POOL_HALO
    last_hb = l // POOL_HALO - 1
    return pl.pallas_call(
        functools.partial(_pool_kernel, seq_len=l),
        grid=(b, l // tm),
        in_specs=[
            pl.BlockSpec((1, tm, d), lambda i, t: (i, t, 0)),
            pl.BlockSpec((1, POOL_HALO, d), lambda i, t: (i, jnp.maximum(t * hb - 1, 0), 0)),
            pl.BlockSpec((1, POOL_HALO, d), lambda i, t: (i, jnp.minimum((t + 1) * hb, last_hb), 0)),
            pl.BlockSpec((1, 6, d), lambda i, t: (i, 0, 0)),
            pl.BlockSpec((1, d), lambda i, t: (0, 0)),
            pl.BlockSpec((1, d), lambda i, t: (0, 0)),
            _resident(w_pool.shape),
            pl.BlockSpec((1, d), lambda i, t: (0, 0)),
        ],
        out_specs=pl.BlockSpec((1, tm, d), lambda i, t: (i, t, 0)),
        out_shape=jax.ShapeDtypeStruct((b, l, d), F32),
        scratch_shapes=[pltpu.VMEM((w_pool.shape[0], tm + 2 * POOL_HALO, w_pool.shape[1]), F32)],
        compiler_params=_params("parallel", "parallel"),
        name="pool_mixer",
    )(x, x, x, mod, npre, npost, w_pool, pool_scale)


def _rope(x, cos, sin_signed, low_half):
    out = []
    for j in range(x.shape[1] // LANES):
        c = x[:, j * LANES:(j + 1) * LANES]
        rot = jnp.where(low_half, pltpu.roll(c, LANES - HEAD_DIM // 4, 1), pltpu.roll(c, HEAD_DIM // 4, 1))
        out.append(c * cos + rot * sin_signed)
    return jnp.concatenate(out, axis=1)


def _qkv_kernel(*refs, rope, q_dim, k_dim, n_casts):
    n_in = (6 if rope else 4) + n_casts
    cast_in, cast_out = refs[n_in - n_casts:n_in], refs[n_in + 3:]
    if rope:
        x_ref, mod_ref, npre_ref, w_ref, cos_ref, sin_ref = refs[:6]
    else:
        x_ref, mod_ref, npre_ref, w_ref = refs[:4]
    q_ref, k_ref, vt_ref = refs[n_in:n_in + 3]
    for src, dst in zip(cast_in, cast_out):
        dst[...] = src[...].astype(BF16)
    h = _norm_mod(x_ref[0], npre_ref[...], mod_ref[0, 0:1, :], mod_ref[0, 1:2, :]).astype(BF16)
    qkv = jnp.dot(h, w_ref[...], preferred_element_type=F32)
    q = qkv[:, :q_dim]
    k = qkv[:, q_dim:q_dim + k_dim]
    v = qkv[:, q_dim + k_dim:]
    if rope:
        cos, sin_signed = cos_ref[...], sin_ref[...]
        lane = lax.broadcasted_iota(jnp.int32, cos.shape, 1)
        low_half = (lane % (HEAD_DIM // 2)) < (HEAD_DIM // 4)
        q = _rope(q, cos, sin_signed, low_half)
        k = _rope(k, cos, sin_signed, low_half)
    q_ref[0] = (q * (HEAD_DIM ** -0.5 * LOG2E)).astype(BF16)
    k_ref[0] = k.astype(BF16)
    vt_ref[0] = v.T.astype(BF16)


def _qkv_call(x, mod, npre, w_qkv_dup, q_dim, rope_tables, tm, casts=()):
    b, l, d = x.shape
    n = w_qkv_dup.shape[1]
    v_dim = (n - q_dim) // 3
    k_dim = 2 * v_dim
    rope = rope_tables is not None
    in_specs = [
        pl.BlockSpec((1, tm, d), lambda i, t: (i, t, 0)),
        pl.BlockSpec((1, 6, d), lambda i, t: (i, 0, 0)),
        pl.BlockSpec((1, d), lambda i, t: (0, 0)),
        _resident(w_qkv_dup.shape),
    ]
    args = [x, mod, npre, w_qkv_dup]
    if rope:
        in_specs += [pl.BlockSpec((tm, LANES), lambda i, t: (t, 0))] * 2
        args += list(rope_tables)
    out_specs = [
        pl.BlockSpec((1, tm, q_dim), lambda i, t: (i, t, 0)),
        pl.BlockSpec((1, tm, k_dim), lambda i, t: (i, t, 0)),
        pl.BlockSpec((1, v_dim, tm), lambda i, t: (i, 0, t)),
    ]
    out_shape = [
        jax.ShapeDtypeStruct((b, l, q_dim), BF16),
        jax.ShapeDtypeStruct((b, l, k_dim), BF16),
        jax.ShapeDtypeStruct((b, v_dim, l), BF16),
    ]
    tiles = l // tm
    n_steps = b * tiles
    for w, layer in casts:
        _, rows, cols = w.shape
        n_blocks = n_steps
        while rows % (n_blocks * BF16_ROWS):
            n_blocks //= 2
        hold, br = n_steps // n_blocks, rows // n_blocks
        in_specs.append(pl.BlockSpec((None, br, cols),
                                     lambda i, t, layer=layer, hold=hold: (layer, (i * tiles + t) // hold, 0)))
        out_specs.append(pl.BlockSpec((br, cols), lambda i, t, hold=hold: ((i * tiles + t) // hold, 0)))
        out_shape.append(jax.ShapeDtypeStruct((rows, cols), BF16))
        args.append(w)
    return pl.pallas_call(
        functools.partial(_qkv_kernel, rope=rope, q_dim=q_dim, k_dim=k_dim, n_casts=len(casts)),
        grid=(b, tiles),
        in_specs=in_specs,
        out_specs=out_specs,
        out_shape=out_shape,
        compiler_params=_params("arbitrary", "arbitrary"),
        name="qkv_rope" if rope else "qkv_ctx",
    )(*args)


def _attend_scores(q_ref, sink_ref, h, k, row0, nq):
    low = lax.broadcasted_iota(jnp.int32, (nq, HEAD_PAIR), 1) < HEAD_DIM
    zero = jnp.zeros((), BF16)
    base = h * GQA_GROUP * HEAD_DIM
    q_rows, sink_cols = [], []
    for g in range(GQA_GROUP):
        pair = q_ref[0, pl.ds(row0, nq), pl.ds(base + (g // 2) * HEAD_PAIR, HEAD_PAIR)]
        q_rows.append(jnp.where(low if g % 2 == 0 else ~low, pair, zero))
        sink_cols.append(jnp.full((1, nq), sink_ref[h * GQA_GROUP + g] * LOG2E, F32))
    qg = jnp.concatenate(q_rows, axis=0)
    snk = jnp.concatenate(sink_cols, axis=1)
    s = lax.dot_general(k, qg, (((1,), (1,)), ((), ())), preferred_element_type=F32)
    return s, snk


def _attend_softmax(s, snk, masks):
    blocks, row = [], 0
    for first, bias in masks:
        if first > row:
            blocks.append(s[row:first])
        row = first + bias.shape[0]
        blocks.append(s[first:row] + bias)
    if masks:
        if row < s.shape[0]:
            blocks.append(s[row:])
        s = jnp.concatenate(blocks, axis=0)
    m = jnp.maximum(snk, jnp.max(s, axis=0, keepdims=True))
    return jnp.exp2(s - m).astype(BF16), jnp.exp2(snk - m)


def _attend_values(o_ref, h, p, sink_p, vt, row0, nq):
    base = h * GQA_GROUP * HEAD_DIM
    vt_ones = jnp.concatenate([vt, jnp.ones((BF16_ROWS, vt.shape[1]), BF16)], axis=0)
    acc = jnp.dot(vt_ones, p, preferred_element_type=F32)
    den = sink_p + acc[HEAD_DIM:HEAD_DIM + 1]
    acc = acc[:HEAD_DIM] * (1.0 / den)
    for j in range(GQA_GROUP // 2):
        pair_t = jnp.concatenate([acc[:, (2 * j) * nq:(2 * j + 1) * nq], acc[:, (2 * j + 1) * nq:(2 * j + 2) * nq]],
                                 axis=0)
        o_ref[0, pl.ds(row0, nq), pl.ds(base + j * HEAD_PAIR, HEAD_PAIR)] = pair_t.T.astype(o_ref.dtype)


def _attn_oproj_kernel(sink_ref, q_ref, kp_ref, kc_ref, kn_ref, vp_ref, vc_ref, vn_ref, kx_ref, vx_ref,
                       x_ref, mod_ref, npost_ref, w_ref, o_ref, a_scr, y_scr, *, steps_per_seq):
    n = pl.program_id(0)
    n_tiles = pl.num_programs(0) - 1
    slot = n % 2
    nq = Q_BLOCK
    rows = q_ref.shape[1]
    n_sub = rows // nq
    piece = 2 * LANES

    def attention(fillers):
        t = n % steps_per_seq
        shape = (nq, GQA_GROUP * nq)
        key = lax.broadcasted_iota(jnp.int32, shape, 0)
        qry = lax.broadcasted_iota(jnp.int32, shape, 1) % nq
        inner_prev = jnp.where(key >= qry, 0.0, NEG_INF)
        inner_next = jnp.where(key <= qry, 0.0, NEG_INF)
        first_prev = jnp.where((key >= qry) & (t > 0), 0.0, NEG_INF)
        last_next = jnp.where((key <= qry) & (t < steps_per_seq - 1), 0.0, NEG_INF)
        k_loc = jnp.concatenate([kp_ref[0], kc_ref[0], kn_ref[0]], axis=0)
        vt_loc = jnp.concatenate([vp_ref[0], vc_ref[0], vn_ref[0]], axis=1)
        n_kv = vt_loc.shape[0] // HEAD_DIM
        a_out = a_scr.at[pl.ds(slot, 1)]

        units = []
        for u in range(n_sub):
            k_all = jnp.concatenate([k_loc[u * nq:(u + 3) * nq], kx_ref[0]], axis=0)
            vt_all = jnp.concatenate([vt_loc[:, u * nq:(u + 3) * nq], vx_ref[0]], axis=1)
            masks = [(0, first_prev if u == 0 else inner_prev),
                     (2 * nq, last_next if u == n_sub - 1 else inner_next)]
            for h in range(n_kv):
                units.append((u * nq, h, k_all[:, h * HEAD_PAIR:(h + 1) * HEAD_PAIR],
                              vt_all[h * HEAD_DIM:(h + 1) * HEAD_DIM, :], masks))
        scores, probs = {}, {}
        for i in range(len(units) + 2):
            if i < len(units):
                row0, h, k, _, _ = units[i]
                scores[i] = _attend_scores(q_ref, sink_ref, h, k, row0, nq)
            if 0 <= i - 2 < len(units):
                row0, h, _, vt, _ = units[i - 2]
                _attend_values(a_out, h, *probs.pop(i - 2), vt, row0, nq)
            if 0 <= i - 1 < len(units):
                probs[i - 1] = _attend_softmax(*scores.pop(i - 1), units[i - 1][4])
            if i < len(fillers):
                fillers[i]()
        for f in fillers[len(units) + 2:]:
            f()

    def project(j):
        def run():
            cols = pl.ds(j * piece, piece)
            y_scr[:, cols] = jnp.dot(a_scr[1 - slot], w_ref[:, cols], preferred_element_type=F32)
        return run

    def finish():
        w = mod_ref[0, 2:3, :] * npost_ref[...]
        for i in range(rows // ROW_SLAB):
            sl = pl.ds(i * ROW_SLAB, ROW_SLAB)
            o_ref[0, sl, :] = x_ref[0, sl, :] + _rms(y_scr[sl, :], w)

    pieces = [project(j) for j in range(w_ref.shape[1] // piece)]

    @pl.when(n == 0)
    def _():
        attention([])

    @pl.when((n > 0) & (n < n_tiles))
    def _():
        attention(pieces)
        finish()

    @pl.when(n == n_tiles)
    def _():
        for f in pieces:
            f()
        finish()


def _attn_oproj_call(q, k, vt, kx, vxt, sinks, x, mod, npost, w_o, n_ctx, n_sub):
    b, l, qd = q.shape
    d = x.shape[2]
    kd, vd = k.shape[2], vt.shape[1]
    nb = l // Q_BLOCK
    rows = n_sub * Q_BLOCK
    spb = nb // n_sub
    n_tiles = b * spb
    cur = lambda n: jnp.minimum(n, n_tiles - 1)
    done = lambda n: jnp.maximum(n - 1, 0)
    prev = lambda t: jnp.maximum(t * n_sub - 1, 0)
    nxt = lambda t: jnp.minimum((t + 1) * n_sub, nb - 1)
    k_halo = lambda f: pl.BlockSpec((1, Q_BLOCK, kd), lambda n: (cur(n) // spb, f(cur(n) % spb), 0))
    v_halo = lambda f: pl.BlockSpec((1, vd, Q_BLOCK), lambda n: (cur(n) // spb, 0, f(cur(n) % spb)))
    return pl.pallas_call(
        functools.partial(_attn_oproj_kernel, steps_per_seq=spb),
        grid=(n_tiles + 1,),
        in_specs=[
            pl.BlockSpec(memory_space=pltpu.SMEM),
            pl.BlockSpec((1, rows, qd), lambda n: (cur(n) // spb, cur(n) % spb, 0)),
            k_halo(prev), pl.BlockSpec((1, rows, kd), lambda n: (cur(n) // spb, cur(n) % spb, 0)), k_halo(nxt),
            v_halo(prev), pl.BlockSpec((1, vd, rows), lambda n: (cur(n) // spb, 0, cur(n) % spb)), v_halo(nxt),
            pl.BlockSpec((1, n_ctx, kd), lambda n: (0, cur(n) // spb, 0)),
            pl.BlockSpec((1, vd, n_ctx), lambda n: (0, 0, cur(n) // spb)),
            pl.BlockSpec((1, rows, d), lambda n: (done(n) // spb, done(n) % spb, 0)),
            pl.BlockSpec((1, 6, d), lambda n: (done(n) // spb, 0, 0)),
            pl.BlockSpec((1, d), lambda n: (0, 0)),
            _resident(w_o.shape),
        ],
        out_specs=pl.BlockSpec((1, rows, d), lambda n: (done(n) // spb, done(n) % spb, 0)),
        out_shape=jax.ShapeDtypeStruct(x.shape, F32),
        scratch_shapes=[pltpu.VMEM((2, rows, qd), BF16), pltpu.VMEM((rows, d), F32)],
        compiler_params=_params("arbitrary"),
        name="window_attn_proj",
    )(sinks, q, k, k, k, vt, vt, vt, kx, vxt, x, mod, npost, w_o)


def _ctx_attn_kernel(sink_ref, q_ref, k_ref, vt_ref, o_ref):
    nq = q_ref.shape[1]
    for h in range(vt_ref.shape[1] // HEAD_DIM):
        sc = _attend_scores(q_ref, sink_ref, h, k_ref[0, :, h * HEAD_PAIR:(h + 1) * HEAD_PAIR], 0, nq)
        _attend_values(o_ref, h, *_attend_softmax(*sc, []), vt_ref[0, h * HEAD_DIM:(h + 1) * HEAD_DIM, :], 0, nq)


def _ctx_attn_call(q, k, vt, sinks, n_ctx):
    qd, kd, vd = q.shape[2], k.shape[2], vt.shape[1]
    return pl.pallas_call(
        _ctx_attn_kernel,
        grid=(q.shape[1] // n_ctx,),
        in_specs=[
            pl.BlockSpec(memory_space=pltpu.SMEM),
            pl.BlockSpec((1, n_ctx, qd), lambda i: (0, i, 0)),
            pl.BlockSpec((1, n_ctx, kd), lambda i: (0, i, 0)),
            pl.BlockSpec((1, vd, n_ctx), lambda i: (0, 0, i)),
        ],
        out_specs=pl.BlockSpec((1, n_ctx, qd), lambda i: (0, i, 0)),
        out_shape=jax.ShapeDtypeStruct(q.shape, BF16),
        compiler_params=_params("parallel"),
        name="ctx_attn",
    )(sinks, q, k, vt)


def _oproj_kernel(a_ref, x_ref, mod_ref, npost_ref, w_ref, o_ref):
    y = jnp.dot(a_ref[0], w_ref[...], preferred_element_type=F32)
    o_ref[0] = x_ref[0] + _rms(y, mod_ref[0, 2:3, :] * npost_ref[...])


def _oproj_call(a, x, mod, npost, w_o, tm):
    b, l, d = x.shape
    ad = a.shape[2]
    return pl.pallas_call(
        _oproj_kernel,
        grid=(b, l // tm),
        in_specs=[
            pl.BlockSpec((1, tm, ad), lambda i, t: (i, t, 0)),
            pl.BlockSpec((1, tm, d), lambda i, t: (i, t, 0)),
            pl.BlockSpec((1, 6, d), lambda i, t: (i, 0, 0)),
            pl.BlockSpec((1, d), lambda i, t: (0, 0)),
            _resident(w_o.shape),
        ],
        out_specs=pl.BlockSpec((1, tm, d), lambda i, t: (i, t, 0)),
        out_shape=jax.ShapeDtypeStruct((b, l, d), F32),
        compiler_params=_params("parallel", "parallel"),
        name="attn_out_proj",
    )(a, x, mod, npost, w_o)


def _rope_tables(l):
    axis_dim = HEAD_DIM // 2
    rows_n = l // GRID_W
    row = jnp.repeat(jnp.arange(rows_n), GRID_W).astype(F32)
    col = jnp.tile(jnp.arange(GRID_W), rows_n).astype(F32)
    inv = 1.0 / (ROPE_BASE ** (jnp.arange(0, axis_dim, 2, dtype=F32) / axis_dim))
    ang_r = row[:, None] * inv[None, :]
    ang_c = col[:, None] * inv[None, :]
    ang = jnp.concatenate([ang_r, ang_r, ang_c, ang_c], axis=-1)
    sign = jnp.tile(jnp.concatenate([-jnp.ones(axis_dim // 2, F32), jnp.ones(axis_dim // 2, F32)]), 2)
    cos, sin = jnp.cos(ang), jnp.sin(ang) * sign[None, :]
    return jnp.tile(cos, (1, LANES // HEAD_DIM)), jnp.tile(sin, (1, LANES // HEAD_DIM))


def _dup_heads(w, n_heads):
    d = w.shape[0]
    w = w.reshape(d, n_heads, 1, HEAD_DIM)
    return jnp.broadcast_to(w, (d, n_heads, 2, HEAD_DIM)).reshape(d, n_heads * HEAD_PAIR)


def _token_tile(l, target):
    return min(l, target)


def kernel(x, c, ctx, c_ctx, w_ada, b_ada, norm_pre_mix, norm_post_mix, norm_pre_ffn, norm_post_ffn,
           w_pool, pool_scale, w_qkv, w_o, attn_sinks, w_gate_up, w_down):
    b, l, d = x.shape
    depth = w_ada.shape[0]
    n_mixers = 2
    q_dim = w_o.shape[1]
    kv_heads = (w_qkv.shape[2] - q_dim) // (2 * HEAD_DIM)
    kv_dim = kv_heads * HEAD_DIM

    rows = -(-(b + 1) // BF16_ROWS) * BF16_ROWS
    c_rows = jnp.concatenate([c, c_ctx[None, :], jnp.zeros((rows - b - 1, d), F32)], axis=0)
    ada = _ada_call(c_rows, w_ada, b_ada).reshape(depth, rows, 6, d)
    mod_x = ada[:, :b]
    mod_c = jnp.broadcast_to(ada[:, b:b + 1], (depth, b, 6, d))

    bf = lambda w: w.astype(BF16)
    ffn_w = {}

    def qkv_weights(w):
        w = bf(w)
        return jnp.concatenate([w[:, :q_dim], _dup_heads(w[:, q_dim:q_dim + kv_dim], kv_heads),
                                w[:, q_dim + kv_dim:]], axis=-1)

    tables = _rope_tables(l)

    lc = ctx.shape[1]
    tm_x = _token_tile(l, 512)
    tm_ffn = _token_tile(l, 1024)
    tm_c = _token_tile(b * lc, 512)
    tf = 512

    flat = lambda a: a.reshape(1, b * lc, a.shape[-1])
    unflat = lambda a: a.reshape(b, lc, a.shape[-1])

    for i in range(depth):
        last = i == depth - 1
        j = i // n_mixers
        npre, npost = norm_pre_mix[i][None, :], norm_post_mix[i][None, :]
        if i % n_mixers == 0:
            ps, w_p = pool_scale[j][None, :], bf(w_pool[j])
            x = _pool_call(x, mod_x[i], npre, npost, w_p, ps, tm_x)
            if not last:
                ctx = _pool_call(ctx, mod_c[i], npre, npost, w_p, ps, _token_tile(lc, 512))
        else:
            w_in, w_out = qkv_weights(w_qkv[j]), bf(w_o[j])
            hosted = [i] + ([i + 1] if i + 1 < depth and (i + 1) % n_mixers == 0 else [])
            q, k, v, *cast = _qkv_call(x, mod_x[i], npre, w_in, q_dim, tables, tm_x,
                                       casts=[(w, m) for m in hosted for w in (w_gate_up, w_down)])
            for idx, m in enumerate(hosted):
                ffn_w[m] = (cast[2 * idx][None], cast[2 * idx + 1][None])
            qc, kc, vct = _qkv_call(flat(ctx), mod_c[i][:1], npre, w_in, q_dim, None, tm_c)
            x = _attn_oproj_call(q, k, v, kc, vct, attn_sinks[j], x, mod_x[i], npost, w_out, lc,
                                 2 if (l // Q_BLOCK) % 2 == 0 else 1)
            if not last:
                ac = _ctx_attn_call(qc, kc, vct, attn_sinks[j], lc)
                ctx = unflat(_oproj_call(ac, flat(ctx), mod_c[i][:1], npost, w_out, tm_c))

        npre, npost = norm_pre_ffn[i][None, :], norm_post_ffn[i][None, :]
        if i not in ffn_w:
            ffn_w[i] = (bf(w_gate_up[i])[None], bf(w_down[i])[None])
        w_gu, w_dn = ffn_w.pop(i)
        x = _ffn_call(x, mod_x[i], npre, npost, w_gu, w_dn, 0, tm_ffn, tf)
        if not last:
            ctx = unflat(_ffn_call(flat(ctx), mod_c[i][:1], npre, npost, w_gu, w_dn, 0, _token_tile(b * lc, 1024), tf))
    return x
```

```python
import functools

import jax
import jax.numpy as jnp
from jax import lax
from jax.experimental import pallas as pl
from jax.experimental.pallas import tpu as pltpu

GRID_W = 64
POOL_WINDOWS = (2, 4, 8, 16)
HEAD_DIM = 64
GQA_GROUP = 8
WINDOW = 128
Q_BLOCK = 128
ROPE_BASE = 10000.0
RMS_EPS = 1e-6
NEG_INF = -1e30
LOG2E = 1.4426950408889634

LANES = 128
SUBLANES = 8
BF16_ROWS = 16
VMEM_LIMIT_BYTES = 56 * 1024 * 1024
SWIGLU_VMEM_LIMIT_BYTES = 63 * 1024 * 1024

ROW_SLAB = BF16_ROWS
POOL_HALO = SUBLANES
HEAD_PAIR = 2 * HEAD_DIM

F32 = jnp.float32
BF16 = jnp.bfloat16


def _params(*semantics, vmem_limit_bytes=VMEM_LIMIT_BYTES):
    return pltpu.CompilerParams(dimension_semantics=semantics, vmem_limit_bytes=vmem_limit_bytes)


def _resident(shape):
    nd = len(shape)
    return pl.BlockSpec(shape, lambda *_: (0,) * nd, pipeline_mode=pl.Buffered(1))


def _rms(x, w):
    ms = jnp.mean(x * x, axis=-1, keepdims=True)
    return x * lax.rsqrt(ms + RMS_EPS) * w


def _norm_mod(x, w, shift, scale):
    return _rms(x, w * (1.0 + scale)) + shift


def _cast_side_jobs(casts, tiles, n_steps):
    in_specs, out_specs, out_shape = [], [], []
    for w, layer in casts:
        _, rows, cols = w.shape
        n_blocks = n_steps
        while rows % (n_blocks * BF16_ROWS):
            n_blocks //= 2
        hold, br = n_steps // n_blocks, rows // n_blocks
        in_specs.append(pl.BlockSpec((None, br, cols),
                                     lambda i, t, layer=layer, hold=hold: (layer, (i * tiles + t) // hold, 0)))
        out_specs.append(pl.BlockSpec((br, cols), lambda i, t, hold=hold: ((i * tiles + t) // hold, 0)))
        out_shape.append(jax.ShapeDtypeStruct((rows, cols), BF16))
    return in_specs, out_specs, out_shape


def _run_casts(cast_in, cast_out):
    for src, dst in zip(cast_in, cast_out):
        dst[...] = src[...].astype(BF16)


def _ada_kernel(c_ref, w_ref, b_ref, o_ref):
    s = jax.nn.silu(c_ref[...]).astype(BF16)
    o_ref[0] = jnp.dot(s, w_ref[0].astype(BF16), preferred_element_type=F32) + b_ref[0]


def _ada_call(c_rows, w_ada, b_ada):
    depth, d, n = w_ada.shape
    rows = c_rows.shape[0]
    tn = 1024
    return pl.pallas_call(
        _ada_kernel,
        grid=(depth, n // tn),
        in_specs=[
            pl.BlockSpec((rows, d), lambda i, j: (0, 0)),
            pl.BlockSpec((1, d, tn), lambda i, j: (i, 0, j)),
            pl.BlockSpec((1, 1, tn), lambda i, j: (i, 0, j)),
        ],
        out_specs=pl.BlockSpec((1, rows, tn), lambda i, j: (i, 0, j)),
        out_shape=jax.ShapeDtypeStruct((depth, rows, n), F32),
        compiler_params=_params("parallel", "parallel"),
        name="ada_proj",
    )(c_rows, w_ada, b_ada.reshape(depth, 1, n))


def _ffn_kernel(x_ref, mod_ref, npre_ref, npost_ref, wg_ref, wu_ref, wd_ref, o_ref, h_ref):
    k = pl.program_id(2)
    last = pl.num_programs(2) - 1
    tm = x_ref.shape[1]

    def prologue():
        w = npre_ref[...] * (1.0 + mod_ref[0, 4:5, :])
        shift = mod_ref[0, 3:4, :]
        for i in range(tm // ROW_SLAB):
            rows = pl.ds(i * ROW_SLAB, ROW_SLAB)
            h_ref[rows, :] = (_rms(x_ref[0, rows, :], w) + shift).astype(BF16)

    def chunk(first):
        h = h_ref[...]
        g = jnp.dot(h, wg_ref[...], preferred_element_type=F32)
        u = jnp.dot(h, wu_ref[...], preferred_element_type=F32)
        a = (jax.nn.silu(g) * u).astype(BF16)
        part = jnp.dot(a, wd_ref[...], preferred_element_type=F32)
        if first:
            o_ref[0] = part
        else:
            o_ref[0] += part

    def epilogue():
        w = mod_ref[0, 5:6, :] * npost_ref[...]
        for i in range(tm // ROW_SLAB):
            rows = pl.ds(i * ROW_SLAB, ROW_SLAB)
            o_ref[0, rows, :] = x_ref[0, rows, :] + _rms(o_ref[0, rows, :], w)

    @pl.when(k == 0)
    def _():
        prologue()
        chunk(True)

    @pl.when((k > 0) & (k < last))
    def _():
        chunk(False)

    @pl.when(k == last)
    def _():
        chunk(False)
        epilogue()


def _ffn_call(x, mod, npre, npost, w_gate_up, w_down, layer, tm, tf):
    b, l, d = x.shape
    f = w_down.shape[1]
    nk = f // tf
    return pl.pallas_call(
        _ffn_kernel,
        grid=(b, l // tm, nk),
        in_specs=[
            pl.BlockSpec((1, tm, d), lambda i, t, k: (i, t, 0)),
            pl.BlockSpec((1, 6, d), lambda i, t, k: (i, 0, 0)),
            pl.BlockSpec((1, d), lambda i, t, k: (0, 0)),
            pl.BlockSpec((1, d), lambda i, t, k: (0, 0)),
            pl.BlockSpec((None, d, tf), lambda i, t, k: (layer, 0, k)),
            pl.BlockSpec((None, d, tf), lambda i, t, k: (layer, 0, nk + k)),
            pl.BlockSpec((None, tf, d), lambda i, t, k: (layer, k, 0)),
        ],
        out_specs=pl.BlockSpec((1, tm, d), lambda i, t, k: (i, t, 0)),
        out_shape=jax.ShapeDtypeStruct((b, l, d), F32),
        scratch_shapes=[pltpu.VMEM((tm, d), BF16)],
        compiler_params=_params("parallel", "parallel", "arbitrary", vmem_limit_bytes=SWIGLU_VMEM_LIMIT_BYTES),
        name="swiglu",
    )(x, mod, npre, npost, w_gate_up, w_gate_up, w_down)


def _pool_kernel(x_ref, xp_ref, xn_ref, mod_ref, npre_ref, npost_ref, wp_ref, ps_ref, *rest, seq_len, n_casts):
    o_ref, h_ref = rest[n_casts], rest[-1]
    _run_casts(rest[:n_casts], rest[n_casts + 1:-1])
    t = pl.program_id(1)
    tm = x_ref.shape[1]
    gd = wp_ref.shape[1]
    d = x_ref.shape[2]
    npre = npre_ref[...]
    shift, scale = mod_ref[0, 0:1, :], mod_ref[0, 1:2, :]
    span = tm + 2 * POOL_HALO
    n_groups = len(POOL_WINDOWS)

    def put(rows, value):
        for g in range(n_groups):
            h_ref[g, rows, :] = value[:, g * gd:(g + 1) * gd]

    hp = _norm_mod(xp_ref[0], npre, shift, scale)
    hn = _norm_mod(xn_ref[0], npre, shift, scale)
    put(pl.ds(0, POOL_HALO), jnp.where(t > 0, hp, 0.0))
    w_mod = npre * (1.0 + scale)
    for i in range(tm // ROW_SLAB):
        put(pl.ds(POOL_HALO + i * ROW_SLAB, ROW_SLAB),
            _rms(x_ref[0, pl.ds(i * ROW_SLAB, ROW_SLAB), :], w_mod) + shift)
    put(pl.ds(POOL_HALO + tm, POOL_HALO), jnp.where(t < pl.num_programs(1) - 1, hn, 0.0))

    pos = t * tm + lax.broadcasted_iota(jnp.int32, (tm, LANES), 0)
    ss = jnp.zeros((tm, 1), F32)
    for g, w in enumerate(POOL_WINDOWS):
        cols = pl.ds(g * gd, gd)
        cur = h_ref[g]
        cur = cur + pltpu.roll(cur, 1, 0)
        reach = 1
        while 2 * reach < w:
            cur = pltpu.roll(cur, reach, 0) + pltpu.roll(cur, span - reach, 0)
            reach *= 2
        acc = cur[POOL_HALO:POOL_HALO + tm]
        cnt = jnp.minimum(pos + w // 2, seq_len) - jnp.maximum(pos - w // 2, 0)
        inv_cnt = jnp.concatenate([1.0 / cnt.astype(F32)] * (gd // LANES), axis=1)
        p = acc * inv_cnt - h_ref[g, pl.ds(POOL_HALO, tm), :]
        y = jnp.dot(p.astype(BF16), wp_ref[g], preferred_element_type=F32) * ps_ref[:, cols]
        o_ref[0, :, cols] = y
        ss = ss + jnp.sum(y * y, axis=-1, keepdims=True)

    rstd = lax.rsqrt(ss * (1.0 / d) + RMS_EPS)
    o_ref[0] = x_ref[0] + o_ref[0] * rstd * (mod_ref[0, 2:3, :] * npost_ref[...])


def _pool_call(x, mod, npre, npost, w_pool, pool_scale, tm, casts=()):
    b, l, d = x.shape
    hb = tm // POOL_HALO
    last_hb = l // POOL_HALO - 1
    tiles = l // tm
    cast_in, cast_out, cast_shape = _cast_side_jobs(casts, tiles, b * tiles)
    out = pl.pallas_call(
        functools.partial(_pool_kernel, seq_len=l, n_casts=len(casts)),
        grid=(b, tiles),
        in_specs=[
            pl.BlockSpec((1, tm, d), lambda i, t: (i, t, 0)),
            pl.BlockSpec((1, POOL_HALO, d), lambda i, t: (i, jnp.maximum(t * hb - 1, 0), 0)),
            pl.BlockSpec((1, POOL_HALO, d), lambda i, t: (i, jnp.minimum((t + 1) * hb, last_hb), 0)),
            pl.BlockSpec((1, 6, d), lambda i, t: (i, 0, 0)),
            pl.BlockSpec((1, d), lambda i, t: (0, 0)),
            pl.BlockSpec((1, d), lambda i, t: (0, 0)),
            _resident(w_pool.shape),
            pl.BlockSpec((1, d), lambda i, t: (0, 0)),
        ] + cast_in,
        out_specs=[pl.BlockSpec((1, tm, d), lambda i, t: (i, t, 0))] + cast_out,
        out_shape=[jax.ShapeDtypeStruct((b, l, d), F32)] + cast_shape,
        scratch_shapes=[pltpu.VMEM((w_pool.shape[0], tm + 2 * POOL_HALO, w_pool.shape[1]), F32)],
        compiler_params=_params("arbitrary", "arbitrary"),
        name="pool_mixer",
    )(x, x, x, mod, npre, npost, w_pool, pool_scale, *[w for w, _ in casts])
    return out if casts else out[0]


def _rope(x, cos, sin_signed, low_half):
    out = []
    for j in range(x.shape[1] // LANES):
        c = x[:, j * LANES:(j + 1) * LANES]
        rot = jnp.where(low_half, pltpu.roll(c, LANES - HEAD_DIM // 4, 1), pltpu.roll(c, HEAD_DIM // 4, 1))
        out.append(c * cos + rot * sin_signed)
    return jnp.concatenate(out, axis=1)


def _qkv_kernel(*refs, rope, q_dim, k_dim, n_casts):
    n_in = (6 if rope else 4) + n_casts
    cast_in, cast_out = refs[n_in - n_casts:n_in], refs[n_in + 3:]
    if rope:
        x_ref, mod_ref, npre_ref, w_ref, cos_ref, sin_ref = refs[:6]
    else:
        x_ref, mod_ref, npre_ref, w_ref = refs[:4]
    q_ref, k_ref, vt_ref = refs[n_in:n_in + 3]
    _run_casts(cast_in, cast_out)
    h = _norm_mod(x_ref[0], npre_ref[...], mod_ref[0, 0:1, :], mod_ref[0, 1:2, :]).astype(BF16)
    qkv = jnp.dot(h, w_ref[...], preferred_element_type=F32)
    q = qkv[:, :q_dim]
    k = qkv[:, q_dim:q_dim + k_dim]
    v = qkv[:, q_dim + k_dim:]
    if rope:
        cos, sin_signed = cos_ref[...], sin_ref[...]
        lane = lax.broadcasted_iota(jnp.int32, cos.shape, 1)
        low_half = (lane % (HEAD_DIM // 2)) < (HEAD_DIM // 4)
        q = _rope(q, cos, sin_signed, low_half)
        k = _rope(k, cos, sin_signed, low_half)
    q_ref[0] = (q * (HEAD_DIM ** -0.5 * LOG2E)).astype(BF16)
    k_ref[0] = k.astype(BF16)
    vt_ref[0] = v.T.astype(BF16)


def _qkv_call(x, mod, npre, w_qkv_dup, q_dim, rope_tables, tm, casts=()):
    b, l, d = x.shape
    n = w_qkv_dup.shape[1]
    v_dim = (n - q_dim) // 3
    k_dim = 2 * v_dim
    rope = rope_tables is not None
    in_specs = [
        pl.BlockSpec((1, tm, d), lambda i, t: (i, t, 0)),
        pl.BlockSpec((1, 6, d), lambda i, t: (i, 0, 0)),
        pl.BlockSpec((1, d), lambda i, t: (0, 0)),
        _resident(w_qkv_dup.shape),
    ]
    args = [x, mod, npre, w_qkv_dup]
    if rope:
        in_specs += [pl.BlockSpec((tm, LANES), lambda i, t: (t, 0))] * 2
        args += list(rope_tables)
    out_specs = [
        pl.BlockSpec((1, tm, q_dim), lambda i, t: (i, t, 0)),
        pl.BlockSpec((1, tm, k_dim), lambda i, t: (i, t, 0)),
        pl.BlockSpec((1, v_dim, tm), lambda i, t: (i, 0, t)),
    ]
    out_shape = [
        jax.ShapeDtypeStruct((b, l, q_dim), BF16),
        jax.ShapeDtypeStruct((b, l, k_dim), BF16),
        jax.ShapeDtypeStruct((b, v_dim, l), BF16),
    ]
    tiles = l // tm
    cast_in, cast_out, cast_shape = _cast_side_jobs(casts, tiles, b * tiles)
    in_specs, out_specs, out_shape = in_specs + cast_in, out_specs + cast_out, out_shape + cast_shape
    args += [w for w, _ in casts]
    return pl.pallas_call(
        functools.partial(_qkv_kernel, rope=rope, q_dim=q_dim, k_dim=k_dim, n_casts=len(casts)),
        grid=(b, tiles),
        in_specs=in_specs,
        out_specs=out_specs,
        out_shape=out_shape,
        compiler_params=_params("arbitrary", "arbitrary"),
        name="qkv_rope" if rope else "qkv_ctx",
    )(*args)


def _attend_scores(q_ref, sink_ref, h, k, row0, nq):
    low = lax.broadcasted_iota(jnp.int32, (nq, HEAD_PAIR), 1) < HEAD_DIM
    zero = jnp.zeros((), BF16)
    base = h * GQA_GROUP * HEAD_DIM
    q_rows, sink_cols = [], []
    for g in range(GQA_GROUP):
        pair = q_ref[0, pl.ds(row0, nq), pl.ds(base + (g // 2) * HEAD_PAIR, HEAD_PAIR)]
        q_rows.append(jnp.where(low if g % 2 == 0 else ~low, pair, zero))
        sink_cols.append(jnp.full((1, nq), sink_ref[h * GQA_GROUP + g] * LOG2E, F32))
    qg = jnp.concatenate(q_rows, axis=0)
    snk = jnp.concatenate(sink_cols, axis=1)
    s = lax.dot_general(k, qg, (((1,), (1,)), ((), ())), preferred_element_type=F32)
    return s, snk


def _attend_softmax(s, snk, masks):
    blocks, row = [], 0
    for first, bias in masks:
        if first > row:
            blocks.append(s[row:first])
        row = first + bias.shape[0]
        blocks.append(s[first:row] + bias)
    if masks:
        if row < s.shape[0]:
            blocks.append(s[row:])
        s = jnp.concatenate(blocks, axis=0)
    m = jnp.maximum(snk, jnp.max(s, axis=0, keepdims=True))
    return jnp.exp2(s - m).astype(BF16), jnp.exp2(snk - m)


def _attend_values(o_ref, h, p, sink_p, vt, row0, nq):
    base = h * GQA_GROUP * HEAD_DIM
    vt_ones = jnp.concatenate([vt, jnp.ones((BF16_ROWS, vt.shape[1]), BF16)], axis=0)
    acc = jnp.dot(vt_ones, p, preferred_element_type=F32)
    den = sink_p + acc[HEAD_DIM:HEAD_DIM + 1]
    acc = acc[:HEAD_DIM] * (1.0 / den)
    for j in range(GQA_GROUP // 2):
        pair_t = jnp.concatenate([acc[:, (2 * j) * nq:(2 * j + 1) * nq], acc[:, (2 * j + 1) * nq:(2 * j + 2) * nq]],
                                 axis=0)
        o_ref[0, pl.ds(row0, nq), pl.ds(base + j * HEAD_PAIR, HEAD_PAIR)] = pair_t.T.astype(o_ref.dtype)


def _attn_oproj_kernel(sink_ref, q_ref, kp_ref, kc_ref, kn_ref, vp_ref, vc_ref, vn_ref, kx_ref, vx_ref,
                       x_ref, mod_ref, npost_ref, w_ref, o_ref, a_scr, y_scr, *, steps_per_seq):
    n = pl.program_id(0)
    n_tiles = pl.num_programs(0) - 1
    slot = n % 2
    nq = Q_BLOCK
    rows = q_ref.shape[1]
    n_sub = rows // nq
    piece = 2 * LANES

    def attention(fillers):
        t = n % steps_per_seq
        shape = (nq, GQA_GROUP * nq)
        key = lax.broadcasted_iota(jnp.int32, shape, 0)
        qry = lax.broadcasted_iota(jnp.int32, shape, 1) % nq
        inner_prev = jnp.where(key >= qry, 0.0, NEG_INF)
        inner_next = jnp.where(key <= qry, 0.0, NEG_INF)
        first_prev = jnp.where((key >= qry) & (t > 0), 0.0, NEG_INF)
        last_next = jnp.where((key <= qry) & (t < steps_per_seq - 1), 0.0, NEG_INF)
        k_loc = jnp.concatenate([kp_ref[0], kc_ref[0], kn_ref[0]], axis=0)
        vt_loc = jnp.concatenate([vp_ref[0], vc_ref[0], vn_ref[0]], axis=1)
        n_kv = vt_loc.shape[0] // HEAD_DIM
        a_out = a_scr.at[pl.ds(slot, 1)]

        units = []
        for u in range(n_sub):
            k_all = jnp.concatenate([k_loc[u * nq:(u + 3) * nq], kx_ref[0]], axis=0)
            vt_all = jnp.concatenate([vt_loc[:, u * nq:(u + 3) * nq], vx_ref[0]], axis=1)
            masks = [(0, first_prev if u == 0 else inner_prev),
                     (2 * nq, last_next if u == n_sub - 1 else inner_next)]
            for h in range(n_kv):
                units.append((u * nq, h, k_all[:, h * HEAD_PAIR:(h + 1) * HEAD_PAIR],
                              vt_all[h * HEAD_DIM:(h + 1) * HEAD_DIM, :], masks))
        scores, probs = {}, {}
        for i in range(len(units) + 2):
            if i < len(units):
                row0, h, k, _, _ = units[i]
                scores[i] = _attend_scores(q_ref, sink_ref, h, k, row0, nq)
            if 0 <= i - 2 < len(units):
                row0, h, _, vt, _ = units[i - 2]
                _attend_values(a_out, h, *probs.pop(i - 2), vt, row0, nq)
            if 0 <= i - 1 < len(units):
                probs[i - 1] = _attend_softmax(*scores.pop(i - 1), units[i - 1][4])
            if i < len(fillers):
                fillers[i]()
        for f in fillers[len(units) + 2:]:
            f()

    def project(j):
        def run():
            cols = pl.ds(j * piece, piece)
            y_scr[:, cols] = jnp.dot(a_scr[1 - slot], w_ref[:, cols], preferred_element_type=F32)
        return run

    def finish():
        w = mod_ref[0, 2:3, :] * npost_ref[...]
        for i in range(rows // ROW_SLAB):
            sl = pl.ds(i * ROW_SLAB, ROW_SLAB)
            o_ref[0, sl, :] = x_ref[0, sl, :] + _rms(y_scr[sl, :], w)

    pieces = [project(j) for j in range(w_ref.shape[1] // piece)]

    @pl.when(n == 0)
    def _():
        attention([])

    @pl.when((n > 0) & (n < n_tiles))
    def _():
        attention(pieces)
        finish()

    @pl.when(n == n_tiles)
    def _():
        for f in pieces:
            f()
        finish()


def _attn_oproj_call(q, k, vt, kx, vxt, sinks, x, mod, npost, w_o, n_ctx, n_sub):
    b, l, qd = q.shape
    d = x.shape[2]
    kd, vd = k.shape[2], vt.shape[1]
    nb = l // Q_BLOCK
    rows = n_sub * Q_BLOCK
    spb = nb // n_sub
    n_tiles = b * spb
    cur = lambda n: jnp.minimum(n, n_tiles - 1)
    done = lambda n: jnp.maximum(n - 1, 0)
    prev = lambda t: jnp.maximum(t * n_sub - 1, 0)
    nxt = lambda t: jnp.minimum((t + 1) * n_sub, nb - 1)
    k_halo = lambda f: pl.BlockSpec((1, Q_BLOCK, kd), lambda n: (cur(n) // spb, f(cur(n) % spb), 0))
    v_halo = lambda f: pl.BlockSpec((1, vd, Q_BLOCK), lambda n: (cur(n) // spb, 0, f(cur(n) % spb)))
    return pl.pallas_call(
        functools.partial(_attn_oproj_kernel, steps_per_seq=spb),
        grid=(n_tiles + 1,),
        in_specs=[
            pl.BlockSpec(memory_space=pltpu.SMEM),
            pl.BlockSpec((1, rows, qd), lambda n: (cur(n) // spb, cur(n) % spb, 0)),
            k_halo(prev), pl.BlockSpec((1, rows, kd), lambda n: (cur(n) // spb, cur(n) % spb, 0)), k_halo(nxt),
            v_halo(prev), pl.BlockSpec((1, vd, rows), lambda n: (cur(n) // spb, 0, cur(n) % spb)), v_halo(nxt),
            pl.BlockSpec((1, n_ctx, kd), lambda n: (0, cur(n) // spb, 0)),
            pl.BlockSpec((1, vd, n_ctx), lambda n: (0, 0, cur(n) // spb)),
            pl.BlockSpec((1, rows, d), lambda n: (done(n) // spb, done(n) % spb, 0)),
            pl.BlockSpec((1, 6, d), lambda n: (done(n) // spb, 0, 0)),
            pl.BlockSpec((1, d), lambda n: (0, 0)),
            _resident(w_o.shape),
        ],
        out_specs=pl.BlockSpec((1, rows, d), lambda n: (done(n) // spb, done(n) % spb, 0)),
        out_shape=jax.ShapeDtypeStruct(x.shape, F32),
        scratch_shapes=[pltpu.VMEM((2, rows, qd), BF16), pltpu.VMEM((rows, d), F32)],
        compiler_params=_params("arbitrary"),
        name="window_attn_proj",
    )(sinks, q, k, k, k, vt, vt, vt, kx, vxt, x, mod, npost, w_o)


def _ctx_attn_kernel(sink_ref, q_ref, k_ref, vt_ref, o_ref):
    nq = q_ref.shape[1]
    for h in range(vt_ref.shape[1] // HEAD_DIM):
        sc = _attend_scores(q_ref, sink_ref, h, k_ref[0, :, h * HEAD_PAIR:(h + 1) * HEAD_PAIR], 0, nq)
        _attend_values(o_ref, h, *_attend_softmax(*sc, []), vt_ref[0, h * HEAD_DIM:(h + 1) * HEAD_DIM, :], 0, nq)


def _ctx_attn_call(q, k, vt, sinks, n_ctx):
    qd, kd, vd = q.shape[2], k.shape[2], vt.shape[1]
    return pl.pallas_call(
        _ctx_attn_kernel,
        grid=(q.shape[1] // n_ctx,),
        in_specs=[
            pl.BlockSpec(memory_space=pltpu.SMEM),
            pl.BlockSpec((1, n_ctx, qd), lambda i: (0, i, 0)),
            pl.BlockSpec((1, n_ctx, kd), lambda i: (0, i, 0)),
            pl.BlockSpec((1, vd, n_ctx), lambda i: (0, 0, i)),
        ],
        out_specs=pl.BlockSpec((1, n_ctx, qd), lambda i: (0, i, 0)),
        out_shape=jax.ShapeDtypeStruct(q.shape, BF16),
        compiler_params=_params("parallel"),
        name="ctx_attn",
    )(sinks, q, k, vt)


def _oproj_kernel(a_ref, x_ref, mod_ref, npost_ref, w_ref, o_ref):
    y = jnp.dot(a_ref[0], w_ref[...], preferred_element_type=F32)
    o_ref[0] = x_ref[0] + _rms(y, mod_ref[0, 2:3, :] * npost_ref[...])


def _oproj_call(a, x, mod, npost, w_o, tm):
    b, l, d = x.shape
    ad = a.shape[2]
    return pl.pallas_call(
        _oproj_kernel,
        grid=(b, l // tm),
        in_specs=[
            pl.BlockSpec((1, tm, ad), lambda i, t: (i, t, 0)),
            pl.BlockSpec((1, tm, d), lambda i, t: (i, t, 0)),
            pl.BlockSpec((1, 6, d), lambda i, t: (i, 0, 0)),
            pl.BlockSpec((1, d), lambda i, t: (0, 0)),
            _resident(w_o.shape),
        ],
        out_specs=pl.BlockSpec((1, tm, d), lambda i, t: (i, t, 0)),
        out_shape=jax.ShapeDtypeStruct((b, l, d), F32),
        compiler_params=_params("parallel", "parallel"),
        name="attn_out_proj",
    )(a, x, mod, npost, w_o)


def _rope_tables(l):
    axis_dim = HEAD_DIM // 2
    rows_n = l // GRID_W
    row = jnp.repeat(jnp.arange(rows_n), GRID_W).astype(F32)
    col = jnp.tile(jnp.arange(GRID_W), rows_n).astype(F32)
    inv = 1.0 / (ROPE_BASE ** (jnp.arange(0, axis_dim, 2, dtype=F32) / axis_dim))
    ang_r = row[:, None] * inv[None, :]
    ang_c = col[:, None] * inv[None, :]
    ang = jnp.concatenate([ang_r, ang_r, ang_c, ang_c], axis=-1)
    sign = jnp.tile(jnp.concatenate([-jnp.ones(axis_dim // 2, F32), jnp.ones(axis_dim // 2, F32)]), 2)
    cos, sin = jnp.cos(ang), jnp.sin(ang) * sign[None, :]
    return jnp.tile(cos, (1, LANES // HEAD_DIM)), jnp.tile(sin, (1, LANES // HEAD_DIM))


def _dup_heads(w, n_heads):
    d = w.shape[0]
    w = w.reshape(d, n_heads, 1, HEAD_DIM)
    return jnp.broadcast_to(w, (d, n_heads, 2, HEAD_DIM)).reshape(d, n_heads * HEAD_PAIR)


def _token_tile(l, target):
    return min(l, target)


def kernel(x, c, ctx, c_ctx, w_ada, b_ada, norm_pre_mix, norm_post_mix, norm_pre_ffn, norm_post_ffn,
           w_pool, pool_scale, w_qkv, w_o, attn_sinks, w_gate_up, w_down):
    b, l, d = x.shape
    depth = w_ada.shape[0]
    n_mixers = 2
    q_dim = w_o.shape[1]
    kv_heads = (w_qkv.shape[2] - q_dim) // (2 * HEAD_DIM)
    kv_dim = kv_heads * HEAD_DIM

    rows = -(-(b + 1) // BF16_ROWS) * BF16_ROWS
    c_rows = jnp.concatenate([c, c_ctx[None, :], jnp.zeros((rows - b - 1, d), F32)], axis=0)
    ada = _ada_call(c_rows, w_ada, b_ada).reshape(depth, rows, 6, d)
    mod_x = ada[:, :b]
    mod_c = jnp.broadcast_to(ada[:, b:b + 1], (depth, b, 6, d))

    bf = lambda w: w.astype(BF16)
    ffn_w = {}

    def qkv_weights(w):
        w = bf(w)
        return jnp.concatenate([w[:, :q_dim], _dup_heads(w[:, q_dim:q_dim + kv_dim], kv_heads),
                                w[:, q_dim + kv_dim:]], axis=-1)

    tables = _rope_tables(l)

    lc = ctx.shape[1]
    tm_x = _token_tile(l, 512)
    tm_ffn = _token_tile(l, 1024)
    tm_c = _token_tile(b * lc, 512)
    tf = 512

    flat = lambda a: a.reshape(1, b * lc, a.shape[-1])
    unflat = lambda a: a.reshape(b, lc, a.shape[-1])

    for i in range(depth):
        last = i == depth - 1
        j = i // n_mixers
        npre, npost = norm_pre_mix[i][None, :], norm_post_mix[i][None, :]
        if i % n_mixers == 0:
            ps, w_p = pool_scale[j][None, :], bf(w_pool[j])
            if i in ffn_w:
                x = _pool_call(x, mod_x[i], npre, npost, w_p, ps, tm_x)
            else:
                x, w_gu, w_dn = _pool_call(x, mod_x[i], npre, npost, w_p, ps, tm_x,
                                           casts=[(w_gate_up, i), (w_down, i)])
                ffn_w[i] = (w_gu[None], w_dn[None])
            if not last:
                ctx = _pool_call(ctx, mod_c[i], npre, npost, w_p, ps, _token_tile(lc, 512))
        else:
            w_in, w_out = qkv_weights(w_qkv[j]), bf(w_o[j])
            hosted = [i] + ([i + 1] if i + 1 < depth and (i + 1) % n_mixers == 0 else [])
            q, k, v, *cast = _qkv_call(x, mod_x[i], npre, w_in, q_dim, tables, tm_x,
                                       casts=[(w, m) for m in hosted for w in (w_gate_up, w_down)])
            for idx, m in enumerate(hosted):
                ffn_w[m] = (cast[2 * idx][None], cast[2 * idx + 1][None])
            qc, kc, vct = _qkv_call(flat(ctx), mod_c[i][:1], npre, w_in, q_dim, None, tm_c)
            x = _attn_oproj_call(q, k, v, kc, vct, attn_sinks[j], x, mod_x[i], npost, w_out, lc,
                                 2 if (l // Q_BLOCK) % 2 == 0 else 1)
            if not last:
                ac = _ctx_attn_call(qc, kc, vct, attn_sinks[j], lc)
                ctx = unflat(_oproj_call(ac, flat(ctx), mod_c[i][:1], npost, w_out, tm_c))

        npre, npost = norm_pre_ffn[i][None, :], norm_post_ffn[i][None, :]
        w_gu, w_dn = ffn_w.pop(i)
        x = _ffn_call(x, mod_x[i], npre, npost, w_gu, w_dn, 0, tm_ffn, tf)
        if not last:
            ctx = unflat(_ffn_call(flat(ctx), mod_c[i][:1], npre, npost, w_gu, w_dn, 0, _token_tile(b * lc, 1024), tf))
    return x
```

```python
import functools

import jax
import jax.numpy as jnp
from jax import lax
from jax.experimental import pallas as pl
from jax.experimental.pallas import tpu as pltpu

GRID_W = 64
POOL_WINDOWS = (2, 4, 8, 16)
HEAD_DIM = 64
GQA_GROUP = 8
WINDOW = 128
Q_BLOCK = 128
ROPE_BASE = 10000.0
RMS_EPS = 1e-6
NEG_INF = -1e30
LOG2E = 1.4426950408889634

LANES = 128
SUBLANES = 8
BF16_ROWS = 16
VMEM_LIMIT_BYTES = 56 * 1024 * 1024
SWIGLU_VMEM_LIMIT_BYTES = 63 * 1024 * 1024

ROW_SLAB = BF16_ROWS
POOL_HALO = SUBLANES
HEAD_PAIR = 2 * HEAD_DIM

F32 = jnp.float32
BF16 = jnp.bfloat16


def _params(*semantics, vmem_limit_bytes=VMEM_LIMIT_BYTES):
    return pltpu.CompilerParams(dimension_semantics=semantics, vmem_limit_bytes=vmem_limit_bytes)


def _resident(shape):
    nd = len(shape)
    return pl.BlockSpec(shape, lambda *_: (0,) * nd, pipeline_mode=pl.Buffered(1))


def _rms(x, w):
    ms = jnp.mean(x * x, axis=-1, keepdims=True)
    return x * lax.rsqrt(ms + RMS_EPS) * w


def _norm_mod(x, w, shift, scale):
    return _rms(x, w * (1.0 + scale)) + shift


def _cast_side_jobs(casts, tiles, n_steps):
    in_specs, out_specs, out_shape = [], [], []
    for w, layer, chunk in casts:
        _, rows, cols = w.shape
        n_blocks = n_steps
        while rows % (n_blocks * BF16_ROWS):
            n_blocks //= 2
        hold, br = n_steps // n_blocks, rows // n_blocks
        in_specs.append(pl.BlockSpec((None, br, cols),
                                     lambda i, t, layer=layer, hold=hold: (layer, (i * tiles + t) // hold, 0)))
        if chunk is None:
            out_specs.append(pl.BlockSpec((br, cols), lambda i, t, hold=hold: ((i * tiles + t) // hold, 0)))
            out_shape.append(jax.ShapeDtypeStruct((rows, cols), BF16))
        else:
            out_specs.append(pl.BlockSpec((cols // chunk, br, chunk),
                                          lambda i, t, hold=hold: (0, (i * tiles + t) // hold, 0)))
            out_shape.append(jax.ShapeDtypeStruct((cols // chunk, rows, chunk), BF16))
    return in_specs, out_specs, out_shape


def _run_casts(cast_in, cast_out):
    for src, dst in zip(cast_in, cast_out):
        if len(dst.shape) == 2:
            dst[...] = src[...].astype(BF16)
        else:
            chunk = dst.shape[2]
            for c in range(dst.shape[0]):
                dst[c] = src[:, c * chunk:(c + 1) * chunk].astype(BF16)


def _ada_kernel(c_ref, w_ref, b_ref, o_ref):
    s = jax.nn.silu(c_ref[...]).astype(BF16)
    o_ref[0] = jnp.dot(s, w_ref[0].astype(BF16), preferred_element_type=F32) + b_ref[0]


def _ada_call(c_rows, w_ada, b_ada):
    depth, d, n = w_ada.shape
    rows = c_rows.shape[0]
    tn = 1024
    return pl.pallas_call(
        _ada_kernel,
        grid=(depth, n // tn),
        in_specs=[
            pl.BlockSpec((rows, d), lambda i, j: (0, 0)),
            pl.BlockSpec((1, d, tn), lambda i, j: (i, 0, j)),
            pl.BlockSpec((1, 1, tn), lambda i, j: (i, 0, j)),
        ],
        out_specs=pl.BlockSpec((1, rows, tn), lambda i, j: (i, 0, j)),
        out_shape=jax.ShapeDtypeStruct((depth, rows, n), F32),
        compiler_params=_params("parallel", "parallel"),
        name="ada_proj",
    )(c_rows, w_ada, b_ada.reshape(depth, 1, n))


def _ffn_kernel(x_ref, mod_ref, npre_ref, npost_ref, wg_ref, wu_ref, wd_ref, o_ref, h_ref):
    k = pl.program_id(2)
    last = pl.num_programs(2) - 1
    tm = x_ref.shape[1]

    def prologue():
        w = npre_ref[...] * (1.0 + mod_ref[0, 4:5, :])
        shift = mod_ref[0, 3:4, :]
        for i in range(tm // ROW_SLAB):
            rows = pl.ds(i * ROW_SLAB, ROW_SLAB)
            h_ref[rows, :] = (_rms(x_ref[0, rows, :], w) + shift).astype(BF16)

    def chunk(first):
        h = h_ref[...]
        g = jnp.dot(h, wg_ref[...], preferred_element_type=F32)
        u = jnp.dot(h, wu_ref[...], preferred_element_type=F32)
        a = (jax.nn.silu(g) * u).astype(BF16)
        part = jnp.dot(a, wd_ref[...], preferred_element_type=F32)
        if first:
            o_ref[0] = part
        else:
            o_ref[0] += part

    def epilogue():
        w = mod_ref[0, 5:6, :] * npost_ref[...]
        for i in range(tm // ROW_SLAB):
            rows = pl.ds(i * ROW_SLAB, ROW_SLAB)
            o_ref[0, rows, :] = x_ref[0, rows, :] + _rms(o_ref[0, rows, :], w)

    @pl.when(k == 0)
    def _():
        prologue()
        chunk(True)

    @pl.when((k > 0) & (k < last))
    def _():
        chunk(False)

    @pl.when(k == last)
    def _():
        chunk(False)
        epilogue()


def _ffn_call(x, mod, npre, npost, w_gate_up, w_down, tm):
    b, l, d = x.shape
    f = w_down.shape[0]
    tf = w_gate_up.shape[2]
    nk = f // tf
    return pl.pallas_call(
        _ffn_kernel,
        grid=(b, l // tm, nk),
        in_specs=[
            pl.BlockSpec((1, tm, d), lambda i, t, k: (i, t, 0)),
            pl.BlockSpec((1, 6, d), lambda i, t, k: (i, 0, 0)),
            pl.BlockSpec((1, d), lambda i, t, k: (0, 0)),
            pl.BlockSpec((1, d), lambda i, t, k: (0, 0)),
            pl.BlockSpec((None, d, tf), lambda i, t, k: (k, 0, 0)),
            pl.BlockSpec((None, d, tf), lambda i, t, k: (nk + k, 0, 0)),
            pl.BlockSpec((tf, d), lambda i, t, k: (k, 0)),
        ],
        out_specs=pl.BlockSpec((1, tm, d), lambda i, t, k: (i, t, 0)),
        out_shape=jax.ShapeDtypeStruct((b, l, d), F32),
        scratch_shapes=[pltpu.VMEM((tm, d), BF16)],
        compiler_params=_params("parallel", "parallel", "arbitrary", vmem_limit_bytes=SWIGLU_VMEM_LIMIT_BYTES),
        name="swiglu",
    )(x, mod, npre, npost, w_gate_up, w_gate_up, w_down)


def _pool_kernel(x_ref, xp_ref, xn_ref, mod_ref, npre_ref, npost_ref, wp_ref, ps_ref, *rest, seq_len, n_casts):
    o_ref, h_ref = rest[n_casts], rest[-1]
    _run_casts(rest[:n_casts], rest[n_casts + 1:-1])
    t = pl.program_id(1)
    tm = x_ref.shape[1]
    gd = wp_ref.shape[1]
    d = x_ref.shape[2]
    npre = npre_ref[...]
    shift, scale = mod_ref[0, 0:1, :], mod_ref[0, 1:2, :]
    span = tm + 2 * POOL_HALO
    n_groups = len(POOL_WINDOWS)

    def put(rows, value):
        for g in range(n_groups):
            h_ref[g, rows, :] = value[:, g * gd:(g + 1) * gd]

    hp = _norm_mod(xp_ref[0], npre, shift, scale)
    hn = _norm_mod(xn_ref[0], npre, shift, scale)
    put(pl.ds(0, POOL_HALO), jnp.where(t > 0, hp, 0.0))
    w_mod = npre * (1.0 + scale)
    for i in range(tm // ROW_SLAB):
        put(pl.ds(POOL_HALO + i * ROW_SLAB, ROW_SLAB),
            _rms(x_ref[0, pl.ds(i * ROW_SLAB, ROW_SLAB), :], w_mod) + shift)
    put(pl.ds(POOL_HALO + tm, POOL_HALO), jnp.where(t < pl.num_programs(1) - 1, hn, 0.0))

    pos = t * tm + lax.broadcasted_iota(jnp.int32, (tm, LANES), 0)
    ss = jnp.zeros((tm, 1), F32)
    for g, w in enumerate(POOL_WINDOWS):
        cols = pl.ds(g * gd, gd)
        cur = h_ref[g]
        cur = cur + pltpu.roll(cur, 1, 0)
        reach = 1
        while 2 * reach < w:
            cur = pltpu.roll(cur, reach, 0) + pltpu.roll(cur, span - reach, 0)
            reach *= 2
        acc = cur[POOL_HALO:POOL_HALO + tm]
        cnt = jnp.minimum(pos + w // 2, seq_len) - jnp.maximum(pos - w // 2, 0)
        inv_cnt = jnp.concatenate([1.0 / cnt.astype(F32)] * (gd // LANES), axis=1)
        p = acc * inv_cnt - h_ref[g, pl.ds(POOL_HALO, tm), :]
        y = jnp.dot(p.astype(BF16), wp_ref[g], preferred_element_type=F32) * ps_ref[:, cols]
        o_ref[0, :, cols] = y
        ss = ss + jnp.sum(y * y, axis=-1, keepdims=True)

    rstd = lax.rsqrt(ss * (1.0 / d) + RMS_EPS)
    o_ref[0] = x_ref[0] + o_ref[0] * rstd * (mod_ref[0, 2:3, :] * npost_ref[...])


def _pool_call(x, mod, npre, npost, w_pool, pool_scale, tm, casts=()):
    b, l, d = x.shape
    hb = tm // POOL_HALO
    last_hb = l // POOL_HALO - 1
    tiles = l // tm
    cast_in, cast_out, cast_shape = _cast_side_jobs(casts, tiles, b * tiles)
    out = pl.pallas_call(
        functools.partial(_pool_kernel, seq_len=l, n_casts=len(casts)),
        grid=(b, tiles),
        in_specs=[
            pl.BlockSpec((1, tm, d), lambda i, t: (i, t, 0)),
            pl.BlockSpec((1, POOL_HALO, d), lambda i, t: (i, jnp.maximum(t * hb - 1, 0), 0)),
            pl.BlockSpec((1, POOL_HALO, d), lambda i, t: (i, jnp.minimum((t + 1) * hb, last_hb), 0)),
            pl.BlockSpec((1, 6, d), lambda i, t: (i, 0, 0)),
            pl.BlockSpec((1, d), lambda i, t: (0, 0)),
            pl.BlockSpec((1, d), lambda i, t: (0, 0)),
            _resident(w_pool.shape),
            pl.BlockSpec((1, d), lambda i, t: (0, 0)),
        ] + cast_in,
        out_specs=[pl.BlockSpec((1, tm, d), lambda i, t: (i, t, 0))] + cast_out,
        out_shape=[jax.ShapeDtypeStruct((b, l, d), F32)] + cast_shape,
        scratch_shapes=[pltpu.VMEM((w_pool.shape[0], tm + 2 * POOL_HALO, w_pool.shape[1]), F32)],
        compiler_params=_params("arbitrary", "arbitrary"),
        name="pool_mixer",
    )(x, x, x, mod, npre, npost, w_pool, pool_scale, *[job[0] for job in casts])
    return out if casts else out[0]


def _rope(x, cos, sin_signed, low_half):
    out = []
    for j in range(x.shape[1] // LANES):
        c = x[:, j * LANES:(j + 1) * LANES]
        rot = jnp.where(low_half, pltpu.roll(c, LANES - HEAD_DIM // 4, 1), pltpu.roll(c, HEAD_DIM // 4, 1))
        out.append(c * cos + rot * sin_signed)
    return jnp.concatenate(out, axis=1)


def _qkv_kernel(*refs, rope, q_dim, k_dim, n_casts):
    n_in = (6 if rope else 4) + n_casts
    cast_in, cast_out = refs[n_in - n_casts:n_in], refs[n_in + 3:]
    if rope:
        x_ref, mod_ref, npre_ref, w_ref, cos_ref, sin_ref = refs[:6]
    else:
        x_ref, mod_ref, npre_ref, w_ref = refs[:4]
    q_ref, k_ref, vt_ref = refs[n_in:n_in + 3]
    _run_casts(cast_in, cast_out)
    h = _norm_mod(x_ref[0], npre_ref[...], mod_ref[0, 0:1, :], mod_ref[0, 1:2, :]).astype(BF16)
    qkv = jnp.dot(h, w_ref[...], preferred_element_type=F32)
    q = qkv[:, :q_dim]
    k = qkv[:, q_dim:q_dim + k_dim]
    v = qkv[:, q_dim + k_dim:]
    if rope:
        cos, sin_signed = cos_ref[...], sin_ref[...]
        lane = lax.broadcasted_iota(jnp.int32, cos.shape, 1)
        low_half = (lane % (HEAD_DIM // 2)) < (HEAD_DIM // 4)
        q = _rope(q, cos, sin_signed, low_half)
        k = _rope(k, cos, sin_signed, low_half)
    q_ref[0] = (q * (HEAD_DIM ** -0.5 * LOG2E)).astype(BF16)
    k_ref[0] = k.astype(BF16)
    vt_ref[0] = v.T.astype(BF16)


def _qkv_call(x, mod, npre, w_qkv_dup, q_dim, rope_tables, tm, casts=()):
    b, l, d = x.shape
    n = w_qkv_dup.shape[1]
    v_dim = (n - q_dim) // 3
    k_dim = 2 * v_dim
    rope = rope_tables is not None
    in_specs = [
        pl.BlockSpec((1, tm, d), lambda i, t: (i, t, 0)),
        pl.BlockSpec((1, 6, d), lambda i, t: (i, 0, 0)),
        pl.BlockSpec((1, d), lambda i, t: (0, 0)),
        _resident(w_qkv_dup.shape),
    ]
    args = [x, mod, npre, w_qkv_dup]
    if rope:
        in_specs += [pl.BlockSpec((tm, LANES), lambda i, t: (t, 0))] * 2
        args += list(rope_tables)
    out_specs = [
        pl.BlockSpec((1, tm, q_dim), lambda i, t: (i, t, 0)),
        pl.BlockSpec((1, tm, k_dim), lambda i, t: (i, t, 0)),
        pl.BlockSpec((1, v_dim, tm), lambda i, t: (i, 0, t)),
    ]
    out_shape = [
        jax.ShapeDtypeStruct((b, l, q_dim), BF16),
        jax.ShapeDtypeStruct((b, l, k_dim), BF16),
        jax.ShapeDtypeStruct((b, v_dim, l), BF16),
    ]
    tiles = l // tm
    cast_in, cast_out, cast_shape = _cast_side_jobs(casts, tiles, b * tiles)
    in_specs, out_specs, out_shape = in_specs + cast_in, out_specs + cast_out, out_shape + cast_shape
    args += [job[0] for job in casts]
    return pl.pallas_call(
        functools.partial(_qkv_kernel, rope=rope, q_dim=q_dim, k_dim=k_dim, n_casts=len(casts)),
        grid=(b, tiles),
        in_specs=in_specs,
        out_specs=out_specs,
        out_shape=out_shape,
        compiler_params=_params("arbitrary", "arbitrary"),
        name="qkv_rope" if rope else "qkv_ctx",
    )(*args)


def _attend_scores(q_ref, sink_ref, h, k, row0, nq):
    low = lax.broadcasted_iota(jnp.int32, (nq, HEAD_PAIR), 1) < HEAD_DIM
    zero = jnp.zeros((), BF16)
    base = h * GQA_GROUP * HEAD_DIM
    q_rows, sink_cols = [], []
    for g in range(GQA_GROUP):
        pair = q_ref[0, pl.ds(row0, nq), pl.ds(base + (g // 2) * HEAD_PAIR, HEAD_PAIR)]
        q_rows.append(jnp.where(low if g % 2 == 0 else ~low, pair, zero))
        sink_cols.append(jnp.full((1, nq), sink_ref[h * GQA_GROUP + g] * LOG2E, F32))
    qg = jnp.concatenate(q_rows, axis=0)
    snk = jnp.concatenate(sink_cols, axis=1)
    s = lax.dot_general(k, qg, (((1,), (1,)), ((), ())), preferred_element_type=F32)
    return s, snk


def _attend_softmax(s, snk, masks):
    blocks, row = [], 0
    for first, bias in masks:
        if first > row:
            blocks.append(s[row:first])
        row = first + bias.shape[0]
        blocks.append(s[first:row] + bias)
    if masks:
        if row < s.shape[0]:
            blocks.append(s[row:])
        s = jnp.concatenate(blocks, axis=0)
    m = jnp.maximum(snk, jnp.max(s, axis=0, keepdims=True))
    return jnp.exp2(s - m).astype(BF16), jnp.exp2(snk - m)


def _attend_values(o_ref, h, p, sink_p, vt, row0, nq):
    base = h * GQA_GROUP * HEAD_DIM
    vt_ones = jnp.concatenate([vt, jnp.ones((BF16_ROWS, vt.shape[1]), BF16)], axis=0)
    acc = jnp.dot(vt_ones, p, preferred_element_type=F32)
    den = sink_p + acc[HEAD_DIM:HEAD_DIM + 1]
    acc = acc[:HEAD_DIM] * (1.0 / den)
    for j in range(GQA_GROUP // 2):
        pair_t = jnp.concatenate([acc[:, (2 * j) * nq:(2 * j + 1) * nq], acc[:, (2 * j + 1) * nq:(2 * j + 2) * nq]],
                                 axis=0)
        o_ref[0, pl.ds(row0, nq), pl.ds(base + j * HEAD_PAIR, HEAD_PAIR)] = pair_t.T.astype(o_ref.dtype)


def _attn_oproj_kernel(sink_ref, q_ref, kp_ref, kc_ref, kn_ref, vp_ref, vc_ref, vn_ref, kx_ref, vx_ref,
                       x_ref, mod_ref, npost_ref, w_ref, o_ref, a_scr, y_scr, *, steps_per_seq):
    n = pl.program_id(0)
    n_tiles = pl.num_programs(0) - 1
    slot = n % 2
    nq = Q_BLOCK
    rows = q_ref.shape[1]
    n_sub = rows // nq
    piece = 2 * LANES

    def attention(fillers):
        t = n % steps_per_seq
        shape = (nq, GQA_GROUP * nq)
        key = lax.broadcasted_iota(jnp.int32, shape, 0)
        qry = lax.broadcasted_iota(jnp.int32, shape, 1) % nq
        inner_prev = jnp.where(key >= qry, 0.0, NEG_INF)
        inner_next = jnp.where(key <= qry, 0.0, NEG_INF)
        first_prev = jnp.where((key >= qry) & (t > 0), 0.0, NEG_INF)
        last_next = jnp.where((key <= qry) & (t < steps_per_seq - 1), 0.0, NEG_INF)
        k_loc = jnp.concatenate([kp_ref[0], kc_ref[0], kn_ref[0]], axis=0)
        vt_loc = jnp.concatenate([vp_ref[0], vc_ref[0], vn_ref[0]], axis=1)
        n_kv = vt_loc.shape[0] // HEAD_DIM
        a_out = a_scr.at[pl.ds(slot, 1)]

        units = []
        for u in range(n_sub):
            k_all = jnp.concatenate([k_loc[u * nq:(u + 3) * nq], kx_ref[0]], axis=0)
            vt_all = jnp.concatenate([vt_loc[:, u * nq:(u + 3) * nq], vx_ref[0]], axis=1)
            masks = [(0, first_prev if u == 0 else inner_prev),
                     (2 * nq, last_next if u == n_sub - 1 else inner_next)]
            for h in range(n_kv):
                units.append((u * nq, h, k_all[:, h * HEAD_PAIR:(h + 1) * HEAD_PAIR],
                              vt_all[h * HEAD_DIM:(h + 1) * HEAD_DIM, :], masks))
        scores, probs = {}, {}
        for i in range(len(units) + 2):
            if i < len(units):
                row0, h, k, _, _ = units[i]
                scores[i] = _attend_scores(q_ref, sink_ref, h, k, row0, nq)
            if 0 <= i - 2 < len(units):
                row0, h, _, vt, _ = units[i - 2]
                _attend_values(a_out, h, *probs.pop(i - 2), vt, row0, nq)
            if 0 <= i - 1 < len(units):
                probs[i - 1] = _attend_softmax(*scores.pop(i - 1), units[i - 1][4])
            if i < len(fillers):
                fillers[i]()
        for f in fillers[len(units) + 2:]:
            f()

    def project(j):
        def run():
            cols = pl.ds(j * piece, piece)
            y_scr[:, cols] = jnp.dot(a_scr[1 - slot], w_ref[:, cols], preferred_element_type=F32)
        return run

    def finish():
        w = mod_ref[0, 2:3, :] * npost_ref[...]
        for i in range(rows // ROW_SLAB):
            sl = pl.ds(i * ROW_SLAB, ROW_SLAB)
            o_ref[0, sl, :] = x_ref[0, sl, :] + _rms(y_scr[sl, :], w)

    pieces = [project(j) for j in range(w_ref.shape[1] // piece)]

    @pl.when(n == 0)
    def _():
        attention([])

    @pl.when((n > 0) & (n < n_tiles))
    def _():
        attention(pieces)
        finish()

    @pl.when(n == n_tiles)
    def _():
        for f in pieces:
            f()
        finish()


def _attn_oproj_call(q, k, vt, kx, vxt, sinks, x, mod, npost, w_o, n_ctx, n_sub):
    b, l, qd = q.shape
    d = x.shape[2]
    kd, vd = k.shape[2], vt.shape[1]
    nb = l // Q_BLOCK
    rows = n_sub * Q_BLOCK
    spb = nb // n_sub
    n_tiles = b * spb
    cur = lambda n: jnp.minimum(n, n_tiles - 1)
    done = lambda n: jnp.maximum(n - 1, 0)
    prev = lambda t: jnp.maximum(t * n_sub - 1, 0)
    nxt = lambda t: jnp.minimum((t + 1) * n_sub, nb - 1)
    k_halo = lambda f: pl.BlockSpec((1, Q_BLOCK, kd), lambda n: (cur(n) // spb, f(cur(n) % spb), 0))
    v_halo = lambda f: pl.BlockSpec((1, vd, Q_BLOCK), lambda n: (cur(n) // spb, 0, f(cur(n) % spb)))
    return pl.pallas_call(
        functools.partial(_attn_oproj_kernel, steps_per_seq=spb),
        grid=(n_tiles + 1,),
        in_specs=[
            pl.BlockSpec(memory_space=pltpu.SMEM),
            pl.BlockSpec((1, rows, qd), lambda n: (cur(n) // spb, cur(n) % spb, 0)),
            k_halo(prev), pl.BlockSpec((1, rows, kd), lambda n: (cur(n) // spb, cur(n) % spb, 0)), k_halo(nxt),
            v_halo(prev), pl.BlockSpec((1, vd, rows), lambda n: (cur(n) // spb, 0, cur(n) % spb)), v_halo(nxt),
            pl.BlockSpec((1, n_ctx, kd), lambda n: (0, cur(n) // spb, 0)),
            pl.BlockSpec((1, vd, n_ctx), lambda n: (0, 0, cur(n) // spb)),
            pl.BlockSpec((1, rows, d), lambda n: (done(n) // spb, done(n) % spb, 0)),
            pl.BlockSpec((1, 6, d), lambda n: (done(n) // spb, 0, 0)),
            pl.BlockSpec((1, d), lambda n: (0, 0)),
            _resident(w_o.shape),
        ],
        out_specs=pl.BlockSpec((1, rows, d), lambda n: (done(n) // spb, done(n) % spb, 0)),
        out_shape=jax.ShapeDtypeStruct(x.shape, F32),
        scratch_shapes=[pltpu.VMEM((2, rows, qd), BF16), pltpu.VMEM((rows, d), F32)],
        compiler_params=_params("arbitrary"),
        name="window_attn_proj",
    )(sinks, q, k, k, k, vt, vt, vt, kx, vxt, x, mod, npost, w_o)


def _ctx_attn_kernel(sink_ref, q_ref, k_ref, vt_ref, o_ref):
    nq = q_ref.shape[1]
    for h in range(vt_ref.shape[1] // HEAD_DIM):
        sc = _attend_scores(q_ref, sink_ref, h, k_ref[0, :, h * HEAD_PAIR:(h + 1) * HEAD_PAIR], 0, nq)
        _attend_values(o_ref, h, *_attend_softmax(*sc, []), vt_ref[0, h * HEAD_DIM:(h + 1) * HEAD_DIM, :], 0, nq)


def _ctx_attn_call(q, k, vt, sinks, n_ctx):
    qd, kd, vd = q.shape[2], k.shape[2], vt.shape[1]
    return pl.pallas_call(
        _ctx_attn_kernel,
        grid=(q.shape[1] // n_ctx,),
        in_specs=[
            pl.BlockSpec(memory_space=pltpu.SMEM),
            pl.BlockSpec((1, n_ctx, qd), lambda i: (0, i, 0)),
            pl.BlockSpec((1, n_ctx, kd), lambda i: (0, i, 0)),
            pl.BlockSpec((1, vd, n_ctx), lambda i: (0, 0, i)),
        ],
        out_specs=pl.BlockSpec((1, n_ctx, qd), lambda i: (0, i, 0)),
        out_shape=jax.ShapeDtypeStruct(q.shape, BF16),
        compiler_params=_params("parallel"),
        name="ctx_attn",
    )(sinks, q, k, vt)


def _oproj_kernel(a_ref, x_ref, mod_ref, npost_ref, w_ref, o_ref):
    y = jnp.dot(a_ref[0], w_ref[...], preferred_element_type=F32)
    o_ref[0] = x_ref[0] + _rms(y, mod_ref[0, 2:3, :] * npost_ref[...])


def _oproj_call(a, x, mod, npost, w_o, tm):
    b, l, d = x.shape
    ad = a.shape[2]
    return pl.pallas_call(
        _oproj_kernel,
        grid=(b, l // tm),
        in_specs=[
            pl.BlockSpec((1, tm, ad), lambda i, t: (i, t, 0)),
            pl.BlockSpec((1, tm, d), lambda i, t: (i, t, 0)),
            pl.BlockSpec((1, 6, d), lambda i, t: (i, 0, 0)),
            pl.BlockSpec((1, d), lambda i, t: (0, 0)),
            _resident(w_o.shape),
        ],
        out_specs=pl.BlockSpec((1, tm, d), lambda i, t: (i, t, 0)),
        out_shape=jax.ShapeDtypeStruct((b, l, d), F32),
        compiler_params=_params("parallel", "parallel"),
        name="attn_out_proj",
    )(a, x, mod, npost, w_o)


def _rope_tables(l):
    axis_dim = HEAD_DIM // 2
    rows_n = l // GRID_W
    row = jnp.repeat(jnp.arange(rows_n), GRID_W).astype(F32)
    col = jnp.tile(jnp.arange(GRID_W), rows_n).astype(F32)
    inv = 1.0 / (ROPE_BASE ** (jnp.arange(0, axis_dim, 2, dtype=F32) / axis_dim))
    ang_r = row[:, None] * inv[None, :]
    ang_c = col[:, None] * inv[None, :]
    ang = jnp.concatenate([ang_r, ang_r, ang_c, ang_c], axis=-1)
    sign = jnp.tile(jnp.concatenate([-jnp.ones(axis_dim // 2, F32), jnp.ones(axis_dim // 2, F32)]), 2)
    cos, sin = jnp.cos(ang), jnp.sin(ang) * sign[None, :]
    return jnp.tile(cos, (1, LANES // HEAD_DIM)), jnp.tile(sin, (1, LANES // HEAD_DIM))


def _dup_heads(w, n_heads):
    d = w.shape[0]
    w = w.reshape(d, n_heads, 1, HEAD_DIM)
    return jnp.broadcast_to(w, (d, n_heads, 2, HEAD_DIM)).reshape(d, n_heads * HEAD_PAIR)


def _token_tile(l, target):
    return min(l, target)


def kernel(x, c, ctx, c_ctx, w_ada, b_ada, norm_pre_mix, norm_post_mix, norm_pre_ffn, norm_post_ffn,
           w_pool, pool_scale, w_qkv, w_o, attn_sinks, w_gate_up, w_down):
    b, l, d = x.shape
    depth = w_ada.shape[0]
    n_mixers = 2
    q_dim = w_o.shape[1]
    kv_heads = (w_qkv.shape[2] - q_dim) // (2 * HEAD_DIM)
    kv_dim = kv_heads * HEAD_DIM

    rows = -(-(b + 1) // BF16_ROWS) * BF16_ROWS
    c_rows = jnp.concatenate([c, c_ctx[None, :], jnp.zeros((rows - b - 1, d), F32)], axis=0)
    ada = _ada_call(c_rows, w_ada, b_ada).reshape(depth, rows, 6, d)
    mod_x = ada[:, :b]
    mod_c = jnp.broadcast_to(ada[:, b:b + 1], (depth, b, 6, d))

    bf = lambda w: w.astype(BF16)
    ffn_w = {}
    tf = 512
    ffn_casts = lambda m: [(w_gate_up, m, tf), (w_down, m, None)]

    def qkv_weights(w):
        w = bf(w)
        return jnp.concatenate([w[:, :q_dim], _dup_heads(w[:, q_dim:q_dim + kv_dim], kv_heads),
                                w[:, q_dim + kv_dim:]], axis=-1)

    tables = _rope_tables(l)

    lc = ctx.shape[1]
    tm_x = _token_tile(l, 512)
    tm_ffn = _token_tile(l, 1024)
    tm_c = _token_tile(b * lc, 512)

    flat = lambda a: a.reshape(1, b * lc, a.shape[-1])
    unflat = lambda a: a.reshape(b, lc, a.shape[-1])

    for i in range(depth):
        last = i == depth - 1
        j = i // n_mixers
        npre, npost = norm_pre_mix[i][None, :], norm_post_mix[i][None, :]
        if i % n_mixers == 0:
            ps, w_p = pool_scale[j][None, :], bf(w_pool[j])
            if i in ffn_w:
                x = _pool_call(x, mod_x[i], npre, npost, w_p, ps, tm_x)
            else:
                x, *cast = _pool_call(x, mod_x[i], npre, npost, w_p, ps, tm_x, casts=ffn_casts(i))
                ffn_w[i] = tuple(cast)
            if not last:
                ctx = _pool_call(ctx, mod_c[i], npre, npost, w_p, ps, _token_tile(lc, 512))
        else:
            w_in, w_out = qkv_weights(w_qkv[j]), bf(w_o[j])
            hosted = [i] + ([i + 1] if i + 1 < depth and (i + 1) % n_mixers == 0 else [])
            q, k, v, *cast = _qkv_call(x, mod_x[i], npre, w_in, q_dim, tables, tm_x,
                                       casts=[job for m in hosted for job in ffn_casts(m)])
            for idx, m in enumerate(hosted):
                ffn_w[m] = (cast[2 * idx], cast[2 * idx + 1])
            qc, kc, vct = _qkv_call(flat(ctx), mod_c[i][:1], npre, w_in, q_dim, None, tm_c)
            x = _attn_oproj_call(q, k, v, kc, vct, attn_sinks[j], x, mod_x[i], npost, w_out, lc,
                                 2 if (l // Q_BLOCK) % 2 == 0 else 1)
            if not last:
                ac = _ctx_attn_call(qc, kc, vct, attn_sinks[j], lc)
                ctx = unflat(_oproj_call(ac, flat(ctx), mod_c[i][:1], npost, w_out, tm_c))

        npre, npost = norm_pre_ffn[i][None, :], norm_post_ffn[i][None, :]
        w_gu, w_dn = ffn_w.pop(i)
        x = _ffn_call(x, mod_x[i], npre, npost, w_gu, w_dn, tm_ffn)
        if not last:
            ctx = unflat(_ffn_call(flat(ctx), mod_c[i][:1], npre, npost, w_gu, w_dn, _token_tile(b * lc, 1024)))
    return x
```

```python
import functools

import jax
import jax.numpy as jnp
from jax import lax
from jax.experimental import pallas as pl
from jax.experimental.pallas import tpu as pltpu

GRID_W = 64
POOL_WINDOWS = (2, 4, 8, 16)
HEAD_DIM = 64
GQA_GROUP = 8
WINDOW = 128
Q_BLOCK = WINDOW
ROPE_BASE = 10000.0
RMS_EPS = 1e-6
NEG_INF = -1e30
LOG2E = 1.4426950408889634

LANES = 128
SUBLANES = 8
BF16_ROWS = 16
MXU_COLS = 256
VMEM_LIMIT_BYTES = 56 * 1024 * 1024
SWIGLU_VMEM_LIMIT_BYTES = 63 * 1024 * 1024

ROW_TILE = 512
SWIGLU_ROW_TILE = 1024
SWIGLU_CHUNK = 512
ADA_COLS = 1024
ATTN_BLOCKS_PER_STEP = 2

ROW_SLAB = BF16_ROWS
POOL_HALO = SUBLANES
HEAD_PAIR = 2 * HEAD_DIM

F32 = jnp.float32
BF16 = jnp.bfloat16


def _params(*semantics, vmem_limit_bytes=VMEM_LIMIT_BYTES):
    return pltpu.CompilerParams(dimension_semantics=semantics, vmem_limit_bytes=vmem_limit_bytes)


def _resident(shape):
    nd = len(shape)
    return pl.BlockSpec(shape, lambda *_: (0,) * nd, pipeline_mode=pl.Buffered(1))


def _rms(x, w):
    ms = jnp.mean(x * x, axis=-1, keepdims=True)
    return x * lax.rsqrt(ms + RMS_EPS) * w


def _norm_mod(x, w, shift, scale):
    return _rms(x, w * (1.0 + scale)) + shift


def _cast_side_jobs(casts, tiles, n_steps):
    in_specs, out_specs, out_shape = [], [], []
    for w, layer, chunk in casts:
        _, rows, cols = w.shape
        n_blocks = n_steps
        while rows % (n_blocks * BF16_ROWS):
            n_blocks //= 2
        hold, br = n_steps // n_blocks, rows // n_blocks
        in_specs.append(pl.BlockSpec((None, br, cols),
                                     lambda i, t, layer=layer, hold=hold: (layer, (i * tiles + t) // hold, 0)))
        if chunk is None:
            out_specs.append(pl.BlockSpec((br, cols), lambda i, t, hold=hold: ((i * tiles + t) // hold, 0)))
            out_shape.append(jax.ShapeDtypeStruct((rows, cols), BF16))
        else:
            out_specs.append(pl.BlockSpec((cols // chunk, br, chunk),
                                          lambda i, t, hold=hold: (0, (i * tiles + t) // hold, 0)))
            out_shape.append(jax.ShapeDtypeStruct((cols // chunk, rows, chunk), BF16))
    return in_specs, out_specs, out_shape


def _run_casts(cast_in, cast_out):
    for src, dst in zip(cast_in, cast_out):
        if len(dst.shape) == 2:
            dst[...] = src[...].astype(BF16)
        else:
            chunk = dst.shape[2]
            for c in range(dst.shape[0]):
                dst[c] = src[:, c * chunk:(c + 1) * chunk].astype(BF16)


def _ada_kernel(c_ref, w_ref, b_ref, o_ref):
    s = jax.nn.silu(c_ref[...]).astype(BF16)
    o_ref[0] = jnp.dot(s, w_ref[0].astype(BF16), preferred_element_type=F32) + b_ref[0]


def _ada_call(c_rows, w_ada, b_ada):
    depth, d, n = w_ada.shape
    rows = c_rows.shape[0]
    tn = ADA_COLS
    return pl.pallas_call(
        _ada_kernel,
        grid=(depth, n // tn),
        in_specs=[
            pl.BlockSpec((rows, d), lambda i, j: (0, 0)),
            pl.BlockSpec((1, d, tn), lambda i, j: (i, 0, j)),
            pl.BlockSpec((1, 1, tn), lambda i, j: (i, 0, j)),
        ],
        out_specs=pl.BlockSpec((1, rows, tn), lambda i, j: (i, 0, j)),
        out_shape=jax.ShapeDtypeStruct((depth, rows, n), F32),
        compiler_params=_params("parallel", "parallel"),
        name="ada_proj",
    )(c_rows, w_ada, b_ada.reshape(depth, 1, n))


def _ffn_kernel(x_ref, mod_ref, npre_ref, npost_ref, wg_ref, wu_ref, wd_ref, o_ref, h_ref):
    k = pl.program_id(2)
    last = pl.num_programs(2) - 1
    tm = x_ref.shape[1]

    def prologue():
        w = npre_ref[...] * (1.0 + mod_ref[0, 4:5, :])
        shift = mod_ref[0, 3:4, :]
        for i in range(tm // ROW_SLAB):
            rows = pl.ds(i * ROW_SLAB, ROW_SLAB)
            h_ref[rows, :] = (_rms(x_ref[0, rows, :], w) + shift).astype(BF16)

    def chunk(first):
        h = h_ref[...]
        g = jnp.dot(h, wg_ref[...], preferred_element_type=F32)
        u = jnp.dot(h, wu_ref[...], preferred_element_type=F32)
        a = (jax.nn.silu(g) * u).astype(BF16)
        part = jnp.dot(a, wd_ref[...], preferred_element_type=F32)
        if first:
            o_ref[0] = part
        else:
            o_ref[0] += part

    def epilogue():
        w = mod_ref[0, 5:6, :] * npost_ref[...]
        for i in range(tm // ROW_SLAB):
            rows = pl.ds(i * ROW_SLAB, ROW_SLAB)
            o_ref[0, rows, :] = x_ref[0, rows, :] + _rms(o_ref[0, rows, :], w)

    @pl.when(k == 0)
    def _():
        prologue()
        chunk(True)

    @pl.when((k > 0) & (k < last))
    def _():
        chunk(False)

    @pl.when(k == last)
    def _():
        chunk(False)
        epilogue()


def _ffn_call(x, mod, npre, npost, w_gate_up, w_down, tm):
    b, l, d = x.shape
    f = w_down.shape[0]
    tf = w_gate_up.shape[2]
    nk = f // tf
    return pl.pallas_call(
        _ffn_kernel,
        grid=(b, l // tm, nk),
        in_specs=[
            pl.BlockSpec((1, tm, d), lambda i, t, k: (i, t, 0)),
            pl.BlockSpec((1, 6, d), lambda i, t, k: (i, 0, 0)),
            pl.BlockSpec((1, d), lambda i, t, k: (0, 0)),
            pl.BlockSpec((1, d), lambda i, t, k: (0, 0)),
            pl.BlockSpec((None, d, tf), lambda i, t, k: (k, 0, 0)),
            pl.BlockSpec((None, d, tf), lambda i, t, k: (nk + k, 0, 0)),
            pl.BlockSpec((tf, d), lambda i, t, k: (k, 0)),
        ],
        out_specs=pl.BlockSpec((1, tm, d), lambda i, t, k: (i, t, 0)),
        out_shape=jax.ShapeDtypeStruct((b, l, d), F32),
        scratch_shapes=[pltpu.VMEM((tm, d), BF16)],
        compiler_params=_params("parallel", "parallel", "arbitrary", vmem_limit_bytes=SWIGLU_VMEM_LIMIT_BYTES),
        name="swiglu",
    )(x, mod, npre, npost, w_gate_up, w_gate_up, w_down)


def _pool_kernel(x_ref, xp_ref, xn_ref, mod_ref, npre_ref, npost_ref, wp_ref, ps_ref, *rest, seq_len, n_casts):
    o_ref, h_ref = rest[n_casts], rest[-1]
    _run_casts(rest[:n_casts], rest[n_casts + 1:-1])
    t = pl.program_id(1)
    tm = x_ref.shape[1]
    gd = wp_ref.shape[1]
    d = x_ref.shape[2]
    npre = npre_ref[...]
    shift, scale = mod_ref[0, 0:1, :], mod_ref[0, 1:2, :]
    span = tm + 2 * POOL_HALO
    n_groups = len(POOL_WINDOWS)

    def put(rows, value):
        for g in range(n_groups):
            h_ref[g, rows, :] = value[:, g * gd:(g + 1) * gd]

    hp = _norm_mod(xp_ref[0], npre, shift, scale)
    hn = _norm_mod(xn_ref[0], npre, shift, scale)
    put(pl.ds(0, POOL_HALO), jnp.where(t > 0, hp, 0.0))
    w_mod = npre * (1.0 + scale)
    for i in range(tm // ROW_SLAB):
        put(pl.ds(POOL_HALO + i * ROW_SLAB, ROW_SLAB),
            _rms(x_ref[0, pl.ds(i * ROW_SLAB, ROW_SLAB), :], w_mod) + shift)
    put(pl.ds(POOL_HALO + tm, POOL_HALO), jnp.where(t < pl.num_programs(1) - 1, hn, 0.0))

    pos = t * tm + lax.broadcasted_iota(jnp.int32, (tm, LANES), 0)
    ss = jnp.zeros((tm, 1), F32)
    for g, w in enumerate(POOL_WINDOWS):
        cols = pl.ds(g * gd, gd)
        cur = h_ref[g]
        cur = cur + pltpu.roll(cur, 1, 0)
        reach = 1
        while 2 * reach < w:
            cur = pltpu.roll(cur, reach, 0) + pltpu.roll(cur, span - reach, 0)
            reach *= 2
        acc = cur[POOL_HALO:POOL_HALO + tm]
        cnt = jnp.minimum(pos + w // 2, seq_len) - jnp.maximum(pos - w // 2, 0)
        inv_cnt = jnp.concatenate([1.0 / cnt.astype(F32)] * (gd // LANES), axis=1)
        p = acc * inv_cnt - h_ref[g, pl.ds(POOL_HALO, tm), :]
        y = jnp.dot(p.astype(BF16), wp_ref[g], preferred_element_type=F32) * ps_ref[:, cols]
        o_ref[0, :, cols] = y
        ss = ss + jnp.sum(y * y, axis=-1, keepdims=True)

    rstd = lax.rsqrt(ss * (1.0 / d) + RMS_EPS)
    o_ref[0] = x_ref[0] + o_ref[0] * rstd * (mod_ref[0, 2:3, :] * npost_ref[...])


def _pool_call(x, mod, npre, npost, w_pool, pool_scale, tm, casts=()):
    b, l, d = x.shape
    hb = tm // POOL_HALO
    last_hb = l // POOL_HALO - 1
    tiles = l // tm
    cast_in, cast_out, cast_shape = _cast_side_jobs(casts, tiles, b * tiles)
    out = pl.pallas_call(
        functools.partial(_pool_kernel, seq_len=l, n_casts=len(casts)),
        grid=(b, tiles),
        in_specs=[
            pl.BlockSpec((1, tm, d), lambda i, t: (i, t, 0)),
            pl.BlockSpec((1, POOL_HALO, d), lambda i, t: (i, jnp.maximum(t * hb - 1, 0), 0)),
            pl.BlockSpec((1, POOL_HALO, d), lambda i, t: (i, jnp.minimum((t + 1) * hb, last_hb), 0)),
            pl.BlockSpec((1, 6, d), lambda i, t: (i, 0, 0)),
            pl.BlockSpec((1, d), lambda i, t: (0, 0)),
            pl.BlockSpec((1, d), lambda i, t: (0, 0)),
            _resident(w_pool.shape),
            pl.BlockSpec((1, d), lambda i, t: (0, 0)),
        ] + cast_in,
        out_specs=[pl.BlockSpec((1, tm, d), lambda i, t: (i, t, 0))] + cast_out,
        out_shape=[jax.ShapeDtypeStruct((b, l, d), F32)] + cast_shape,
        scratch_shapes=[pltpu.VMEM((w_pool.shape[0], tm + 2 * POOL_HALO, w_pool.shape[1]), F32)],
        compiler_params=_params("arbitrary", "arbitrary"),
        name="pool_mixer",
    )(x, x, x, mod, npre, npost, w_pool, pool_scale, *[job[0] for job in casts])
    return out if casts else out[0]


def _rope(x, cos, sin_signed, low_half):
    out = []
    for j in range(x.shape[1] // LANES):
        c = x[:, j * LANES:(j + 1) * LANES]
        rot = jnp.where(low_half, pltpu.roll(c, LANES - HEAD_DIM // 4, 1), pltpu.roll(c, HEAD_DIM // 4, 1))
        out.append(c * cos + rot * sin_signed)
    return jnp.concatenate(out, axis=1)


def _qkv_kernel(*refs, rope, q_dim, k_dim, n_casts):
    n_in = (6 if rope else 4) + n_casts
    cast_in, cast_out = refs[n_in - n_casts:n_in], refs[n_in + 3:]
    if rope:
        x_ref, mod_ref, npre_ref, w_ref, cos_ref, sin_ref = refs[:6]
    else:
        x_ref, mod_ref, npre_ref, w_ref = refs[:4]
    q_ref, k_ref, vt_ref = refs[n_in:n_in + 3]
    _run_casts(cast_in, cast_out)
    h = _norm_mod(x_ref[0], npre_ref[...], mod_ref[0, 0:1, :], mod_ref[0, 1:2, :]).astype(BF16)
    qkv = jnp.dot(h, w_ref[...], preferred_element_type=F32)
    q = qkv[:, :q_dim]
    k = qkv[:, q_dim:q_dim + k_dim]
    v = qkv[:, q_dim + k_dim:]
    if rope:
        cos, sin_signed = cos_ref[...], sin_ref[...]
        lane = lax.broadcasted_iota(jnp.int32, cos.shape, 1)
        low_half = (lane % (HEAD_DIM // 2)) < (HEAD_DIM // 4)
        q = _rope(q, cos, sin_signed, low_half)
        k = _rope(k, cos, sin_signed, low_half)
    q_ref[0] = (q * (HEAD_DIM ** -0.5 * LOG2E)).astype(BF16)
    k_ref[0] = k.astype(BF16)
    vt_ref[0] = v.T.astype(BF16)


def _qkv_call(x, mod, npre, w_qkv_dup, q_dim, rope_tables, tm, casts=()):
    b, l, d = x.shape
    n = w_qkv_dup.shape[1]
    v_dim = (n - q_dim) // 3
    k_dim = 2 * v_dim
    rope = rope_tables is not None
    in_specs = [
        pl.BlockSpec((1, tm, d), lambda i, t: (i, t, 0)),
        pl.BlockSpec((1, 6, d), lambda i, t: (i, 0, 0)),
        pl.BlockSpec((1, d), lambda i, t: (0, 0)),
        _resident(w_qkv_dup.shape),
    ]
    args = [x, mod, npre, w_qkv_dup]
    if rope:
        in_specs += [pl.BlockSpec((tm, LANES), lambda i, t: (t, 0))] * 2
        args += list(rope_tables)
    out_specs = [
        pl.BlockSpec((1, tm, q_dim), lambda i, t: (i, t, 0)),
        pl.BlockSpec((1, tm, k_dim), lambda i, t: (i, t, 0)),
        pl.BlockSpec((1, v_dim, tm), lambda i, t: (i, 0, t)),
    ]
    out_shape = [
        jax.ShapeDtypeStruct((b, l, q_dim), BF16),
        jax.ShapeDtypeStruct((b, l, k_dim), BF16),
        jax.ShapeDtypeStruct((b, v_dim, l), BF16),
    ]
    tiles = l // tm
    cast_in, cast_out, cast_shape = _cast_side_jobs(casts, tiles, b * tiles)
    in_specs, out_specs, out_shape = in_specs + cast_in, out_specs + cast_out, out_shape + cast_shape
    args += [job[0] for job in casts]
    return pl.pallas_call(
        functools.partial(_qkv_kernel, rope=rope, q_dim=q_dim, k_dim=k_dim, n_casts=len(casts)),
        grid=(b, tiles),
        in_specs=in_specs,
        out_specs=out_specs,
        out_shape=out_shape,
        compiler_params=_params("arbitrary", "arbitrary"),
        name="qkv_rope" if rope else "qkv_ctx",
    )(*args)


def _attend_scores(q_ref, sink_ref, h, k, row0, nq):
    low = lax.broadcasted_iota(jnp.int32, (nq, HEAD_PAIR), 1) < HEAD_DIM
    zero = jnp.zeros((), BF16)
    base = h * GQA_GROUP * HEAD_DIM
    q_rows, sink_cols = [], []
    for g in range(GQA_GROUP):
        pair = q_ref[0, pl.ds(row0, nq), pl.ds(base + (g // 2) * HEAD_PAIR, HEAD_PAIR)]
        q_rows.append(jnp.where(low if g % 2 == 0 else ~low, pair, zero))
        sink_cols.append(jnp.full((1, nq), sink_ref[h * GQA_GROUP + g] * LOG2E, F32))
    qg = jnp.concatenate(q_rows, axis=0)
    snk = jnp.concatenate(sink_cols, axis=1)
    s = lax.dot_general(k, qg, (((1,), (1,)), ((), ())), preferred_element_type=F32)
    return s, snk


def _attend_softmax(s, snk, masks):
    blocks, row = [], 0
    for first, bias in masks:
        if first > row:
            blocks.append(s[row:first])
        row = first + bias.shape[0]
        blocks.append(s[first:row] + bias)
    if masks:
        if row < s.shape[0]:
            blocks.append(s[row:])
        s = jnp.concatenate(blocks, axis=0)
    m = jnp.maximum(snk, jnp.max(s, axis=0, keepdims=True))
    return jnp.exp2(s - m).astype(BF16), jnp.exp2(snk - m)


def _attend_values(o_ref, h, p, sink_p, vt, row0, nq):
    base = h * GQA_GROUP * HEAD_DIM
    vt_ones = jnp.concatenate([vt, jnp.ones((BF16_ROWS, vt.shape[1]), BF16)], axis=0)
    acc = jnp.dot(vt_ones, p, preferred_element_type=F32)
    den = sink_p + acc[HEAD_DIM:HEAD_DIM + 1]
    acc = acc[:HEAD_DIM] * (1.0 / den)
    for j in range(GQA_GROUP // 2):
        pair_t = jnp.concatenate([acc[:, (2 * j) * nq:(2 * j + 1) * nq], acc[:, (2 * j + 1) * nq:(2 * j + 2) * nq]],
                                 axis=0)
        o_ref[0, pl.ds(row0, nq), pl.ds(base + j * HEAD_PAIR, HEAD_PAIR)] = pair_t.T.astype(o_ref.dtype)


def _attn_oproj_kernel(sink_ref, q_ref, kp_ref, kc_ref, kn_ref, vp_ref, vc_ref, vn_ref, kx_ref, vx_ref,
                       x_ref, mod_ref, npost_ref, w_ref, o_ref, a_scr, y_scr, *, steps_per_seq):
    n = pl.program_id(0)
    n_tiles = pl.num_programs(0) - 1
    slot = n % 2
    nq = Q_BLOCK
    rows = q_ref.shape[1]
    n_sub = rows // nq
    piece = MXU_COLS

    def attention(fillers):
        t = n % steps_per_seq
        shape = (nq, GQA_GROUP * nq)
        key = lax.broadcasted_iota(jnp.int32, shape, 0)
        qry = lax.broadcasted_iota(jnp.int32, shape, 1) % nq
        inner_prev = jnp.where(key >= qry, 0.0, NEG_INF)
        inner_next = jnp.where(key <= qry, 0.0, NEG_INF)
        first_prev = jnp.where((key >= qry) & (t > 0), 0.0, NEG_INF)
        last_next = jnp.where((key <= qry) & (t < steps_per_seq - 1), 0.0, NEG_INF)
        k_loc = jnp.concatenate([kp_ref[0], kc_ref[0], kn_ref[0]], axis=0)
        vt_loc = jnp.concatenate([vp_ref[0], vc_ref[0], vn_ref[0]], axis=1)
        n_kv = vt_loc.shape[0] // HEAD_DIM
        a_out = a_scr.at[pl.ds(slot, 1)]

        units = []
        for u in range(n_sub):
            k_all = jnp.concatenate([k_loc[u * nq:(u + 3) * nq], kx_ref[0]], axis=0)
            vt_all = jnp.concatenate([vt_loc[:, u * nq:(u + 3) * nq], vx_ref[0]], axis=1)
            masks = [(0, first_prev if u == 0 else inner_prev),
                     (2 * nq, last_next if u == n_sub - 1 else inner_next)]
            for h in range(n_kv):
                units.append((u * nq, h, k_all[:, h * HEAD_PAIR:(h + 1) * HEAD_PAIR],
                              vt_all[h * HEAD_DIM:(h + 1) * HEAD_DIM, :], masks))
        scores, probs = {}, {}
        for i in range(len(units) + 2):
            if i < len(units):
                row0, h, k, _, _ = units[i]
                scores[i] = _attend_scores(q_ref, sink_ref, h, k, row0, nq)
            if 0 <= i - 2 < len(units):
                row0, h, _, vt, _ = units[i - 2]
                _attend_values(a_out, h, *probs.pop(i - 2), vt, row0, nq)
            if 0 <= i - 1 < len(units):
                probs[i - 1] = _attend_softmax(*scores.pop(i - 1), units[i - 1][4])
            if i < len(fillers):
                fillers[i]()
        for f in fillers[len(units) + 2:]:
            f()

    def project(j):
        def run():
            cols = pl.ds(j * piece, piece)
            y_scr[:, cols] = jnp.dot(a_scr[1 - slot], w_ref[:, cols], preferred_element_type=F32)
        return run

    def finish():
        w = mod_ref[0, 2:3, :] * npost_ref[...]
        for i in range(rows // ROW_SLAB):
            sl = pl.ds(i * ROW_SLAB, ROW_SLAB)
            o_ref[0, sl, :] = x_ref[0, sl, :] + _rms(y_scr[sl, :], w)

    pieces = [project(j) for j in range(w_ref.shape[1] // piece)]

    @pl.when(n == 0)
    def _():
        attention([])

    @pl.when((n > 0) & (n < n_tiles))
    def _():
        attention(pieces)
        finish()

    @pl.when(n == n_tiles)
    def _():
        for f in pieces:
            f()
        finish()


def _attn_oproj_call(q, k, vt, kx, vxt, sinks, x, mod, npost, w_o, n_ctx, n_sub):
    b, l, qd = q.shape
    d = x.shape[2]
    kd, vd = k.shape[2], vt.shape[1]
    nb = l // Q_BLOCK
    rows = n_sub * Q_BLOCK
    spb = nb // n_sub
    n_tiles = b * spb
    cur = lambda n: jnp.minimum(n, n_tiles - 1)
    done = lambda n: jnp.maximum(n - 1, 0)
    prev = lambda t: jnp.maximum(t * n_sub - 1, 0)
    nxt = lambda t: jnp.minimum((t + 1) * n_sub, nb - 1)
    k_halo = lambda f: pl.BlockSpec((1, Q_BLOCK, kd), lambda n: (cur(n) // spb, f(cur(n) % spb), 0))
    v_halo = lambda f: pl.BlockSpec((1, vd, Q_BLOCK), lambda n: (cur(n) // spb, 0, f(cur(n) % spb)))
    return pl.pallas_call(
        functools.partial(_attn_oproj_kernel, steps_per_seq=spb),
        grid=(n_tiles + 1,),
        in_specs=[
            pl.BlockSpec(memory_space=pltpu.SMEM),
            pl.BlockSpec((1, rows, qd), lambda n: (cur(n) // spb, cur(n) % spb, 0)),
            k_halo(prev), pl.BlockSpec((1, rows, kd), lambda n: (cur(n) // spb, cur(n) % spb, 0)), k_halo(nxt),
            v_halo(prev), pl.BlockSpec((1, vd, rows), lambda n: (cur(n) // spb, 0, cur(n) % spb)), v_halo(nxt),
            pl.BlockSpec((1, n_ctx, kd), lambda n: (0, cur(n) // spb, 0)),
            pl.BlockSpec((1, vd, n_ctx), lambda n: (0, 0, cur(n) // spb)),
            pl.BlockSpec((1, rows, d), lambda n: (done(n) // spb, done(n) % spb, 0)),
            pl.BlockSpec((1, 6, d), lambda n: (done(n) // spb, 0, 0)),
            pl.BlockSpec((1, d), lambda n: (0, 0)),
            _resident(w_o.shape),
        ],
        out_specs=pl.BlockSpec((1, rows, d), lambda n: (done(n) // spb, done(n) % spb, 0)),
        out_shape=jax.ShapeDtypeStruct(x.shape, F32),
        scratch_shapes=[pltpu.VMEM((2, rows, qd), BF16), pltpu.VMEM((rows, d), F32)],
        compiler_params=_params("arbitrary"),
        name="window_attn_proj",
    )(sinks, q, k, k, k, vt, vt, vt, kx, vxt, x, mod, npost, w_o)


def _ctx_attn_kernel(sink_ref, q_ref, k_ref, vt_ref, o_ref):
    nq = q_ref.shape[1]
    for h in range(vt_ref.shape[1] // HEAD_DIM):
        sc = _attend_scores(q_ref, sink_ref, h, k_ref[0, :, h * HEAD_PAIR:(h + 1) * HEAD_PAIR], 0, nq)
        _attend_values(o_ref, h, *_attend_softmax(*sc, []), vt_ref[0, h * HEAD_DIM:(h + 1) * HEAD_DIM, :], 0, nq)


def _ctx_attn_call(q, k, vt, sinks, n_ctx):
    qd, kd, vd = q.shape[2], k.shape[2], vt.shape[1]
    return pl.pallas_call(
        _ctx_attn_kernel,
        grid=(q.shape[1] // n_ctx,),
        in_specs=[
            pl.BlockSpec(memory_space=pltpu.SMEM),
            pl.BlockSpec((1, n_ctx, qd), lambda i: (0, i, 0)),
            pl.BlockSpec((1, n_ctx, kd), lambda i: (0, i, 0)),
            pl.BlockSpec((1, vd, n_ctx), lambda i: (0, 0, i)),
        ],
        out_specs=pl.BlockSpec((1, n_ctx, qd), lambda i: (0, i, 0)),
        out_shape=jax.ShapeDtypeStruct(q.shape, BF16),
        compiler_params=_params("parallel"),
        name="ctx_attn",
    )(sinks, q, k, vt)


def _oproj_kernel(a_ref, x_ref, mod_ref, npost_ref, w_ref, o_ref):
    y = jnp.dot(a_ref[0], w_ref[...], preferred_element_type=F32)
    o_ref[0] = x_ref[0] + _rms(y, mod_ref[0, 2:3, :] * npost_ref[...])


def _oproj_call(a, x, mod, npost, w_o, tm):
    b, l, d = x.shape
    ad = a.shape[2]
    return pl.pallas_call(
        _oproj_kernel,
        grid=(b, l // tm),
        in_specs=[
            pl.BlockSpec((1, tm, ad), lambda i, t: (i, t, 0)),
            pl.BlockSpec((1, tm, d), lambda i, t: (i, t, 0)),
            pl.BlockSpec((1, 6, d), lambda i, t: (i, 0, 0)),
            pl.BlockSpec((1, d), lambda i, t: (0, 0)),
            _resident(w_o.shape),
        ],
        out_specs=pl.BlockSpec((1, tm, d), lambda i, t: (i, t, 0)),
        out_shape=jax.ShapeDtypeStruct((b, l, d), F32),
        compiler_params=_params("parallel", "parallel"),
        name="attn_out_proj",
    )(a, x, mod, npost, w_o)


def _rope_tables(l):
    axis_dim = HEAD_DIM // 2
    rows_n = l // GRID_W
    row = jnp.repeat(jnp.arange(rows_n), GRID_W).astype(F32)
    col = jnp.tile(jnp.arange(GRID_W), rows_n).astype(F32)
    inv = 1.0 / (ROPE_BASE ** (jnp.arange(0, axis_dim, 2, dtype=F32) / axis_dim))
    ang_r = row[:, None] * inv[None, :]
    ang_c = col[:, None] * inv[None, :]
    ang = jnp.concatenate([ang_r, ang_r, ang_c, ang_c], axis=-1)
    sign = jnp.tile(jnp.concatenate([-jnp.ones(axis_dim // 2, F32), jnp.ones(axis_dim // 2, F32)]), 2)
    cos, sin = jnp.cos(ang), jnp.sin(ang) * sign[None, :]
    return jnp.tile(cos, (1, LANES // HEAD_DIM)), jnp.tile(sin, (1, LANES // HEAD_DIM))


def _dup_heads(w, n_heads):
    d = w.shape[0]
    w = w.reshape(d, n_heads, 1, HEAD_DIM)
    return jnp.broadcast_to(w, (d, n_heads, 2, HEAD_DIM)).reshape(d, n_heads * HEAD_PAIR)


def _token_tile(l, target):
    return min(l, target)


def kernel(x, c, ctx, c_ctx, w_ada, b_ada, norm_pre_mix, norm_post_mix, norm_pre_ffn, norm_post_ffn,
           w_pool, pool_scale, w_qkv, w_o, attn_sinks, w_gate_up, w_down):
    b, l, d = x.shape
    depth = w_ada.shape[0]
    n_mixers = 2
    q_dim = w_o.shape[1]
    kv_heads = (w_qkv.shape[2] - q_dim) // (2 * HEAD_DIM)
    kv_dim = kv_heads * HEAD_DIM

    rows = -(-(b + 1) // BF16_ROWS) * BF16_ROWS
    c_rows = jnp.concatenate([c, c_ctx[None, :], jnp.zeros((rows - b - 1, d), F32)], axis=0)
    ada = _ada_call(c_rows, w_ada, b_ada).reshape(depth, rows, 6, d)
    mod_x = ada[:, :b]
    mod_c = jnp.broadcast_to(ada[:, b:b + 1], (depth, b, 6, d))

    bf = lambda w: w.astype(BF16)
    ffn_w = {}
    ffn_casts = lambda m: [(w_gate_up, m, SWIGLU_CHUNK), (w_down, m, None)]

    def qkv_weights(w):
        w = bf(w)
        return jnp.concatenate([w[:, :q_dim], _dup_heads(w[:, q_dim:q_dim + kv_dim], kv_heads),
                                w[:, q_dim + kv_dim:]], axis=-1)

    tables = _rope_tables(l)

    lc = ctx.shape[1]
    nb = l // Q_BLOCK
    tm_x, tm_ffn = _token_tile(l, ROW_TILE), _token_tile(l, SWIGLU_ROW_TILE)
    tm_c, tm_c_ffn = _token_tile(b * lc, ROW_TILE), _token_tile(b * lc, SWIGLU_ROW_TILE)

    flat = lambda a: a.reshape(1, b * lc, a.shape[-1])
    unflat = lambda a: a.reshape(b, lc, a.shape[-1])

    for i in range(depth):
        last = i == depth - 1
        j = i // n_mixers
        npre, npost = norm_pre_mix[i][None, :], norm_post_mix[i][None, :]
        if i % n_mixers == 0:
            ps, w_p = pool_scale[j][None, :], bf(w_pool[j])
            if i in ffn_w:
                x = _pool_call(x, mod_x[i], npre, npost, w_p, ps, tm_x)
            else:
                x, *cast = _pool_call(x, mod_x[i], npre, npost, w_p, ps, tm_x, casts=ffn_casts(i))
                ffn_w[i] = tuple(cast)
            if not last:
                ctx = _pool_call(ctx, mod_c[i], npre, npost, w_p, ps, _token_tile(lc, ROW_TILE))
        else:
            w_in, w_out = qkv_weights(w_qkv[j]), bf(w_o[j])
            hosted = [i] + ([i + 1] if i + 1 < depth and (i + 1) % n_mixers == 0 else [])
            q, k, v, *cast = _qkv_call(x, mod_x[i], npre, w_in, q_dim, tables, tm_x,
                                       casts=[job for m in hosted for job in ffn_casts(m)])
            for idx, m in enumerate(hosted):
                ffn_w[m] = (cast[2 * idx], cast[2 * idx + 1])
            qc, kc, vct = _qkv_call(flat(ctx), mod_c[i][:1], npre, w_in, q_dim, None, tm_c)
            x = _attn_oproj_call(q, k, v, kc, vct, attn_sinks[j], x, mod_x[i], npost, w_out, lc,
                                 ATTN_BLOCKS_PER_STEP if nb % ATTN_BLOCKS_PER_STEP == 0 else 1)
            if not last:
                ac = _ctx_attn_call(qc, kc, vct, attn_sinks[j], lc)
                ctx = unflat(_oproj_call(ac, flat(ctx), mod_c[i][:1], npost, w_out, tm_c))

        npre, npost = norm_pre_ffn[i][None, :], norm_post_ffn[i][None, :]
        w_gu, w_dn = ffn_w.pop(i)
        x = _ffn_call(x, mod_x[i], npre, npost, w_gu, w_dn, tm_ffn)
        if not last:
            ctx = unflat(_ffn_call(flat(ctx), mod_c[i][:1], npre, npost, w_gu, w_dn, tm_c_ffn))
    return x
```

```python
import functools

import jax
import jax.numpy as jnp
from jax import lax
from jax.experimental import pallas as pl
from jax.experimental.pallas import tpu as pltpu

GRID_W = 64
POOL_WINDOWS = (2, 4, 8, 16)
HEAD_DIM = 64
GQA_GROUP = 8
WINDOW = 128
Q_BLOCK = WINDOW
ROPE_BASE = 10000.0
RMS_EPS = 1e-6
NEG_INF = -1e30
LOG2E = 1.4426950408889634

LANES = 128
SUBLANES = 8
BF16_ROWS = 16
MXU_COLS = 256
VMEM_LIMIT_BYTES = 56 * 1024 * 1024
SWIGLU_VMEM_LIMIT_BYTES = 63 * 1024 * 1024

ROW_TILE = 512
SWIGLU_ROW_TILE = 1024
SWIGLU_CHUNK = 512
ADA_COLS = 1024
ATTN_BLOCKS_PER_STEP = 4

ROW_SLAB = BF16_ROWS
POOL_HALO = SUBLANES
HEAD_PAIR = 2 * HEAD_DIM

F32 = jnp.float32
BF16 = jnp.bfloat16


def _params(*semantics, vmem_limit_bytes=VMEM_LIMIT_BYTES):
    return pltpu.CompilerParams(dimension_semantics=semantics, vmem_limit_bytes=vmem_limit_bytes)


def _resident(shape):
    nd = len(shape)
    return pl.BlockSpec(shape, lambda *_: (0,) * nd, pipeline_mode=pl.Buffered(1))


def _rms(x, w):
    ms = jnp.mean(x * x, axis=-1, keepdims=True)
    return x * lax.rsqrt(ms + RMS_EPS) * w


def _norm_mod(x, w, shift, scale):
    return _rms(x, w * (1.0 + scale)) + shift


def _cast_side_jobs(casts, tiles, n_steps):
    in_specs, out_specs, out_shape = [], [], []
    for w, layer, chunk in casts:
        _, rows, cols = w.shape
        n_blocks = n_steps
        while rows % (n_blocks * BF16_ROWS):
            n_blocks //= 2
        hold, br = n_steps // n_blocks, rows // n_blocks
        in_specs.append(pl.BlockSpec((None, br, cols),
                                     lambda i, t, layer=layer, hold=hold: (layer, (i * tiles + t) // hold, 0)))
        if chunk is None:
            out_specs.append(pl.BlockSpec((br, cols), lambda i, t, hold=hold: ((i * tiles + t) // hold, 0)))
            out_shape.append(jax.ShapeDtypeStruct((rows, cols), BF16))
        else:
            out_specs.append(pl.BlockSpec((cols // chunk, br, chunk),
                                          lambda i, t, hold=hold: (0, (i * tiles + t) // hold, 0)))
            out_shape.append(jax.ShapeDtypeStruct((cols // chunk, rows, chunk), BF16))
    return in_specs, out_specs, out_shape


def _run_casts(cast_in, cast_out):
    for src, dst in zip(cast_in, cast_out):
        if len(dst.shape) == 2:
            dst[...] = src[...].astype(BF16)
        else:
            chunk = dst.shape[2]
            for c in range(dst.shape[0]):
                dst[c] = src[:, c * chunk:(c + 1) * chunk].astype(BF16)


def _ada_kernel(c_ref, w_ref, b_ref, o_ref):
    s = jax.nn.silu(c_ref[...]).astype(BF16)
    o_ref[0] = jnp.dot(s, w_ref[0].astype(BF16), preferred_element_type=F32) + b_ref[0]


def _ada_call(c_rows, w_ada, b_ada):
    depth, d, n = w_ada.shape
    rows = c_rows.shape[0]
    tn = ADA_COLS
    return pl.pallas_call(
        _ada_kernel,
        grid=(depth, n // tn),
        in_specs=[
            pl.BlockSpec((rows, d), lambda i, j: (0, 0)),
            pl.BlockSpec((1, d, tn), lambda i, j: (i, 0, j)),
            pl.BlockSpec((1, 1, tn), lambda i, j: (i, 0, j)),
        ],
        out_specs=pl.BlockSpec((1, rows, tn), lambda i, j: (i, 0, j)),
        out_shape=jax.ShapeDtypeStruct((depth, rows, n), F32),
        compiler_params=_params("parallel", "parallel"),
        name="ada_proj",
    )(c_rows, w_ada, b_ada.reshape(depth, 1, n))


def _ffn_kernel(x_ref, mod_ref, npre_ref, npost_ref, wg_ref, wu_ref, wd_ref, o_ref, h_ref):
    k = pl.program_id(2)
    last = pl.num_programs(2) - 1
    tm = x_ref.shape[1]

    def prologue():
        w = npre_ref[...] * (1.0 + mod_ref[0, 4:5, :])
        shift = mod_ref[0, 3:4, :]
        for i in range(tm // ROW_SLAB):
            rows = pl.ds(i * ROW_SLAB, ROW_SLAB)
            h_ref[rows, :] = (_rms(x_ref[0, rows, :], w) + shift).astype(BF16)

    def chunk(first):
        h = h_ref[...]
        g = jnp.dot(h, wg_ref[...], preferred_element_type=F32)
        u = jnp.dot(h, wu_ref[...], preferred_element_type=F32)
        a = (jax.nn.silu(g) * u).astype(BF16)
        part = jnp.dot(a, wd_ref[...], preferred_element_type=F32)
        if first:
            o_ref[0] = part
        else:
            o_ref[0] += part

    def epilogue():
        w = mod_ref[0, 5:6, :] * npost_ref[...]
        for i in range(tm // ROW_SLAB):
            rows = pl.ds(i * ROW_SLAB, ROW_SLAB)
            o_ref[0, rows, :] = x_ref[0, rows, :] + _rms(o_ref[0, rows, :], w)

    @pl.when(k == 0)
    def _():
        prologue()
        chunk(True)

    @pl.when((k > 0) & (k < last))
    def _():
        chunk(False)

    @pl.when(k == last)
    def _():
        chunk(False)
        epilogue()


def _ffn_call(x, mod, npre, npost, w_gate_up, w_down, tm):
    b, l, d = x.shape
    f = w_down.shape[0]
    tf = w_gate_up.shape[2]
    nk = f // tf
    return pl.pallas_call(
        _ffn_kernel,
        grid=(b, l // tm, nk),
        in_specs=[
            pl.BlockSpec((1, tm, d), lambda i, t, k: (i, t, 0)),
            pl.BlockSpec((1, 6, d), lambda i, t, k: (i, 0, 0)),
            pl.BlockSpec((1, d), lambda i, t, k: (0, 0)),
            pl.BlockSpec((1, d), lambda i, t, k: (0, 0)),
            pl.BlockSpec((None, d, tf), lambda i, t, k: (k, 0, 0)),
            pl.BlockSpec((None, d, tf), lambda i, t, k: (nk + k, 0, 0)),
            pl.BlockSpec((tf, d), lambda i, t, k: (k, 0)),
        ],
        out_specs=pl.BlockSpec((1, tm, d), lambda i, t, k: (i, t, 0)),
        out_shape=jax.ShapeDtypeStruct((b, l, d), F32),
        scratch_shapes=[pltpu.VMEM((tm, d), BF16)],
        compiler_params=_params("parallel", "parallel", "arbitrary", vmem_limit_bytes=SWIGLU_VMEM_LIMIT_BYTES),
        name="swiglu",
    )(x, mod, npre, npost, w_gate_up, w_gate_up, w_down)


def _pool_kernel(x_ref, xp_ref, xn_ref, mod_ref, npre_ref, npost_ref, wp_ref, ps_ref, *rest, seq_len, n_casts):
    o_ref, h_ref = rest[n_casts], rest[-1]
    _run_casts(rest[:n_casts], rest[n_casts + 1:-1])
    t = pl.program_id(1)
    tm = x_ref.shape[1]
    gd = wp_ref.shape[1]
    d = x_ref.shape[2]
    npre = npre_ref[...]
    shift, scale = mod_ref[0, 0:1, :], mod_ref[0, 1:2, :]
    span = tm + 2 * POOL_HALO
    n_groups = len(POOL_WINDOWS)

    def put(rows, value):
        for g in range(n_groups):
            h_ref[g, rows, :] = value[:, g * gd:(g + 1) * gd]

    hp = _norm_mod(xp_ref[0], npre, shift, scale)
    hn = _norm_mod(xn_ref[0], npre, shift, scale)
    put(pl.ds(0, POOL_HALO), jnp.where(t > 0, hp, 0.0))
    w_mod = npre * (1.0 + scale)
    for i in range(tm // ROW_SLAB):
        put(pl.ds(POOL_HALO + i * ROW_SLAB, ROW_SLAB),
            _rms(x_ref[0, pl.ds(i * ROW_SLAB, ROW_SLAB), :], w_mod) + shift)
    put(pl.ds(POOL_HALO + tm, POOL_HALO), jnp.where(t < pl.num_programs(1) - 1, hn, 0.0))

    pos = t * tm + lax.broadcasted_iota(jnp.int32, (tm, LANES), 0)
    ss = jnp.zeros((tm, 1), F32)
    for g, w in enumerate(POOL_WINDOWS):
        cols = pl.ds(g * gd, gd)
        cur = h_ref[g]
        cur = cur + pltpu.roll(cur, 1, 0)
        reach = 1
        while 2 * reach < w:
            cur = pltpu.roll(cur, reach, 0) + pltpu.roll(cur, span - reach, 0)
            reach *= 2
        acc = cur[POOL_HALO:POOL_HALO + tm]
        cnt = jnp.minimum(pos + w // 2, seq_len) - jnp.maximum(pos - w // 2, 0)
        inv_cnt = jnp.concatenate([1.0 / cnt.astype(F32)] * (gd // LANES), axis=1)
        p = acc * inv_cnt - h_ref[g, pl.ds(POOL_HALO, tm), :]
        y = jnp.dot(p.astype(BF16), wp_ref[g], preferred_element_type=F32) * ps_ref[:, cols]
        o_ref[0, :, cols] = y
        ss = ss + jnp.sum(y * y, axis=-1, keepdims=True)

    rstd = lax.rsqrt(ss * (1.0 / d) + RMS_EPS)
    o_ref[0] = x_ref[0] + o_ref[0] * rstd * (mod_ref[0, 2:3, :] * npost_ref[...])


def _pool_call(x, mod, npre, npost, w_pool, pool_scale, tm, casts=()):
    b, l, d = x.shape
    hb = tm // POOL_HALO
    last_hb = l // POOL_HALO - 1
    tiles = l // tm
    cast_in, cast_out, cast_shape = _cast_side_jobs(casts, tiles, b * tiles)
    out = pl.pallas_call(
        functools.partial(_pool_kernel, seq_len=l, n_casts=len(casts)),
        grid=(b, tiles),
        in_specs=[
            pl.BlockSpec((1, tm, d), lambda i, t: (i, t, 0)),
            pl.BlockSpec((1, POOL_HALO, d), lambda i, t: (i, jnp.maximum(t * hb - 1, 0), 0)),
            pl.BlockSpec((1, POOL_HALO, d), lambda i, t: (i, jnp.minimum((t + 1) * hb, last_hb), 0)),
            pl.BlockSpec((1, 6, d), lambda i, t: (i, 0, 0)),
            pl.BlockSpec((1, d), lambda i, t: (0, 0)),
            pl.BlockSpec((1, d), lambda i, t: (0, 0)),
            _resident(w_pool.shape),
            pl.BlockSpec((1, d), lambda i, t: (0, 0)),
        ] + cast_in,
        out_specs=[pl.BlockSpec((1, tm, d), lambda i, t: (i, t, 0))] + cast_out,
        out_shape=[jax.ShapeDtypeStruct((b, l, d), F32)] + cast_shape,
        scratch_shapes=[pltpu.VMEM((w_pool.shape[0], tm + 2 * POOL_HALO, w_pool.shape[1]), F32)],
        compiler_params=_params("arbitrary", "arbitrary"),
        name="pool_mixer",
    )(x, x, x, mod, npre, npost, w_pool, pool_scale, *[job[0] for job in casts])
    return out if casts else out[0]


def _rope(x, cos, sin_signed, low_half):
    out = []
    for j in range(x.shape[1] // LANES):
        c = x[:, j * LANES:(j + 1) * LANES]
        rot = jnp.where(low_half, pltpu.roll(c, LANES - HEAD_DIM // 4, 1), pltpu.roll(c, HEAD_DIM // 4, 1))
        out.append(c * cos + rot * sin_signed)
    return jnp.concatenate(out, axis=1)


def _qkv_kernel(*refs, rope, q_dim, k_dim, n_casts):
    n_in = (6 if rope else 4) + n_casts
    cast_in, cast_out = refs[n_in - n_casts:n_in], refs[n_in + 3:]
    if rope:
        x_ref, mod_ref, npre_ref, w_ref, cos_ref, sin_ref = refs[:6]
    else:
        x_ref, mod_ref, npre_ref, w_ref = refs[:4]
    q_ref, k_ref, vt_ref = refs[n_in:n_in + 3]
    _run_casts(cast_in, cast_out)
    h = _norm_mod(x_ref[0], npre_ref[...], mod_ref[0, 0:1, :], mod_ref[0, 1:2, :]).astype(BF16)
    qkv = jnp.dot(h, w_ref[...], preferred_element_type=F32)
    q = qkv[:, :q_dim]
    k = qkv[:, q_dim:q_dim + k_dim]
    v = qkv[:, q_dim + k_dim:]
    if rope:
        cos, sin_signed = cos_ref[...], sin_ref[...]
        lane = lax.broadcasted_iota(jnp.int32, cos.shape, 1)
        low_half = (lane % (HEAD_DIM // 2)) < (HEAD_DIM // 4)
        q = _rope(q, cos, sin_signed, low_half)
        k = _rope(k, cos, sin_signed, low_half)
    q_ref[0] = (q * (HEAD_DIM ** -0.5 * LOG2E)).astype(BF16)
    k_ref[0] = k.astype(BF16)
    vt_ref[0] = v.T.astype(BF16)


def _qkv_call(x, mod, npre, w_qkv_dup, q_dim, rope_tables, tm, casts=()):
    b, l, d = x.shape
    n = w_qkv_dup.shape[1]
    v_dim = (n - q_dim) // 3
    k_dim = 2 * v_dim
    rope = rope_tables is not None
    in_specs = [
        pl.BlockSpec((1, tm, d), lambda i, t: (i, t, 0)),
        pl.BlockSpec((1, 6, d), lambda i, t: (i, 0, 0)),
        pl.BlockSpec((1, d), lambda i, t: (0, 0)),
        _resident(w_qkv_dup.shape),
    ]
    args = [x, mod, npre, w_qkv_dup]
    if rope:
        in_specs += [pl.BlockSpec((tm, LANES), lambda i, t: (t, 0))] * 2
        args += list(rope_tables)
    out_specs = [
        pl.BlockSpec((1, tm, q_dim), lambda i, t: (i, t, 0)),
        pl.BlockSpec((1, tm, k_dim), lambda i, t: (i, t, 0)),
        pl.BlockSpec((1, v_dim, tm), lambda i, t: (i, 0, t)),
    ]
    out_shape = [
        jax.ShapeDtypeStruct((b, l, q_dim), BF16),
        jax.ShapeDtypeStruct((b, l, k_dim), BF16),
        jax.ShapeDtypeStruct((b, v_dim, l), BF16),
    ]
    tiles = l // tm
    cast_in, cast_out, cast_shape = _cast_side_jobs(casts, tiles, b * tiles)
    in_specs, out_specs, out_shape = in_specs + cast_in, out_specs + cast_out, out_shape + cast_shape
    args += [job[0] for job in casts]
    return pl.pallas_call(
        functools.partial(_qkv_kernel, rope=rope, q_dim=q_dim, k_dim=k_dim, n_casts=len(casts)),
        grid=(b, tiles),
        in_specs=in_specs,
        out_specs=out_specs,
        out_shape=out_shape,
        compiler_params=_params("arbitrary", "arbitrary"),
        name="qkv_rope" if rope else "qkv_ctx",
    )(*args)


def _attend_scores(q_ref, sink_ref, h, k, row0, nq):
    low = lax.broadcasted_iota(jnp.int32, (nq, HEAD_PAIR), 1) < HEAD_DIM
    zero = jnp.zeros((), BF16)
    base = h * GQA_GROUP * HEAD_DIM
    q_rows, sink_cols = [], []
    for g in range(GQA_GROUP):
        pair = q_ref[0, pl.ds(row0, nq), pl.ds(base + (g // 2) * HEAD_PAIR, HEAD_PAIR)]
        q_rows.append(jnp.where(low if g % 2 == 0 else ~low, pair, zero))
        sink_cols.append(jnp.full((1, nq), sink_ref[h * GQA_GROUP + g] * LOG2E, F32))
    qg = jnp.concatenate(q_rows, axis=0)
    snk = jnp.concatenate(sink_cols, axis=1)
    s = lax.dot_general(k, qg, (((1,), (1,)), ((), ())), preferred_element_type=F32)
    return s, snk


def _attend_softmax(s, snk, masks):
    blocks, row = [], 0
    for first, bias in masks:
        if first > row:
            blocks.append(s[row:first])
        row = first + bias.shape[0]
        blocks.append(s[first:row] + bias)
    if masks:
        if row < s.shape[0]:
            blocks.append(s[row:])
        s = jnp.concatenate(blocks, axis=0)
    m = jnp.maximum(snk, jnp.max(s, axis=0, keepdims=True))
    return jnp.exp2(s - m).astype(BF16), jnp.exp2(snk - m)


def _attend_values(o_ref, h, p, sink_p, vt, row0, nq):
    base = h * GQA_GROUP * HEAD_DIM
    vt_ones = jnp.concatenate([vt, jnp.ones((BF16_ROWS, vt.shape[1]), BF16)], axis=0)
    acc = jnp.dot(vt_ones, p, preferred_element_type=F32)
    den = sink_p + acc[HEAD_DIM:HEAD_DIM + 1]
    acc = acc[:HEAD_DIM] * (1.0 / den)
    for j in range(GQA_GROUP // 2):
        pair_t = jnp.concatenate([acc[:, (2 * j) * nq:(2 * j + 1) * nq], acc[:, (2 * j + 1) * nq:(2 * j + 2) * nq]],
                                 axis=0)
        o_ref[0, pl.ds(row0, nq), pl.ds(base + j * HEAD_PAIR, HEAD_PAIR)] = pair_t.T.astype(o_ref.dtype)


def _attn_oproj_kernel(sink_ref, q_ref, kp_ref, kc_ref, kn_ref, vp_ref, vc_ref, vn_ref, kx_ref, vx_ref,
                       x_ref, mod_ref, npost_ref, w_ref, o_ref, a_scr, y_scr, *, steps_per_seq):
    n = pl.program_id(0)
    n_tiles = pl.num_programs(0) - 1
    slot = n % 2
    nq = Q_BLOCK
    rows = q_ref.shape[1]
    n_sub = rows // nq
    piece = MXU_COLS

    def attention(fillers):
        t = n % steps_per_seq
        shape = (nq, GQA_GROUP * nq)
        key = lax.broadcasted_iota(jnp.int32, shape, 0)
        qry = lax.broadcasted_iota(jnp.int32, shape, 1) % nq
        inner_prev = jnp.where(key >= qry, 0.0, NEG_INF)
        inner_next = jnp.where(key <= qry, 0.0, NEG_INF)
        first_prev = jnp.where((key >= qry) & (t > 0), 0.0, NEG_INF)
        last_next = jnp.where((key <= qry) & (t < steps_per_seq - 1), 0.0, NEG_INF)
        k_loc = jnp.concatenate([kp_ref[0], kc_ref[0], kn_ref[0]], axis=0)
        vt_loc = jnp.concatenate([vp_ref[0], vc_ref[0], vn_ref[0]], axis=1)
        n_kv = vt_loc.shape[0] // HEAD_DIM
        a_out = a_scr.at[pl.ds(slot, 1)]

        units = []
        for u in range(n_sub):
            k_all = jnp.concatenate([k_loc[u * nq:(u + 3) * nq], kx_ref[0]], axis=0)
            vt_all = jnp.concatenate([vt_loc[:, u * nq:(u + 3) * nq], vx_ref[0]], axis=1)
            masks = [(0, first_prev if u == 0 else inner_prev),
                     (2 * nq, last_next if u == n_sub - 1 else inner_next)]
            for h in range(n_kv):
                units.append((u * nq, h, k_all[:, h * HEAD_PAIR:(h + 1) * HEAD_PAIR],
                              vt_all[h * HEAD_DIM:(h + 1) * HEAD_DIM, :], masks))
        scores, probs = {}, {}
        for i in range(len(units) + 2):
            if i < len(units):
                row0, h, k, _, _ = units[i]
                scores[i] = _attend_scores(q_ref, sink_ref, h, k, row0, nq)
            if 0 <= i - 2 < len(units):
                row0, h, _, vt, _ = units[i - 2]
                _attend_values(a_out, h, *probs.pop(i - 2), vt, row0, nq)
            if 0 <= i - 1 < len(units):
                probs[i - 1] = _attend_softmax(*scores.pop(i - 1), units[i - 1][4])
            if i < len(fillers):
                fillers[i]()
        for f in fillers[len(units) + 2:]:
            f()

    def project(j):
        def run():
            cols = pl.ds(j * piece, piece)
            y_scr[:, cols] = jnp.dot(a_scr[1 - slot], w_ref[:, cols], preferred_element_type=F32)
        return run

    def finish():
        w = mod_ref[0, 2:3, :] * npost_ref[...]
        for i in range(rows // ROW_SLAB):
            sl = pl.ds(i * ROW_SLAB, ROW_SLAB)
            o_ref[0, sl, :] = x_ref[0, sl, :] + _rms(y_scr[sl, :], w)

    pieces = [project(j) for j in range(w_ref.shape[1] // piece)]

    @pl.when(n == 0)
    def _():
        attention([])

    @pl.when((n > 0) & (n < n_tiles))
    def _():
        attention(pieces)
        finish()

    @pl.when(n == n_tiles)
    def _():
        for f in pieces:
            f()
        finish()


def _attn_oproj_call(q, k, vt, kx, vxt, sinks, x, mod, npost, w_o, n_ctx, n_sub):
    b, l, qd = q.shape
    d = x.shape[2]
    kd, vd = k.shape[2], vt.shape[1]
    nb = l // Q_BLOCK
    rows = n_sub * Q_BLOCK
    spb = nb // n_sub
    n_tiles = b * spb
    cur = lambda n: jnp.minimum(n, n_tiles - 1)
    done = lambda n: jnp.maximum(n - 1, 0)
    prev = lambda t: jnp.maximum(t * n_sub - 1, 0)
    nxt = lambda t: jnp.minimum((t + 1) * n_sub, nb - 1)
    k_halo = lambda f: pl.BlockSpec((1, Q_BLOCK, kd), lambda n: (cur(n) // spb, f(cur(n) % spb), 0))
    v_halo = lambda f: pl.BlockSpec((1, vd, Q_BLOCK), lambda n: (cur(n) // spb, 0, f(cur(n) % spb)))
    return pl.pallas_call(
        functools.partial(_attn_oproj_kernel, steps_per_seq=spb),
        grid=(n_tiles + 1,),
        in_specs=[
            pl.BlockSpec(memory_space=pltpu.SMEM),
            pl.BlockSpec((1, rows, qd), lambda n: (cur(n) // spb, cur(n) % spb, 0)),
            k_halo(prev), pl.BlockSpec((1, rows, kd), lambda n: (cur(n) // spb, cur(n) % spb, 0)), k_halo(nxt),
            v_halo(prev), pl.BlockSpec((1, vd, rows), lambda n: (cur(n) // spb, 0, cur(n) % spb)), v_halo(nxt),
            pl.BlockSpec((1, n_ctx, kd), lambda n: (0, cur(n) // spb, 0)),
            pl.BlockSpec((1, vd, n_ctx), lambda n: (0, 0, cur(n) // spb)),
            pl.BlockSpec((1, rows, d), lambda n: (done(n) // spb, done(n) % spb, 0)),
            pl.BlockSpec((1, 6, d), lambda n: (done(n) // spb, 0, 0)),
            pl.BlockSpec((1, d), lambda n: (0, 0)),
            _resident(w_o.shape),
        ],
        out_specs=pl.BlockSpec((1, rows, d), lambda n: (done(n) // spb, done(n) % spb, 0)),
        out_shape=jax.ShapeDtypeStruct(x.shape, F32),
        scratch_shapes=[pltpu.VMEM((2, rows, qd), BF16), pltpu.VMEM((rows, d), F32)],
        compiler_params=_params("arbitrary"),
        name="window_attn_proj",
    )(sinks, q, k, k, k, vt, vt, vt, kx, vxt, x, mod, npost, w_o)


def _ctx_attn_kernel(sink_ref, q_ref, k_ref, vt_ref, o_ref):
    nq = q_ref.shape[1]
    for h in range(vt_ref.shape[1] // HEAD_DIM):
        sc = _attend_scores(q_ref, sink_ref, h, k_ref[0, :, h * HEAD_PAIR:(h + 1) * HEAD_PAIR], 0, nq)
        _attend_values(o_ref, h, *_attend_softmax(*sc, []), vt_ref[0, h * HEAD_DIM:(h + 1) * HEAD_DIM, :], 0, nq)


def _ctx_attn_call(q, k, vt, sinks, n_ctx):
    qd, kd, vd = q.shape[2], k.shape[2], vt.shape[1]
    return pl.pallas_call(
        _ctx_attn_kernel,
        grid=(q.shape[1] // n_ctx,),
        in_specs=[
            pl.BlockSpec(memory_space=pltpu.SMEM),
            pl.BlockSpec((1, n_ctx, qd), lambda i: (0, i, 0)),
            pl.BlockSpec((1, n_ctx, kd), lambda i: (0, i, 0)),
            pl.BlockSpec((1, vd, n_ctx), lambda i: (0, 0, i)),
        ],
        out_specs=pl.BlockSpec((1, n_ctx, qd), lambda i: (0, i, 0)),
        out_shape=jax.ShapeDtypeStruct(q.shape, BF16),
        compiler_params=_params("parallel"),
        name="ctx_attn",
    )(sinks, q, k, vt)


def _oproj_kernel(a_ref, x_ref, mod_ref, npost_ref, w_ref, o_ref):
    y = jnp.dot(a_ref[0], w_ref[...], preferred_element_type=F32)
    o_ref[0] = x_ref[0] + _rms(y, mod_ref[0, 2:3, :] * npost_ref[...])


def _oproj_call(a, x, mod, npost, w_o, tm):
    b, l, d = x.shape
    ad = a.shape[2]
    return pl.pallas_call(
        _oproj_kernel,
        grid=(b, l // tm),
        in_specs=[
            pl.BlockSpec((1, tm, ad), lambda i, t: (i, t, 0)),
            pl.BlockSpec((1, tm, d), lambda i, t: (i, t, 0)),
            pl.BlockSpec((1, 6, d), lambda i, t: (i, 0, 0)),
            pl.BlockSpec((1, d), lambda i, t: (0, 0)),
            _resident(w_o.shape),
        ],
        out_specs=pl.BlockSpec((1, tm, d), lambda i, t: (i, t, 0)),
        out_shape=jax.ShapeDtypeStruct((b, l, d), F32),
        compiler_params=_params("parallel", "parallel"),
        name="attn_out_proj",
    )(a, x, mod, npost, w_o)


def _rope_tables(l):
    axis_dim = HEAD_DIM // 2
    rows_n = l // GRID_W
    row = jnp.repeat(jnp.arange(rows_n), GRID_W).astype(F32)
    col = jnp.tile(jnp.arange(GRID_W), rows_n).astype(F32)
    inv = 1.0 / (ROPE_BASE ** (jnp.arange(0, axis_dim, 2, dtype=F32) / axis_dim))
    ang_r = row[:, None] * inv[None, :]
    ang_c = col[:, None] * inv[None, :]
    ang = jnp.concatenate([ang_r, ang_r, ang_c, ang_c], axis=-1)
    sign = jnp.tile(jnp.concatenate([-jnp.ones(axis_dim // 2, F32), jnp.ones(axis_dim // 2, F32)]), 2)
    cos, sin = jnp.cos(ang), jnp.sin(ang) * sign[None, :]
    return jnp.tile(cos, (1, LANES // HEAD_DIM)), jnp.tile(sin, (1, LANES // HEAD_DIM))


def _dup_heads(w, n_heads):
    d = w.shape[0]
    w = w.reshape(d, n_heads, 1, HEAD_DIM)
    return jnp.broadcast_to(w, (d, n_heads, 2, HEAD_DIM)).reshape(d, n_heads * HEAD_PAIR)


def _token_tile(l, target):
    return min(l, target)


def kernel(x, c, ctx, c_ctx, w_ada, b_ada, norm_pre_mix, norm_post_mix, norm_pre_ffn, norm_post_ffn,
           w_pool, pool_scale, w_qkv, w_o, attn_sinks, w_gate_up, w_down):
    b, l, d = x.shape
    depth = w_ada.shape[0]
    n_mixers = 2
    q_dim = w_o.shape[1]
    kv_heads = (w_qkv.shape[2] - q_dim) // (2 * HEAD_DIM)
    kv_dim = kv_heads * HEAD_DIM

    rows = -(-(b + 1) // BF16_ROWS) * BF16_ROWS
    c_rows = jnp.concatenate([c, c_ctx[None, :], jnp.zeros((rows - b - 1, d), F32)], axis=0)
    ada = _ada_call(c_rows, w_ada, b_ada).reshape(depth, rows, 6, d)
    mod_x = ada[:, :b]
    mod_c = jnp.broadcast_to(ada[:, b:b + 1], (depth, b, 6, d))

    bf = lambda w: w.astype(BF16)
    ffn_w = {}
    ffn_casts = lambda m: [(w_gate_up, m, SWIGLU_CHUNK), (w_down, m, None)]

    def qkv_weights(w):
        w = bf(w)
        return jnp.concatenate([w[:, :q_dim], _dup_heads(w[:, q_dim:q_dim + kv_dim], kv_heads),
                                w[:, q_dim + kv_dim:]], axis=-1)

    tables = _rope_tables(l)

    lc = ctx.shape[1]
    nb = l // Q_BLOCK
    tm_x, tm_ffn = _token_tile(l, ROW_TILE), _token_tile(l, SWIGLU_ROW_TILE)
    tm_c, tm_c_ffn = _token_tile(b * lc, ROW_TILE), _token_tile(b * lc, SWIGLU_ROW_TILE)

    flat = lambda a: a.reshape(1, b * lc, a.shape[-1])
    unflat = lambda a: a.reshape(b, lc, a.shape[-1])

    for i in range(depth):
        last = i == depth - 1
        j = i // n_mixers
        npre, npost = norm_pre_mix[i][None, :], norm_post_mix[i][None, :]
        if i % n_mixers == 0:
            ps, w_p = pool_scale[j][None, :], bf(w_pool[j])
            if i in ffn_w:
                x = _pool_call(x, mod_x[i], npre, npost, w_p, ps, tm_x)
            else:
                x, *cast = _pool_call(x, mod_x[i], npre, npost, w_p, ps, tm_x, casts=ffn_casts(i))
                ffn_w[i] = tuple(cast)
            if not last:
                ctx = _pool_call(ctx, mod_c[i], npre, npost, w_p, ps, _token_tile(lc, ROW_TILE))
        else:
            w_in, w_out = qkv_weights(w_qkv[j]), bf(w_o[j])
            hosted = [i] + ([i + 1] if i + 1 < depth and (i + 1) % n_mixers == 0 else [])
            q, k, v, *cast = _qkv_call(x, mod_x[i], npre, w_in, q_dim, tables, tm_x,
                                       casts=[job for m in hosted for job in ffn_casts(m)])
            for idx, m in enumerate(hosted):
                ffn_w[m] = (cast[2 * idx], cast[2 * idx + 1])
            qc, kc, vct = _qkv_call(flat(ctx), mod_c[i][:1], npre, w_in, q_dim, None, tm_c)
            x = _attn_oproj_call(q, k, v, kc, vct, attn_sinks[j], x, mod_x[i], npost, w_out, lc,
                                 ATTN_BLOCKS_PER_STEP if nb % ATTN_BLOCKS_PER_STEP == 0 else 1)
            if not last:
                ac = _ctx_attn_call(qc, kc, vct, attn_sinks[j], lc)
                ctx = unflat(_oproj_call(ac, flat(ctx), mod_c[i][:1], npost, w_out, tm_c))

        npre, npost = norm_pre_ffn[i][None, :], norm_post_ffn[i][None, :]
        w_gu, w_dn = ffn_w.pop(i)
        x = _ffn_call(x, mod_x[i], npre, npost, w_gu, w_dn, tm_ffn)
        if not last:
            ctx = unflat(_ffn_call(flat(ctx), mod_c[i][:1], npre, npost, w_gu, w_dn, tm_c_ffn))
    return x
```
